```python
import jax, jax.numpy as jnp
from jax import lax
import numpy as np

D_MODEL = 1024
BATCH = 16
SEQ = 2048
DEPTH = 2

RWKV_HEADS = 6
RWKV_HEAD_DIM = 64
RWKV_WIDTH = RWKV_HEADS * RWKV_HEAD_DIM
W_LORA = 64
A_LORA = 64
G_LORA = 128
RET_HEADS = 6
RET_HEAD_DIM = 64
RET_WIDTH = RET_HEADS * RET_HEAD_DIM
RET_CHUNK = 128
POOL_GROUPS = 4
POOL_GROUP_DIM = 64
POOL_WIDTH = POOL_GROUPS * POOL_GROUP_DIM
POOL_WINDOWS = (2, 4, 8, 16)
MIX_WIDTH = RWKV_WIDTH + RET_WIDTH + POOL_WIDTH
RWKV_COLS = 3 * RWKV_WIDTH + W_LORA + A_LORA + G_LORA
RET_COLS = 4 * RET_WIDTH
IN_COLS = RWKV_COLS + RET_COLS + POOL_WIDTH
N_EXPERTS = 64
TOP_K = 8
N_GROUPS = 8
TOPK_GROUPS = 4
EXPERT_HIDDEN = 256
SHARED_HIDDEN = 256
ROUTED_SCALE = 2.5
MOE_BLOCK = 256
NORM_EPS = 1e-6
RWKV_LNX_EPS = 64e-5
RET_NORM_EPS = 1e-6
ROPE_BASE = 10000.0

kernel_name = 'hybrid_rwkv7_retention_pool_moe_adaln'


def rms_norm(x, g):
    xf = x.astype(jnp.float32)
    y = xf * lax.rsqrt(jnp.mean(xf * xf, -1, keepdims=True) + NORM_EPS)
    return (y * g.astype(jnp.float32)).astype(x.dtype)


def token_shift(p, mu):
    prev = jnp.pad(p, ((0, 0), (1, 0), (0, 0)))[:, :-1]
    return p + (prev - p) * mu


def group_norm(y, w, b, eps):
    B, T, H, d = y.shape
    mu = jnp.mean(y, -1, keepdims=True)
    var = jnp.mean(jnp.square(y - mu), -1, keepdims=True)
    yn = (y - mu) * lax.rsqrt(var + eps)
    return yn.reshape(B, T, H * d) * w + b


def rwkv7_scan(r, decay, k, v, kk, a):
    def step(S, inp):
        r_t, w_t, k_t, v_t, kk_t, a_t = inp
        sa = jnp.einsum('bhij,bhj->bhi', S, -kk_t)
        S = (S * w_t[:, :, None, :] + sa[..., None] * (kk_t * a_t)[:, :, None, :]
             + v_t[..., None] * k_t[:, :, None, :])
        return S, jnp.einsum('bhij,bhj->bhi', S, r_t)
    B, T, H, d = r.shape
    xs = tuple(jnp.moveaxis(t, 1, 0) for t in (r, decay, k, v, kk, a))
    _, ys = lax.scan(step, jnp.zeros((B, H, d, d), jnp.float32), xs)
    return jnp.moveaxis(ys, 0, 1)


def rwkv7_mixer(p, mu, w_up, w0, a_up, a0, g_up, k_k, k_a, r_k, lnx_w, lnx_b):
    B, T, _ = p.shape
    H, Dh, R = RWKV_HEADS, RWKV_HEAD_DIM, RWKV_WIDTH
    p = token_shift(p.astype(jnp.float32), mu)
    r = p[..., 0:R]
    k = p[..., R:2 * R]
    v = p[..., 2 * R:3 * R]
    o = 3 * R
    xw = p[..., o:o + W_LORA]
    xa = p[..., o + W_LORA:o + W_LORA + A_LORA]
    xg = p[..., o + W_LORA + A_LORA:]
    z = w0 + jnp.tanh(xw) @ w_up
    log_decay = -jnp.exp(-jax.nn.softplus(-z) - 0.5)
    a = jax.nn.sigmoid(a0 + xa @ a_up)
    g = jax.nn.sigmoid(xg) @ g_up

    def heads(t):
        return t.reshape(B, T, H, Dh)

    kk = heads(k * k_k)
    kk = kk / jnp.maximum(jnp.linalg.norm(kk, axis=-1, keepdims=True), 1e-12)
    k = k * (1.0 + (a - 1.0) * k_a)
    r_h, k_h, v_h, a_h = heads(r), heads(k), heads(v), heads(a)
    y = rwkv7_scan(r_h, jnp.exp(heads(log_decay)), k_h, v_h, kk, a_h)
    y = group_norm(y, lnx_w, lnx_b, RWKV_LNX_EPS)
    bonus = jnp.sum(r_h * k_h * r_k, -1, keepdims=True) * v_h
    return (y + bonus.reshape(B, T, R)) * g


def rotary(t, cos, sin):
    half = t.shape[-1] // 2
    t1, t2 = t[..., :half], t[..., half:]
    return jnp.concatenate([t1 * cos - t2 * sin, t1 * sin + t2 * cos], -1)


def retention_mixer(p):
    B, T, _ = p.shape
    H, d, C = RET_HEADS, RET_HEAD_DIM, RET_CHUNK
    NC = T // C
    p = p.astype(jnp.float32)
    W = RET_WIDTH
    q = p[..., 0:W].reshape(B, T, H, d)
    k = p[..., W:2 * W].reshape(B, T, H, d)
    v = p[..., 2 * W:3 * W].reshape(B, T, H, d)
    gate = p[..., 3 * W:]
    pos = jnp.arange(T, dtype=jnp.float32)
    inv_freq = ROPE_BASE ** (-jnp.arange(0, d, 2, dtype=jnp.float32) / d)
    ang = pos[:, None] * inv_freq[None, :]
    cos, sin = jnp.cos(ang)[:, None, :], jnp.sin(ang)[:, None, :]
    q = rotary(q, cos, sin)
    k = rotary(k, cos, sin) * (d ** -0.5)
    log_gamma = jnp.log1p(-(2.0 ** (-5.0 - jnp.arange(H, dtype=jnp.float32))))
    idx = jnp.arange(C, dtype=jnp.float32)
    diff = idx[:, None] - idx[None, :]
    inner_decay = jnp.where(diff >= 0,
                            jnp.exp(log_gamma[:, None, None] * jnp.maximum(diff, 0.0)), 0.0)
    xi = jnp.exp(log_gamma[:, None] * (idx + 1.0))
    zeta = jnp.exp(log_gamma[:, None] * (C - 1.0 - idx))
    chunk_decay = jnp.exp(log_gamma * C)
    qc = q.reshape(B, NC, C, H, d)
    kc = k.reshape(B, NC, C, H, d)
    vc = v.reshape(B, NC, C, H, d)
    scores = jnp.einsum('bnqhd,bnkhd->bnhqk', qc, kc) * inner_decay
    y_inner = jnp.einsum('bnhqk,bnkhe->bnqhe', scores, vc)
    kv = jnp.einsum('bnkhd,hk,bnkhe->nbhde', kc, zeta, vc)

    def step(R, kv_n):
        return chunk_decay[None, :, None, None] * R + kv_n, R

    _, R_prev = lax.scan(step, jnp.zeros((B, H, d, d), jnp.float32), kv)
    y_cross = jnp.einsum('bnqhd,hq,nbhde->bnqhe', qc, xi, R_prev)
    y = (y_inner + y_cross).reshape(B, T, H, d)
    y = y * lax.rsqrt(jnp.mean(y * y, -1, keepdims=True) + RET_NORM_EPS)
    return jax.nn.silu(gate) * y.reshape(B, T, W)


def pool_mixer(u, pool_w, pool_scale):
    B, T, _ = u.shape
    uf = u.astype(jnp.float32).reshape(B, T, POOL_GROUPS, POOL_GROUP_DIM)
    cs = jnp.cumsum(uf, axis=1)
    t = jnp.arange(T)
    pooled = []
    for gi, win in enumerate(POOL_WINDOWS):
        csg = cs[:, :, gi]
        lagged = jnp.pad(csg, ((0, 0), (win, 0), (0, 0)))[:, :T]
        count = jnp.minimum(t + 1, win).astype(jnp.float32)[None, :, None]
        pooled.append((csg - lagged) / count - uf[:, :, gi])
    pooled = jnp.stack(pooled, axis=2)
    out = jnp.einsum('btgc,gcd->btgd', pooled, pool_w)
    return out.reshape(B, T, POOL_WIDTH) * pool_scale


def swiglu(h, wg, wu, wd):
    return (jax.nn.silu(h @ wg) * (h @ wu)) @ wd


def moe_ffn(h, w_router, router_bias, we_gate, we_up, we_down, ws_gate, ws_up, ws_down):
    B, T, D = h.shape
    N = B * T
    hf = h.reshape(N, D)
    scores = jax.nn.sigmoid((hf @ w_router).astype(jnp.float32))
    choice = scores + router_bias.astype(jnp.float32)
    per_group = N_EXPERTS // N_GROUPS
    group_score = lax.top_k(choice.reshape(N, N_GROUPS, per_group), 2)[0].sum(-1)
    _, gsel = lax.top_k(group_score, TOPK_GROUPS)
    gmask = jax.nn.one_hot(gsel, N_GROUPS, dtype=jnp.float32).sum(-2) > 0
    emask = jnp.repeat(gmask, per_group, axis=-1)
    _, top_idx = lax.top_k(jnp.where(emask, choice, -jnp.inf), TOP_K)
    top_w = jnp.take_along_axis(scores, top_idx, -1)
    top_w = top_w / jnp.sum(top_w, -1, keepdims=True) * ROUTED_SCALE
    NK = N * TOP_K
    flat_e = top_idx.reshape(NK)
    flat_tok = jnp.repeat(jnp.arange(N, dtype=jnp.int32), TOP_K)
    flat_w = top_w.reshape(NK)
    order = jnp.argsort(flat_e)
    se = flat_e[order]
    counts = jnp.bincount(flat_e, length=N_EXPERTS)
    padded = (counts + MOE_BLOCK - 1) // MOE_BLOCK * MOE_BLOCK
    pad_end = jnp.cumsum(padded)
    pad_start = pad_end - padded
    start = jnp.cumsum(counts) - counts
    dest = pad_start[se] + jnp.arange(NK, dtype=jnp.int32) - start[se]
    n_blocks = -(-NK // MOE_BLOCK) + N_EXPERTS
    P = n_blocks * MOE_BLOCK
    row_tok = jnp.full((P,), N, jnp.int32).at[dest].set(flat_tok[order])
    row_w = jnp.zeros((P,), jnp.float32).at[dest].set(flat_w[order])
    block_expert = jnp.minimum(
        jnp.searchsorted(pad_end, jnp.arange(n_blocks, dtype=jnp.int32) * MOE_BLOCK, side='right'),
        N_EXPERTS - 1)
    h_pad = jnp.concatenate([hf, jnp.zeros((1, D), hf.dtype)], 0)

    def run_block(args):
        rows, wts, e = args
        yb = swiglu(h_pad[rows], we_gate[e], we_up[e], we_down[e])
        return yb * wts[:, None].astype(yb.dtype)

    yb = lax.map(run_block, (row_tok.reshape(n_blocks, MOE_BLOCK),
                             row_w.reshape(n_blocks, MOE_BLOCK), block_expert))
    routed = jax.ops.segment_sum(yb.reshape(P, D), row_tok, num_segments=N + 1)[:N]
    shared = swiglu(hf, ws_gate, ws_up, ws_down)
    return (routed + shared).reshape(B, T, D)


def setup_inputs(seed: int = 0) -> dict:
    key = jax.random.key(seed)
    ks = jax.random.split(key, 32)
    L, D = DEPTH, D_MODEL
    f32 = jnp.float32

    def nrm(k, shape, scale):
        return jax.random.normal(k, shape, f32) * scale

    return {
        'x': nrm(ks[0], (BATCH, SEQ, D), 1.0),
        'c': nrm(ks[1], (BATCH, D), 1.0),
        'norm1_g': 1.0 + nrm(ks[2], (L, D), 0.02),
        'norm2_g': 1.0 + nrm(ks[3], (L, D), 0.02),
        'w_ada': nrm(ks[4], (L, D, 6 * D), 0.5 * D ** -0.5),
        'b_ada': nrm(ks[5], (L, 6 * D), 0.02),
        'w_in': nrm(ks[6], (L, D, IN_COLS), D ** -0.5),
        'w_out': nrm(ks[7], (L, MIX_WIDTH, D), MIX_WIDTH ** -0.5),
        'rwkv_mu': jax.random.uniform(ks[8], (L, RWKV_COLS), f32),
        'rwkv_w_up': nrm(ks[9], (L, W_LORA, RWKV_WIDTH), W_LORA ** -0.5),
        'rwkv_w0': jax.random.uniform(ks[10], (L, RWKV_WIDTH), f32, -5.0, -1.0),
        'rwkv_a_up': nrm(ks[11], (L, A_LORA, RWKV_WIDTH), A_LORA ** -0.5),
        'rwkv_a0': nrm(ks[12], (L, RWKV_WIDTH), 0.5),
        'rwkv_g_up': nrm(ks[13], (L, G_LORA, RWKV_WIDTH), G_LORA ** -0.5),
        'rwkv_k_k': 0.85 + nrm(ks[14], (L, RWKV_WIDTH), 0.05),
        'rwkv_k_a': 1.0 + nrm(ks[15], (L, RWKV_WIDTH), 0.05),
        'rwkv_r_k': nrm(ks[16], (L, RWKV_HEADS, RWKV_HEAD_DIM), 0.1),
        'rwkv_lnx_w': 1.0 + nrm(ks[17], (L, RWKV_WIDTH), 0.1),
        'rwkv_lnx_b': nrm(ks[18], (L, RWKV_WIDTH), 0.01),
        'pool_w': nrm(ks[19], (L, POOL_GROUPS, POOL_GROUP_DIM, POOL_GROUP_DIM), POOL_GROUP_DIM ** -0.5),
        'pool_scale': 1.0 + nrm(ks[20], (L, POOL_WIDTH), 0.1),
        'w_router': nrm(ks[21], (L, D, N_EXPERTS), D ** -0.5),
        'router_bias': nrm(ks[22], (L, N_EXPERTS), 0.01),
        'we_gate': nrm(ks[23], (L, N_EXPERTS, D, EXPERT_HIDDEN), D ** -0.5),
        'we_up': nrm(ks[24], (L, N_EXPERTS, D, EXPERT_HIDDEN), D ** -0.5),
        'we_down': nrm(ks[25], (L, N_EXPERTS, EXPERT_HIDDEN, D), EXPERT_HIDDEN ** -0.5),
        'ws_gate': nrm(ks[26], (L, D, SHARED_HIDDEN), D ** -0.5),
        'ws_up': nrm(ks[27], (L, D, SHARED_HIDDEN), D ** -0.5),
        'ws_down': nrm(ks[28], (L, SHARED_HIDDEN, D), SHARED_HIDDEN ** -0.5),
        'final_g': 1.0 + nrm(ks[29], (D,), 0.02),
    }


def reference(x, c, norm1_g, norm2_g, w_ada, b_ada, w_in, w_out, rwkv_mu, rwkv_w_up, rwkv_w0,
              rwkv_a_up, rwkv_a0, rwkv_g_up, rwkv_k_k, rwkv_k_a, rwkv_r_k, rwkv_lnx_w, rwkv_lnx_b,
              pool_w, pool_scale, w_router, router_bias, we_gate, we_up, we_down,
              ws_gate, ws_up, ws_down, final_g):
    mod = jnp.einsum('bd,ldm->lbm', jax.nn.silu(c), w_ada) + b_ada[:, None, :]
    for l in range(DEPTH):
        shift1, scale1, gate1, shift2, scale2, gate2 = jnp.split(mod[l][:, None, :], 6, axis=-1)
        h = rms_norm(x, norm1_g[l]) * (1.0 + scale1) + shift1
        p = h @ w_in[l]
        y_rwkv = rwkv7_mixer(p[..., :RWKV_COLS], rwkv_mu[l], rwkv_w_up[l], rwkv_w0[l],
                             rwkv_a_up[l], rwkv_a0[l], rwkv_g_up[l], rwkv_k_k[l], rwkv_k_a[l],
                             rwkv_r_k[l], rwkv_lnx_w[l], rwkv_lnx_b[l])
        y_ret = retention_mixer(p[..., RWKV_COLS:RWKV_COLS + RET_COLS])
        y_pool = pool_mixer(p[..., RWKV_COLS + RET_COLS:], pool_w[l], pool_scale[l])
        mixed = jnp.concatenate([y_rwkv, y_ret, y_pool], -1).astype(x.dtype) @ w_out[l]
        x = x + gate1 * mixed
        h = rms_norm(x, norm2_g[l]) * (1.0 + scale2) + shift2
        y = moe_ffn(h, w_router[l], router_bias[l], we_gate[l], we_up[l], we_down[l],
                    ws_gate[l], ws_up[l], ws_down[l])
        x = x + gate2 * y
    return rms_norm(x, final_g)
```

```python
import functools
import math

import numpy as np
import jax
import jax.numpy as jnp
from jax import lax
from jax.experimental import pallas as pl
from jax.experimental.pallas import tpu as pltpu

F32 = jnp.float32
BF16 = jnp.bfloat16
HIGHEST = lax.Precision.HIGHEST

D_MODEL = 1024
HEADS = 6
HEAD_DIM = 64
WIDTH = HEADS * HEAD_DIM
W_LORA = 64
A_LORA = 64
G_LORA = 128
RWKV_COLS = 3 * WIDTH + W_LORA + A_LORA + G_LORA
RET_COLS = 4 * WIDTH
RET_EXT_COLS = 6 * WIDTH
RET_CHUNK = 128
RWKV_CHUNK = 64
RWKV_SUB = 16
POOL_GROUPS = 4
POOL_GROUP_DIM = 64
POOL_WIDTH = POOL_GROUPS * POOL_GROUP_DIM
POOL_WINDOWS = (2, 4, 8, 16)
N_EXPERTS = 64
TOP_K = 8
N_GROUPS = 8
TOPK_GROUPS = 4
EXPERT_HIDDEN = 256
ROUTED_SCALE = 2.5
NORM_EPS = 1e-6
RWKV_LNX_EPS = 64e-5
RET_NORM_EPS = 1e-6
ROPE_BASE = 10000.0
LANES = 128
VMEM_LIMIT = 48 * 1024 * 1024


def _cparams(sem):
    return pltpu.CompilerParams(dimension_semantics=sem, vmem_limit_bytes=VMEM_LIMIT)


def _dot(a, b):
    return jnp.dot(a.astype(BF16), b.astype(BF16), preferred_element_type=F32)


def _dot_nt(a, b):
    return lax.dot_general(a.astype(BF16), b.astype(BF16), (((1,), (1,)), ((), ())),
                           preferred_element_type=F32)


def _dot_tn(a, b):
    return lax.dot_general(a.astype(BF16), b.astype(BF16), (((0,), (0,)), ((), ())),
                           preferred_element_type=F32)


def _dot_f32(a, b):
    return jnp.dot(a, b, preferred_element_type=F32, precision=HIGHEST)


def _sigmoid(x):
    return 1.0 / (1.0 + jnp.exp(-x))


def _silu(x):
    return x * _sigmoid(x)


def _head_masks(width):
    lane = lax.broadcasted_iota(jnp.int32, (1, width), 1)
    return [(lane // HEAD_DIM == h).astype(F32) for h in range(width // HEAD_DIM)]


def _stack_heads(x, masks):
    return jnp.concatenate([x * m for m in masks], axis=0)


def _select_heads(stacked, masks, c):
    out = stacked[0:c] * masks[0]
    for h in range(1, len(masks)):
        out = out + stacked[h * c:(h + 1) * c] * masks[h]
    return out


def _adaln_kernel(c_ref, w_ref, b_ref, o_ref):
    o_ref[0] = _dot_f32(_silu(c_ref[...]), w_ref[0]) + b_ref[0]


def _adaln(c, w_ada, b_ada):
    L, D, M = w_ada.shape
    B = c.shape[0]
    tn = 1536
    return pl.pallas_call(
        _adaln_kernel,
        grid=(L, M // tn),
        in_specs=[
            pl.BlockSpec((B, D), lambda l, j: (0, 0)),
            pl.BlockSpec((1, D, tn), lambda l, j: (l, 0, j)),
            pl.BlockSpec((1, 1, tn), lambda l, j: (l, 0, j)),
        ],
        out_specs=pl.BlockSpec((1, B, tn), lambda l, j: (l, 0, j)),
        out_shape=jax.ShapeDtypeStruct((L, B, M), F32),
        compiler_params=_cparams(("arbitrary", "arbitrary")),
        name="adaln",
    )(c, w_ada, b_ada.reshape(L, 1, M))


def _modulated_norm(x, g, scale, shift):
    ms = jnp.mean(x * x, axis=-1, keepdims=True)
    return x * lax.rsqrt(ms + NORM_EPS) * g * (1.0 + scale) + shift


def _inproj_kernel(x_ref, g_ref, sc_ref, sh_ref, w1_ref, w2_ref, w3_ref, o1_ref, o2_ref, o3_ref):
    h = _modulated_norm(x_ref[...], g_ref[...], sc_ref[0], sh_ref[0]).astype(BF16)
    o1_ref[...] = jnp.dot(h, w1_ref[...], preferred_element_type=F32)
    o2_ref[...] = jnp.dot(h, w2_ref[...], preferred_element_type=F32)
    o3_ref[...] = jnp.dot(h, w3_ref[...], preferred_element_type=F32)


def _inproj(x2, g, scale, shift, w1, w2, w3, T):
    N, D = x2.shape
    B = N // T
    tm = min(256, T)
    per_b = T // tm
    row = lambda i: (i, 0)
    const = lambda i: (0, 0)
    bvec = lambda i: (i // per_b, 0, 0)
    return pl.pallas_call(
        _inproj_kernel,
        grid=(N // tm,),
        in_specs=[
            pl.BlockSpec((tm, D), row),
            pl.BlockSpec((1, D), const),
            pl.BlockSpec((1, 1, D), bvec),
            pl.BlockSpec((1, 1, D), bvec),
            pl.BlockSpec(w1.shape, const),
            pl.BlockSpec(w2.shape, const),
            pl.BlockSpec(w3.shape, const),
        ],
        out_specs=[
            pl.BlockSpec((tm, w1.shape[1]), row),
            pl.BlockSpec((tm, w2.shape[1]), row),
            pl.BlockSpec((tm, w3.shape[1]), row),
        ],
        out_shape=[
            jax.ShapeDtypeStruct((N, w1.shape[1]), F32),
            jax.ShapeDtypeStruct((N, w2.shape[1]), F32),
            jax.ShapeDtypeStruct((N, w3.shape[1]), F32),
        ],
        compiler_params=_cparams(("arbitrary",)),
        name="inproj",
    )(x2, g.reshape(1, D), scale.reshape(B, 1, D), shift.reshape(B, 1, D), w1, w2, w3)


def _unit_lower_inverse(a3):
    H, C, _ = a3.shape
    ri = lax.broadcasted_iota(jnp.int32, (H, C, C), 1)
    ci = lax.broadcasted_iota(jnp.int32, (H, C, C), 2)
    eye = (ri == ci).astype(F32)
    same = (ri // RWKV_SUB) == (ci // RWKV_SUB)
    dm = jnp.where(same, a3, 0.0)
    off = jnp.where(same, 0.0, a3)

    def bmm(x, y):
        return jnp.einsum('hij,hjk->hik', x.astype(BF16), y.astype(BF16),
                          preferred_element_type=F32)

    d2 = bmm(dm, dm)
    d4 = bmm(d2, d2)
    d8 = bmm(d4, d4)
    x = eye - dm
    x = x + bmm(x, d2)
    x = x + bmm(x, d4)
    x = x + bmm(x, d8)
    n = bmm(x, off)
    n2 = bmm(n, n)
    y = eye - n
    y = y + bmm(y, n2)
    return bmm(y, x)


def _rwkv_kernel(p_ref, mu_ref, wup_ref, w0_ref, aup_ref, a0_ref, gup_ref, kk_ref, ka_ref,
                 rk_ref, lnw_ref, lnb_ref, hsum_ref, o_ref, carry_ref, s_ref):
    C = RWKV_CHUNK
    W = WIDTH
    H = HEADS

    @pl.when(pl.program_id(1) == 0)
    def _():
        carry_ref[...] = jnp.zeros_like(carry_ref)
        s_ref[...] = jnp.zeros_like(s_ref)

    p = p_ref[0]
    row = lax.broadcasted_iota(jnp.int32, (C, 1), 0)
    prev = jnp.where(row == 0, carry_ref[...], pltpu.roll(p, 1, 0))
    carry_ref[...] = p[C - 1:C, :]
    xs = p + (prev - p) * mu_ref[...]

    r = xs[:, 0:W]
    k = xs[:, W:2 * W]
    v = xs[:, 2 * W:3 * W]
    xwa = xs[:, 3 * W:3 * W + W_LORA + A_LORA]
    xg = xs[:, 3 * W + W_LORA + A_LORA:]
    hsum = hsum_ref[...]

    z = w0_ref[...] + _dot(jnp.tanh(xwa), wup_ref[...])
    log_w = -math.exp(-0.5) * _sigmoid(z)
    a = _sigmoid(a0_ref[...] + _dot(xwa, aup_ref[...]))
    g = _dot(_sigmoid(xg), gup_ref[...])
    kk = k * kk_ref[...]
    kk = kk / jnp.maximum(jnp.sqrt(_dot_f32(kk * kk, hsum)), 1e-12)
    k = k * (1.0 + (a - 1.0) * ka_ref[...])

    ti = lax.broadcasted_iota(jnp.int32, (C, C), 0)
    si = lax.broadcasted_iota(jnp.int32, (C, C), 1)
    cum = _dot_f32((ti >= si).astype(F32), log_w)
    g_in = jnp.exp(cum)
    g_inv = jnp.exp(-cum)
    g_end = g_in[C - 1:C, :]
    kt = kk * jnp.exp(cum - log_w)
    bt = kk * a * g_inv
    kq = k * g_inv
    rt = r * g_in

    masks = _head_masks(W)
    lhs = jnp.concatenate([_stack_heads(kt, masks), _stack_heads(rt, masks)], axis=0)
    gb = _dot_nt(lhs, bt).reshape(2 * H, C, C)
    gk = _dot_nt(lhs, kq).reshape(2 * H, C, C)
    t3 = lax.broadcasted_iota(jnp.int32, (H, C, C), 1)
    s3 = lax.broadcasted_iota(jnp.int32, (H, C, C), 2)
    a_ab = jnp.where(t3 > s3, gb[:H], 0.0)
    a_ak = jnp.where(t3 > s3, gk[:H], 0.0)
    p_rb = jnp.where(t3 >= s3, gb[H:], 0.0)
    p_rk = jnp.where(t3 >= s3, gk[H:], 0.0)
    t_inv = _unit_lower_inverse(a_ab)

    s0 = s_ref[...]
    ks = _dot_nt(kt, s0)
    rs = _dot_nt(rt, s0)
    av = _select_heads(_dot(a_ak.reshape(H * C, C), v), masks, C)
    u = _select_heads(_dot(t_inv.reshape(H * C, C), -(ks + av)), masks, C)
    y = (rs + _select_heads(_dot(p_rb.reshape(H * C, C), u), masks, C)
         + _select_heads(_dot(p_rk.reshape(H * C, C), v), masks, C))
    upd = _dot_tn(jnp.concatenate([u, v], axis=0),
                  jnp.concatenate([bt * g_end, kq * g_end], axis=0))
    hi = lax.broadcasted_iota(jnp.int32, (W, W), 0) // HEAD_DIM
    hj = lax.broadcasted_iota(jnp.int32, (W, W), 1) // HEAD_DIM
    s_ref[...] = s0 * g_end + jnp.where(hi == hj, upd, 0.0)

    inv_d = 1.0 / HEAD_DIM
    mean = _dot_f32(y, hsum) * inv_d
    yc = y - mean
    var = _dot_f32(yc * yc, hsum) * inv_d
    yn = yc * lax.rsqrt(var + RWKV_LNX_EPS) * lnw_ref[...] + lnb_ref[...]
    bonus = _dot_f32(r * k * rk_ref[...], hsum) * v
    o_ref[0] = (yn + bonus) * g


def _rwkv(p, mu, wup, w0, aup, a0, gup, k_k, k_a, r_k, lnx_w, lnx_b):
    B, T, _ = p.shape
    C = RWKV_CHUNK
    W = WIDTH
    lora_in = W_LORA + A_LORA
    wup_pad = jnp.zeros((lora_in, W), F32).at[:W_LORA].set(wup).astype(BF16)
    aup_pad = jnp.zeros((lora_in, W), F32).at[W_LORA:].set(aup).astype(BF16)
    head = np.arange(W) // HEAD_DIM
    hsum = jnp.asarray((head[:, None] == head[None, :]).astype(np.float32))
    vec = lambda a: a.reshape(1, -1)
    const = lambda b, t: (0, 0)
    params = [vec(mu), wup_pad, vec(w0), aup_pad, vec(a0), gup.astype(BF16), vec(k_k), vec(k_a),
              vec(r_k), vec(lnx_w), vec(lnx_b), hsum]
    return pl.pallas_call(
        _rwkv_kernel,
        grid=(B, T // C),
        in_specs=[pl.BlockSpec((1, C, RWKV_COLS), lambda b, t: (b, t, 0))]
        + [pl.BlockSpec(a.shape, const) for a in params],
        out_specs=pl.BlockSpec((1, C, W), lambda b, t: (b, t, 0)),
        out_shape=jax.ShapeDtypeStruct((B, T, W), F32),
        scratch_shapes=[pltpu.VMEM((1, RWKV_COLS), F32), pltpu.VMEM((W, W), F32)],
        compiler_params=_cparams(("arbitrary", "arbitrary")),
        name="rwkv7",
    )(p, *params)


def _retention_kernel(p_ref, cos_ref, sin_ref, dec_ref, xi_ref, zeta_ref, cd_ref, hsum_ref,
                      o_ref, s_ref):
    C = RET_CHUNK
    W = WIDTH

    @pl.when(pl.program_id(1) == 0)
    def _():
        s_ref[...] = jnp.zeros_like(s_ref)

    p = p_ref[0]
    cos = cos_ref[...]
    sin = sin_ref[...]
    q = p[:, 0:W] * cos + p[:, 4 * W:5 * W] * sin
    k = (p[:, W:2 * W] * cos + p[:, 5 * W:6 * W] * sin) * (HEAD_DIM ** -0.5)
    v = p[:, 2 * W:3 * W]
    gate = p[:, 3 * W:4 * W]

    masks = _head_masks(W)
    scores = _dot_nt(_stack_heads(q, masks), k) * dec_ref[...]
    y = _select_heads(_dot(scores, v), masks, C)
    s0 = s_ref[...]
    y = y + _dot(q * xi_ref[...], s0)
    kv = _dot_tn(k * zeta_ref[...], v)
    hi = lax.broadcasted_iota(jnp.int32, (W, W), 0) // HEAD_DIM
    hj = lax.broadcasted_iota(jnp.int32, (W, W), 1) // HEAD_DIM
    s_ref[...] = s0 * cd_ref[...] + jnp.where(hi == hj, kv, 0.0)

    ms = _dot_f32(y * y, hsum_ref[...]) * (1.0 / HEAD_DIM)
    o_ref[0] = _silu(gate) * (y * lax.rsqrt(ms + RET_NORM_EPS))


def _retention_tables(T):
    C, H, d = RET_CHUNK, HEADS, HEAD_DIM
    pos = jnp.arange(T, dtype=F32)
    inv_freq = ROPE_BASE ** (-jnp.arange(0, d, 2, dtype=F32) / d)
    ang = pos[:, None] * inv_freq[None, :]
    cos = jnp.cos(ang)
    sin = jnp.sin(ang)
    cos_full = jnp.tile(jnp.concatenate([cos, cos], -1), (1, H))
    sin_full = jnp.tile(jnp.concatenate([-sin, sin], -1), (1, H))
    log_gamma = jnp.log1p(-(2.0 ** (-5.0 - jnp.arange(H, dtype=F32))))
    idx = jnp.arange(C, dtype=F32)
    diff = idx[:, None] - idx[None, :]
    dec = jnp.where(diff >= 0, jnp.exp(log_gamma[:, None, None] * jnp.maximum(diff, 0.0)), 0.0)
    xi = jnp.exp(log_gamma[:, None] * (idx + 1.0))
    zeta = jnp.exp(log_gamma[:, None] * (C - 1.0 - idx))
    cd = jnp.exp(log_gamma * C)
    per_lane = lambda a: jnp.repeat(a.T, d, axis=1)
    return (cos_full, sin_full, dec.reshape(H * C, C), per_lane(xi), per_lane(zeta),
            jnp.repeat(cd, d).reshape(1, H * d))


def _retention(p):
    B, T, cols = p.shape
    C = RET_CHUNK
    W = WIDTH
    cos, sin, dec, xi, zeta, cd = _retention_tables(T)
    head = np.arange(W) // HEAD_DIM
    hsum = jnp.asarray((head[:, None] == head[None, :]).astype(np.float32))
    const = lambda b, t: (0, 0)
    return pl.pallas_call(
        _retention_kernel,
        grid=(B, T // C),
        in_specs=[
            pl.BlockSpec((1, C, cols), lambda b, t: (b, t, 0)),
            pl.BlockSpec((C, W), lambda b, t: (t, 0)),
            pl.BlockSpec((C, W), lambda b, t: (t, 0)),
            pl.BlockSpec(dec.shape, const),
            pl.BlockSpec(xi.shape, const),
            pl.BlockSpec(zeta.shape, const),
            pl.BlockSpec(cd.shape, const),
            pl.BlockSpec(hsum.shape, const),
        ],
        out_specs=pl.BlockSpec((1, C, W), lambda b, t: (b, t, 0)),
        out_shape=jax.ShapeDtypeStruct((B, T, W), F32),
        scratch_shapes=[pltpu.VMEM((W, W), F32)],
        compiler_params=_cparams(("arbitrary", "arbitrary")),
        name="retention",
    )(p, cos, sin, dec, xi, zeta, cd, hsum)


def _pool_kernel(u_ref, w_ref, scale_ref, o_ref):
    u = u_ref[0]
    T = u.shape[0]
    row = lax.broadcasted_iota(jnp.int32, (T, 1), 0)

    def lag(x, k):
        return jnp.where(row >= k, pltpu.roll(x, k, 0), 0.0)

    s2 = u + lag(u, 1)
    s4 = s2 + lag(s2, 2)
    s8 = s4 + lag(s4, 4)
    s16 = s8 + lag(s8, 8)
    grp = lax.broadcasted_iota(jnp.int32, (1, POOL_WIDTH), 1) // POOL_GROUP_DIM
    s = jnp.where(grp == 0, s2, jnp.where(grp == 1, s4, jnp.where(grp == 2, s8, s16)))
    win = jnp.where(grp == 0, POOL_WINDOWS[0],
                    jnp.where(grp == 1, POOL_WINDOWS[1],
                              jnp.where(grp == 2, POOL_WINDOWS[2], POOL_WINDOWS[3])))
    count = jnp.minimum(row + 1, win).astype(F32)
    pooled = s / count - u
    o_ref[0] = _dot(pooled, w_ref[...]) * scale_ref[...]


def _pool(u, pool_w, pool_scale):
    B, T, Wp = u.shape
    G, d = POOL_GROUPS, POOL_GROUP_DIM
    wbd = jnp.zeros((Wp, Wp), F32)
    for gi in range(G):
        wbd = wbd.at[gi * d:(gi + 1) * d, gi * d:(gi + 1) * d].set(pool_w[gi])
    return pl.pallas_call(
        _pool_kernel,
        grid=(B,),
        in_specs=[
            pl.BlockSpec((1, T, Wp), lambda b: (b, 0, 0)),
            pl.BlockSpec((Wp, Wp), lambda b: (0, 0)),
            pl.BlockSpec((1, Wp), lambda b: (0, 0)),
        ],
        out_specs=pl.BlockSpec((1, T, Wp), lambda b: (b, 0, 0)),
        out_shape=jax.ShapeDtypeStruct((B, T, Wp), F32),
        compiler_params=_cparams(("arbitrary",)),
        name="pool",
    )(u, wbd.astype(BF16), pool_scale.reshape(1, Wp))


def _route(logits_t, bias_col):
    E, tm = logits_t.shape
    per_group = E // N_GROUPS
    neg_inf = -jnp.inf
    scores = _sigmoid(logits_t)
    choice = scores + bias_col
    c3 = choice.reshape(N_GROUPS, per_group, tm)
    sub = lax.broadcasted_iota(jnp.int32, c3.shape, 1)
    m1 = jnp.max(c3, axis=1, keepdims=True)
    first = jnp.min(jnp.where(c3 == m1, sub, per_group), axis=1, keepdims=True)
    m2 = jnp.max(jnp.where(sub == first, neg_inf, c3), axis=1, keepdims=True)
    gs = m1 + m2
    gidx = lax.broadcasted_iota(jnp.int32, gs.shape, 0)
    grank = jnp.zeros(gs.shape, jnp.int32)
    for j in range(N_GROUPS):
        other = gs[j:j + 1]
        ahead = jnp.where(other > gs, 1, jnp.where((other == gs) & (gidx > j), 1, 0))
        grank = grank + ahead
    gmask = jnp.broadcast_to(grank < TOPK_GROUPS, c3.shape)
    masked = jnp.where(gmask, c3, neg_inf).reshape(E, tm)
    eidx = lax.broadcasted_iota(jnp.int32, (E, tm), 0)
    rank = jnp.zeros((E, tm), jnp.int32)
    for j in range(E):
        other = masked[j:j + 1, :]
        ahead = jnp.where(other > masked, 1, jnp.where((other == masked) & (eidx > j), 1, 0))
        rank = rank + ahead
    top = jnp.where(rank < TOP_K, scores, 0.0)
    return top / jnp.sum(top, axis=0, keepdims=True) * ROUTED_SCALE


def _outproj_kernel(yr_ref, yt_ref, yp_ref, x_ref, w1_ref, w2_ref, w3_ref, g1_ref, ng_ref,
                    sc_ref, sh_ref, wr_ref, rb_ref, x1_ref, h_ref, wc_ref):
    mixed = (_dot(yr_ref[...], w1_ref[...]) + _dot(yt_ref[...], w2_ref[...])
             + _dot(yp_ref[...], w3_ref[...]))
    x1 = x_ref[...] + g1_ref[0] * mixed
    x1_ref[...] = x1
    h = _modulated_norm(x1, ng_ref[...], sc_ref[0], sh_ref[0])
    h_ref[...] = h.astype(BF16)
    logits_t = lax.dot_general(wr_ref[...], h, (((1,), (1,)), ((), ())),
                               preferred_element_type=F32, precision=HIGHEST)
    wc_t = _route(logits_t, rb_ref[...])
    pad = jnp.zeros((LANES - N_EXPERTS, wc_t.shape[1]), F32)
    wc_ref[...] = jnp.concatenate([wc_t, pad], axis=0).T


def _outproj(yr, yt, yp, x2, w_out, gate1, ng, scale2, shift2, w_router, router_bias, T):
    N, D = x2.shape
    B = N // T
    tm = min(256, T)
    per_b = T // tm
    W = WIDTH
    w1 = w_out[:W].astype(BF16)
    w2 = w_out[W:2 * W].astype(BF16)
    w3 = w_out[2 * W:].astype(BF16)
    row = lambda i: (i, 0)
    const = lambda i: (0, 0)
    bvec = lambda i: (i // per_b, 0, 0)
    return pl.pallas_call(
        _outproj_kernel,
        grid=(N // tm,),
        in_specs=[
            pl.BlockSpec((tm, W), row),
            pl.BlockSpec((tm, W), row),
            pl.BlockSpec((tm, POOL_WIDTH), row),
            pl.BlockSpec((tm, D), row),
            pl.BlockSpec(w1.shape, const),
            pl.BlockSpec(w2.shape, const),
            pl.BlockSpec(w3.shape, const),
            pl.BlockSpec((1, 1, D), bvec),
            pl.BlockSpec((1, D), const),
            pl.BlockSpec((1, 1, D), bvec),
            pl.BlockSpec((1, 1, D), bvec),
            pl.BlockSpec((N_EXPERTS, D), const),
            pl.BlockSpec((N_EXPERTS, 1), const),
        ],
        out_specs=[
            pl.BlockSpec((tm, D), row),
            pl.BlockSpec((tm, D), row),
            pl.BlockSpec((tm, LANES), row),
        ],
        out_shape=[
            jax.ShapeDtypeStruct((N, D), F32),
            jax.ShapeDtypeStruct((N, D), BF16),
            jax.ShapeDtypeStruct((N, LANES), F32),
        ],
        compiler_params=_cparams(("arbitrary",)),
        name="outproj_router",
    )(yr, yt, yp, x2, w1, w2, w3, gate1.reshape(B, 1, D), ng.reshape(1, D),
      scale2.reshape(B, 1, D), shift2.reshape(B, 1, D), w_router.T,
      router_bias.reshape(N_EXPERTS, 1))


def _moe_kernel(h_ref, wc_ref, wg_ref, wu_ref, wd_ref, sg_ref, su_ref, sd_ref, x1_ref, g2_ref,
                fg_ref, o_ref, acc_ref, *, final_norm):
    e = pl.program_id(1)
    h = h_ref[...]

    @pl.when(e == 0)
    def _():
        hid = _silu(jnp.dot(h, sg_ref[...], preferred_element_type=F32)) * jnp.dot(
            h, su_ref[...], preferred_element_type=F32)
        acc_ref[...] = _dot(hid, sd_ref[...])

    lane = lax.broadcasted_iota(jnp.int32, wc_ref.shape, 1)
    w = jnp.sum(jnp.where(lane == e, wc_ref[...], 0.0), axis=1, keepdims=True)
    hid = _silu(jnp.dot(h, wg_ref[0], preferred_element_type=F32)) * jnp.dot(
        h, wu_ref[0], preferred_element_type=F32)
    acc_ref[...] += _dot(hid * w, wd_ref[0])

    @pl.when(e == pl.num_programs(1) - 1)
    def _():
        xo = x1_ref[...] + g2_ref[0] * acc_ref[...]
        if final_norm:
            ms = jnp.mean(xo * xo, axis=-1, keepdims=True)
            xo = xo * lax.rsqrt(ms + NORM_EPS) * fg_ref[...]
        o_ref[...] = xo


def _moe(h, wc, wg, wu, wd, sg, su, sd, x1, gate2, final_g, T, final_norm):
    N, D = x1.shape
    B = N // T
    tm = min(1024, T)
    per_b = T // tm
    E, _, Hd = wg.shape
    row = lambda i, e: (i, 0)
    const = lambda i, e: (0, 0)
    return pl.pallas_call(
        functools.partial(_moe_kernel, final_norm=final_norm),
        grid=(N // tm, E),
        in_specs=[
            pl.BlockSpec((tm, D), row),
            pl.BlockSpec((tm, LANES), row),
            pl.BlockSpec((1, D, Hd), lambda i, e: (e, 0, 0)),
            pl.BlockSpec((1, D, Hd), lambda i, e: (e, 0, 0)),
            pl.BlockSpec((1, Hd, D), lambda i, e: (e, 0, 0)),
            pl.BlockSpec(sg.shape, const),
            pl.BlockSpec(su.shape, const),
            pl.BlockSpec(sd.shape, const),
            pl.BlockSpec((tm, D), row),
            pl.BlockSpec((1, 1, D), lambda i, e: (i // per_b, 0, 0)),
            pl.BlockSpec((1, D), const),
        ],
        out_specs=pl.BlockSpec((tm, D), row),
        out_shape=jax.ShapeDtypeStruct((N, D), F32),
        scratch_shapes=[pltpu.VMEM((tm, D), F32)],
        compiler_params=_cparams(("arbitrary", "arbitrary")),
        name="moe",
    )(h, wc, wg, wu, wd, sg, su, sd, x1, gate2.reshape(B, 1, D), final_g.reshape(1, D))


def _split_w_in(w_in):
    W = WIDTH
    half = HEAD_DIM // 2
    j = np.arange(W)
    swap = (j // HEAD_DIM) * HEAD_DIM + (j % HEAD_DIM + half) % HEAD_DIM
    w_rwkv = w_in[:, :RWKV_COLS]
    w_ret = w_in[:, RWKV_COLS:RWKV_COLS + RET_COLS]
    w_pool = w_in[:, RWKV_COLS + RET_COLS:]
    q_sw = w_ret[:, 0:W][:, swap]
    k_sw = w_ret[:, W:2 * W][:, swap]
    w_ret_ext = jnp.concatenate([w_ret, q_sw, k_sw], axis=1)
    return w_rwkv.astype(BF16), w_ret_ext.astype(BF16), w_pool.astype(BF16)


def kernel(x, c, norm1_g, norm2_g, w_ada, b_ada, w_in, w_out, rwkv_mu, rwkv_w_up, rwkv_w0,
           rwkv_a_up, rwkv_a0, rwkv_g_up, rwkv_k_k, rwkv_k_a, rwkv_r_k, rwkv_lnx_w, rwkv_lnx_b,
           pool_w, pool_scale, w_router, router_bias, we_gate, we_up, we_down,
           ws_gate, ws_up, ws_down, final_g):
    B, T, D = x.shape
    L = w_in.shape[0]
    N = B * T
    mod = _adaln(c, w_ada, b_ada)
    x2 = x.reshape(N, D)
    for l in range(L):
        shift1, scale1, gate1, shift2, scale2, gate2 = jnp.split(mod[l], 6, axis=-1)
        w1, w2, w3 = _split_w_in(w_in[l])
        p_rwkv, p_ret, p_pool = _inproj(x2, norm1_g[l], scale1, shift1, w1, w2, w3, T)
        y_rwkv = _rwkv(p_rwkv.reshape(B, T, -1), rwkv_mu[l], rwkv_w_up[l], rwkv_w0[l],
                       rwkv_a_up[l], rwkv_a0[l], rwkv_g_up[l], rwkv_k_k[l], rwkv_k_a[l],
                       rwkv_r_k[l], rwkv_lnx_w[l], rwkv_lnx_b[l])
        y_ret = _retention(p_ret.reshape(B, T, -1))
        y_pool = _pool(p_pool.reshape(B, T, -1), pool_w[l], pool_scale[l])
        x1, h, wc = _outproj(y_rwkv.reshape(N, -1), y_ret.reshape(N, -1), y_pool.reshape(N, -1),
                             x2, w_out[l], gate1, norm2_g[l], scale2, shift2,
                             w_router[l], router_bias[l], T)
        x2 = _moe(h, wc, we_gate[l].astype(BF16), we_up[l].astype(BF16),
                  we_down[l].astype(BF16), ws_gate[l].astype(BF16), ws_up[l].astype(BF16),
                  ws_down[l].astype(BF16), x1, gate2, final_g, T, final_norm=(l == L - 1))
    return x2.reshape(B, T, D)
```

```python
import functools
import math

import numpy as np
import jax
import jax.numpy as jnp
from jax import lax
from jax.experimental import pallas as pl
from jax.experimental.pallas import tpu as pltpu

F32 = jnp.float32
BF16 = jnp.bfloat16
HIGHEST = lax.Precision.HIGHEST

D_MODEL = 1024
HEADS = 6
HEAD_DIM = 64
WIDTH = HEADS * HEAD_DIM
W_LORA = 64
A_LORA = 64
G_LORA = 128
RWKV_COLS = 3 * WIDTH + W_LORA + A_LORA + G_LORA
RET_COLS = 4 * WIDTH
RET_EXT_COLS = 6 * WIDTH
RET_CHUNK = 128
RWKV_CHUNK = 64
RWKV_SUB = 16
RWKV_ROWS = 4
POOL_GROUPS = 4
POOL_GROUP_DIM = 64
POOL_WIDTH = POOL_GROUPS * POOL_GROUP_DIM
POOL_WINDOWS = (2, 4, 8, 16)
N_EXPERTS = 64
TOP_K = 8
N_GROUPS = 8
TOPK_GROUPS = 4
EXPERT_HIDDEN = 256
ROUTED_SCALE = 2.5
NORM_EPS = 1e-6
RWKV_LNX_EPS = 64e-5
RET_NORM_EPS = 1e-6
ROPE_BASE = 10000.0
LANES = 128
VMEM_LIMIT = 48 * 1024 * 1024


def _cparams(sem):
    return pltpu.CompilerParams(dimension_semantics=sem, vmem_limit_bytes=VMEM_LIMIT)


def _dot(a, b):
    return jnp.dot(a.astype(BF16), b.astype(BF16), preferred_element_type=F32)


def _dot_nt(a, b):
    return lax.dot_general(a.astype(BF16), b.astype(BF16), (((1,), (1,)), ((), ())),
                           preferred_element_type=F32)


def _dot_tn(a, b):
    return lax.dot_general(a.astype(BF16), b.astype(BF16), (((0,), (0,)), ((), ())),
                           preferred_element_type=F32)


def _dot_f32(a, b):
    return jnp.dot(a, b, preferred_element_type=F32, precision=HIGHEST)


def _sigmoid(x):
    return 1.0 / (1.0 + jnp.exp(-x))


def _silu(x):
    return x * _sigmoid(x)


def _head_masks(width):
    lane = lax.broadcasted_iota(jnp.int32, (1, width), 1)
    return [(lane // HEAD_DIM == h).astype(F32) for h in range(width // HEAD_DIM)]


def _stack_heads(x, masks):
    return jnp.concatenate([x * m for m in masks], axis=0)


def _select_heads(stacked, masks, c):
    out = stacked[0:c] * masks[0]
    for h in range(1, len(masks)):
        out = out + stacked[h * c:(h + 1) * c] * masks[h]
    return out


def _adaln_kernel(c_ref, w_ref, b_ref, o_ref):
    o_ref[0] = _dot_f32(_silu(c_ref[...]), w_ref[0]) + b_ref[0]


def _adaln(c, w_ada, b_ada):
    L, D, M = w_ada.shape
    B = c.shape[0]
    tn = 1536
    return pl.pallas_call(
        _adaln_kernel,
        grid=(L, M // tn),
        in_specs=[
            pl.BlockSpec((B, D), lambda l, j: (0, 0)),
            pl.BlockSpec((1, D, tn), lambda l, j: (l, 0, j)),
            pl.BlockSpec((1, 1, tn), lambda l, j: (l, 0, j)),
        ],
        out_specs=pl.BlockSpec((1, B, tn), lambda l, j: (l, 0, j)),
        out_shape=jax.ShapeDtypeStruct((L, B, M), F32),
        compiler_params=_cparams(("arbitrary", "arbitrary")),
        name="adaln",
    )(c, w_ada, b_ada.reshape(L, 1, M))


def _modulated_norm(x, g, scale, shift):
    ms = jnp.mean(x * x, axis=-1, keepdims=True)
    return x * lax.rsqrt(ms + NORM_EPS) * g * (1.0 + scale) + shift


def _inproj_kernel(x_ref, g_ref, sc_ref, sh_ref, w1_ref, w2_ref, w3_ref, o1_ref, o2_ref, o3_ref):
    h = _modulated_norm(x_ref[...], g_ref[...], sc_ref[0], sh_ref[0]).astype(BF16)
    o1_ref[...] = jnp.dot(h, w1_ref[...], preferred_element_type=F32)
    o2_ref[...] = jnp.dot(h, w2_ref[...], preferred_element_type=F32)
    o3_ref[...] = jnp.dot(h, w3_ref[...], preferred_element_type=F32)


def _inproj(x2, g, scale, shift, w1, w2, w3, T):
    N, D = x2.shape
    B = N // T
    tm = min(256, T)
    per_b = T // tm
    row = lambda i: (i, 0)
    const = lambda i: (0, 0)
    bvec = lambda i: (i // per_b, 0, 0)
    return pl.pallas_call(
        _inproj_kernel,
        grid=(N // tm,),
        in_specs=[
            pl.BlockSpec((tm, D), row),
            pl.BlockSpec((1, D), const),
            pl.BlockSpec((1, 1, D), bvec),
            pl.BlockSpec((1, 1, D), bvec),
            pl.BlockSpec(w1.shape, const),
            pl.BlockSpec(w2.shape, const),
            pl.BlockSpec(w3.shape, const),
        ],
        out_specs=[
            pl.BlockSpec((tm, w1.shape[1]), row),
            pl.BlockSpec((tm, w2.shape[1]), row),
            pl.BlockSpec((tm, w3.shape[1]), row),
        ],
        out_shape=[
            jax.ShapeDtypeStruct((N, w1.shape[1]), F32),
            jax.ShapeDtypeStruct((N, w2.shape[1]), F32),
            jax.ShapeDtypeStruct((N, w3.shape[1]), F32),
        ],
        compiler_params=_cparams(("arbitrary",)),
        name="inproj",
    )(x2, g.reshape(1, D), scale.reshape(B, 1, D), shift.reshape(B, 1, D), w1, w2, w3)


def _unit_lower_inverse(a3):
    H, C, _ = a3.shape
    ri = lax.broadcasted_iota(jnp.int32, (H, C, C), 1)
    ci = lax.broadcasted_iota(jnp.int32, (H, C, C), 2)
    eye = (ri == ci).astype(F32)
    same = (ri // RWKV_SUB) == (ci // RWKV_SUB)
    dm = jnp.where(same, a3, 0.0)
    off = jnp.where(same, 0.0, a3)

    def bmm(x, y):
        return jnp.einsum('hij,hjk->hik', x.astype(BF16), y.astype(BF16),
                          preferred_element_type=F32)

    d2 = bmm(dm, dm)
    d4 = bmm(d2, d2)
    d8 = bmm(d4, d4)
    x = eye - dm
    x = x + bmm(x, d2)
    x = x + bmm(x, d4)
    x = x + bmm(x, d8)
    n = bmm(x, off)
    n2 = bmm(n, n)
    y = eye - n
    y = y + bmm(y, n2)
    return bmm(y, x)


def _rwkv_prepare(p, carry, prm):
    (mu, wup, w0, aup, a0, gup, k_k, k_a, r_k, lnw, lnb, hsum) = prm
    C = RWKV_CHUNK
    W = WIDTH
    H = HEADS
    row = lax.broadcasted_iota(jnp.int32, (C, 1), 0)
    prev = jnp.where(row == 0, carry, pltpu.roll(p, 1, 0))
    xs = p + (prev - p) * mu

    r = xs[:, 0:W]
    k = xs[:, W:2 * W]
    v = xs[:, 2 * W:3 * W]
    xwa = xs[:, 3 * W:3 * W + W_LORA + A_LORA]
    xg = xs[:, 3 * W + W_LORA + A_LORA:]

    z = w0 + _dot(jnp.tanh(xwa), wup)
    log_w = -math.exp(-0.5) * _sigmoid(z)
    a = _sigmoid(a0 + _dot(xwa, aup))
    g = _dot(_sigmoid(xg), gup)
    kk = k * k_k
    kk = kk / jnp.maximum(jnp.sqrt(_dot(kk * kk, hsum)), 1e-12)
    k = k * (1.0 + (a - 1.0) * k_a)

    ti = lax.broadcasted_iota(jnp.int32, (C, C), 0)
    si = lax.broadcasted_iota(jnp.int32, (C, C), 1)
    tri = jnp.where(ti >= si, 1.0, 0.0).astype(BF16)
    log_w_hi = log_w.astype(BF16)
    cum = (jnp.dot(tri, log_w_hi, preferred_element_type=F32)
           + _dot(tri, log_w - log_w_hi.astype(F32)))
    g_in = jnp.exp(cum)
    g_inv = jnp.exp(-cum)
    g_end = g_in[C - 1:C, :]
    kt = kk * jnp.exp(cum - log_w)
    bt = kk * a * g_inv
    kq = k * g_inv
    rt = r * g_in

    masks = _head_masks(W)
    lhs = jnp.concatenate([_stack_heads(kt, masks), _stack_heads(rt, masks)], axis=0)
    gb = _dot_nt(lhs, bt).reshape(2 * H, C, C)
    gk = _dot_nt(lhs, kq).reshape(2 * H, C, C)
    t3 = lax.broadcasted_iota(jnp.int32, (H, C, C), 1)
    s3 = lax.broadcasted_iota(jnp.int32, (H, C, C), 2)
    a_ab = jnp.where(t3 > s3, gb[:H], 0.0)
    a_ak = jnp.where(t3 > s3, gk[:H], 0.0)
    p_rb = jnp.where(t3 >= s3, gb[H:], 0.0)
    p_rk = jnp.where(t3 >= s3, gk[H:], 0.0)
    return a_ab, (a_ak, p_rb, p_rk, kt, rt, bt, kq, g_end, r, k, v, g)


def _rwkv_finish(vals, t_inv, s0, prm):
    (a_ak, p_rb, p_rk, kt, rt, bt, kq, g_end, r, k, v, g) = vals
    (mu, wup, w0, aup, a0, gup, k_k, k_a, r_k, lnw, lnb, hsum) = prm
    C = RWKV_CHUNK
    W = WIDTH
    H = HEADS
    masks = _head_masks(W)
    ks = _dot_nt(kt, s0)
    rs = _dot_nt(rt, s0)
    av = _select_heads(_dot(a_ak.reshape(H * C, C), v), masks, C)
    u = _select_heads(_dot(t_inv.reshape(H * C, C), -(ks + av)), masks, C)
    y = (rs + _select_heads(_dot(p_rb.reshape(H * C, C), u), masks, C)
         + _select_heads(_dot(p_rk.reshape(H * C, C), v), masks, C))
    upd = _dot_tn(jnp.concatenate([u, v], axis=0),
                  jnp.concatenate([bt * g_end, kq * g_end], axis=0))
    hi = lax.broadcasted_iota(jnp.int32, (W, W), 0) // HEAD_DIM
    hj = lax.broadcasted_iota(jnp.int32, (W, W), 1) // HEAD_DIM
    s_new = s0 * g_end + jnp.where(hi == hj, upd, 0.0)

    inv_d = 1.0 / HEAD_DIM
    mean = _dot(y, hsum) * inv_d
    yc = y - mean
    var = _dot(yc * yc, hsum) * inv_d
    yn = yc * lax.rsqrt(var + RWKV_LNX_EPS) * lnw + lnb
    bonus = _dot(r * k * r_k, hsum) * v
    return (yn + bonus) * g, s_new


def _rwkv_kernel(p_ref, mu_ref, wup_ref, w0_ref, aup_ref, a0_ref, gup_ref, kk_ref, ka_ref,
                 rk_ref, lnw_ref, lnb_ref, hsum_ref, o_ref, carry_ref, s_ref):
    @pl.when(pl.program_id(1) == 0)
    def _():
        carry_ref[...] = jnp.zeros_like(carry_ref)
        s_ref[...] = jnp.zeros_like(s_ref)

    prm = tuple(ref[...] for ref in (mu_ref, wup_ref, w0_ref, aup_ref, a0_ref, gup_ref, kk_ref,
                                     ka_ref, rk_ref, lnw_ref, lnb_ref, hsum_ref))
    G = p_ref.shape[0]
    H = HEADS
    a_abs, vals = [], []
    for i in range(G):
        p = p_ref[i]
        a_ab, val = _rwkv_prepare(p, carry_ref[i], prm)
        carry_ref[i] = p[RWKV_CHUNK - 1:RWKV_CHUNK, :]
        a_abs.append(a_ab)
        vals.append(val)
    t_inv = _unit_lower_inverse(jnp.concatenate(a_abs, axis=0))
    for i in range(G):
        out, s_new = _rwkv_finish(vals[i], t_inv[i * H:(i + 1) * H], s_ref[i], prm)
        s_ref[i] = s_new
        o_ref[i] = out


def _rwkv(p, mu, wup, w0, aup, a0, gup, k_k, k_a, r_k, lnx_w, lnx_b):
    B, T, _ = p.shape
    C = RWKV_CHUNK
    W = WIDTH
    G = RWKV_ROWS if B % RWKV_ROWS == 0 else 1
    lora_in = W_LORA + A_LORA
    wup_pad = jnp.zeros((lora_in, W), F32).at[:W_LORA].set(wup).astype(BF16)
    aup_pad = jnp.zeros((lora_in, W), F32).at[W_LORA:].set(aup).astype(BF16)
    head = np.arange(W) // HEAD_DIM
    hsum = jnp.asarray((head[:, None] == head[None, :]).astype(np.float32)).astype(BF16)
    vec = lambda a: a.reshape(1, -1)
    const = lambda b, t: (0, 0)
    params = [vec(mu), wup_pad, vec(w0), aup_pad, vec(a0), gup.astype(BF16), vec(k_k), vec(k_a),
              vec(r_k), vec(lnx_w), vec(lnx_b), hsum]
    return pl.pallas_call(
        _rwkv_kernel,
        grid=(B // G, T // C),
        in_specs=[pl.BlockSpec((G, C, RWKV_COLS), lambda b, t: (b, t, 0))]
        + [pl.BlockSpec(a.shape, const) for a in params],
        out_specs=pl.BlockSpec((G, C, W), lambda b, t: (b, t, 0)),
        out_shape=jax.ShapeDtypeStruct((B, T, W), F32),
        scratch_shapes=[pltpu.VMEM((G, 1, RWKV_COLS), F32), pltpu.VMEM((G, W, W), F32)],
        compiler_params=_cparams(("arbitrary", "arbitrary")),
        name="rwkv7",
    )(p, *params)


def _retention_kernel(p_ref, cos_ref, sin_ref, dec_ref, xi_ref, zeta_ref, cd_ref, hsum_ref,
                      o_ref, s_ref):
    C = RET_CHUNK
    W = WIDTH

    @pl.when(pl.program_id(1) == 0)
    def _():
        s_ref[...] = jnp.zeros_like(s_ref)

    p = p_ref[0]
    cos = cos_ref[...]
    sin = sin_ref[...]
    q = p[:, 0:W] * cos + p[:, 4 * W:5 * W] * sin
    k = (p[:, W:2 * W] * cos + p[:, 5 * W:6 * W] * sin) * (HEAD_DIM ** -0.5)
    v = p[:, 2 * W:3 * W]
    gate = p[:, 3 * W:4 * W]

    masks = _head_masks(W)
    scores = _dot_nt(_stack_heads(q, masks), k) * dec_ref[...]
    y = _select_heads(_dot(scores, v), masks, C)
    s0 = s_ref[...]
    y = y + _dot(q * xi_ref[...], s0)
    kv = _dot_tn(k * zeta_ref[...], v)
    hi = lax.broadcasted_iota(jnp.int32, (W, W), 0) // HEAD_DIM
    hj = lax.broadcasted_iota(jnp.int32, (W, W), 1) // HEAD_DIM
    s_ref[...] = s0 * cd_ref[...] + jnp.where(hi == hj, kv, 0.0)

    ms = _dot(y * y, hsum_ref[...]) * (1.0 / HEAD_DIM)
    o_ref[0] = _silu(gate) * (y * lax.rsqrt(ms + RET_NORM_EPS))


def _retention_tables(T):
    C, H, d = RET_CHUNK, HEADS, HEAD_DIM
    pos = jnp.arange(T, dtype=F32)
    inv_freq = ROPE_BASE ** (-jnp.arange(0, d, 2, dtype=F32) / d)
    ang = pos[:, None] * inv_freq[None, :]
    cos = jnp.cos(ang)
    sin = jnp.sin(ang)
    cos_full = jnp.tile(jnp.concatenate([cos, cos], -1), (1, H))
    sin_full = jnp.tile(jnp.concatenate([-sin, sin], -1), (1, H))
    log_gamma = jnp.log1p(-(2.0 ** (-5.0 - jnp.arange(H, dtype=F32))))
    idx = jnp.arange(C, dtype=F32)
    diff = idx[:, None] - idx[None, :]
    dec = jnp.where(diff >= 0, jnp.exp(log_gamma[:, None, None] * jnp.maximum(diff, 0.0)), 0.0)
    xi = jnp.exp(log_gamma[:, None] * (idx + 1.0))
    zeta = jnp.exp(log_gamma[:, None] * (C - 1.0 - idx))
    cd = jnp.exp(log_gamma * C)
    per_lane = lambda a: jnp.repeat(a.T, d, axis=1)
    return (cos_full, sin_full, dec.reshape(H * C, C), per_lane(xi), per_lane(zeta),
            jnp.repeat(cd, d).reshape(1, H * d))


def _retention(p):
    B, T, cols = p.shape
    C = RET_CHUNK
    W = WIDTH
    cos, sin, dec, xi, zeta, cd = _retention_tables(T)
    head = np.arange(W) // HEAD_DIM
    hsum = jnp.asarray((head[:, None] == head[None, :]).astype(np.float32)).astype(BF16)
    const = lambda b, t: (0, 0)
    return pl.pallas_call(
        _retention_kernel,
        grid=(B, T // C),
        in_specs=[
            pl.BlockSpec((1, C, cols), lambda b, t: (b, t, 0)),
            pl.BlockSpec((C, W), lambda b, t: (t, 0)),
            pl.BlockSpec((C, W), lambda b, t: (t, 0)),
            pl.BlockSpec(dec.shape, const),
            pl.BlockSpec(xi.shape, const),
            pl.BlockSpec(zeta.shape, const),
            pl.BlockSpec(cd.shape, const),
            pl.BlockSpec(hsum.shape, const),
        ],
        out_specs=pl.BlockSpec((1, C, W), lambda b, t: (b, t, 0)),
        out_shape=jax.ShapeDtypeStruct((B, T, W), F32),
        scratch_shapes=[pltpu.VMEM((W, W), F32)],
        compiler_params=_cparams(("arbitrary", "arbitrary")),
        name="retention",
    )(p, cos, sin, dec, xi, zeta, cd, hsum)


def _pool_kernel(u_ref, w_ref, scale_ref, o_ref):
    u = u_ref[0]
    T = u.shape[0]
    row = lax.broadcasted_iota(jnp.int32, (T, 1), 0)

    def lag(x, k):
        return jnp.where(row >= k, pltpu.roll(x, k, 0), 0.0)

    s2 = u + lag(u, 1)
    s4 = s2 + lag(s2, 2)
    s8 = s4 + lag(s4, 4)
    s16 = s8 + lag(s8, 8)
    grp = lax.broadcasted_iota(jnp.int32, (1, POOL_WIDTH), 1) // POOL_GROUP_DIM
    s = jnp.where(grp == 0, s2, jnp.where(grp == 1, s4, jnp.where(grp == 2, s8, s16)))
    win = jnp.where(grp == 0, POOL_WINDOWS[0],
                    jnp.where(grp == 1, POOL_WINDOWS[1],
                              jnp.where(grp == 2, POOL_WINDOWS[2], POOL_WINDOWS[3])))
    count = jnp.minimum(row + 1, win).astype(F32)
    pooled = s / count - u
    o_ref[0] = _dot(pooled, w_ref[...]) * scale_ref[...]


def _pool(u, pool_w, pool_scale):
    B, T, Wp = u.shape
    G, d = POOL_GROUPS, POOL_GROUP_DIM
    wbd = jnp.zeros((Wp, Wp), F32)
    for gi in range(G):
        wbd = wbd.at[gi * d:(gi + 1) * d, gi * d:(gi + 1) * d].set(pool_w[gi])
    return pl.pallas_call(
        _pool_kernel,
        grid=(B,),
        in_specs=[
            pl.BlockSpec((1, T, Wp), lambda b: (b, 0, 0)),
            pl.BlockSpec((Wp, Wp), lambda b: (0, 0)),
            pl.BlockSpec((1, Wp), lambda b: (0, 0)),
        ],
        out_specs=pl.BlockSpec((1, T, Wp), lambda b: (b, 0, 0)),
        out_shape=jax.ShapeDtypeStruct((B, T, Wp), F32),
        compiler_params=_cparams(("arbitrary",)),
        name="pool",
    )(u, wbd.astype(BF16), pool_scale.reshape(1, Wp))


def _route(logits_t, bias_col):
    E, tm = logits_t.shape
    per_group = E // N_GROUPS
    neg_inf = -jnp.inf
    scores = _sigmoid(logits_t)
    choice = scores + bias_col
    c3 = choice.reshape(N_GROUPS, per_group, tm)
    sub = lax.broadcasted_iota(jnp.int32, c3.shape, 1)
    m1 = jnp.max(c3, axis=1, keepdims=True)
    first = jnp.min(jnp.where(c3 == m1, sub, per_group), axis=1, keepdims=True)
    m2 = jnp.max(jnp.where(sub == first, neg_inf, c3), axis=1, keepdims=True)
    gs = m1 + m2
    gidx = lax.broadcasted_iota(jnp.int32, gs.shape, 0)
    grank = jnp.zeros(gs.shape, jnp.int32)
    for j in range(N_GROUPS):
        other = gs[j:j + 1]
        ahead = jnp.where(other > gs, 1, jnp.where((other == gs) & (gidx > j), 1, 0))
        grank = grank + ahead
    gmask = jnp.broadcast_to(grank < TOPK_GROUPS, c3.shape)
    masked = jnp.where(gmask, c3, neg_inf).reshape(E, tm)
    eidx = lax.broadcasted_iota(jnp.int32, (E, tm), 0)
    rank = jnp.zeros((E, tm), jnp.int32)
    for j in range(E):
        other = masked[j:j + 1, :]
        ahead = jnp.where(other > masked, 1, jnp.where((other == masked) & (eidx > j), 1, 0))
        rank = rank + ahead
    top = jnp.where(rank < TOP_K, scores, 0.0)
    return top / jnp.sum(top, axis=0, keepdims=True) * ROUTED_SCALE


def _outproj_kernel(yr_ref, yt_ref, yp_ref, x_ref, w1_ref, w2_ref, w3_ref, g1_ref, ng_ref,
                    sc_ref, sh_ref, wr_ref, rb_ref, x1_ref, h_ref, wc_ref):
    mixed = (_dot(yr_ref[...], w1_ref[...]) + _dot(yt_ref[...], w2_ref[...])
             + _dot(yp_ref[...], w3_ref[...]))
    x1 = x_ref[...] + g1_ref[0] * mixed
    x1_ref[...] = x1
    h = _modulated_norm(x1, ng_ref[...], sc_ref[0], sh_ref[0])
    h_ref[...] = h.astype(BF16)
    logits_t = lax.dot_general(wr_ref[...], h, (((1,), (1,)), ((), ())),
                               preferred_element_type=F32, precision=HIGHEST)
    wc_t = _route(logits_t, rb_ref[...])
    pad = jnp.zeros((LANES - N_EXPERTS, wc_t.shape[1]), F32)
    wc_ref[...] = jnp.concatenate([wc_t, pad], axis=0).T


def _outproj(yr, yt, yp, x2, w_out, gate1, ng, scale2, shift2, w_router, router_bias, T):
    N, D = x2.shape
    B = N // T
    tm = min(256, T)
    per_b = T // tm
    W = WIDTH
    w1 = w_out[:W].astype(BF16)
    w2 = w_out[W:2 * W].astype(BF16)
    w3 = w_out[2 * W:].astype(BF16)
    row = lambda i: (i, 0)
    const = lambda i: (0, 0)
    bvec = lambda i: (i // per_b, 0, 0)
    return pl.pallas_call(
        _outproj_kernel,
        grid=(N // tm,),
        in_specs=[
            pl.BlockSpec((tm, W), row),
            pl.BlockSpec((tm, W), row),
            pl.BlockSpec((tm, POOL_WIDTH), row),
            pl.BlockSpec((tm, D), row),
            pl.BlockSpec(w1.shape, const),
            pl.BlockSpec(w2.shape, const),
            pl.BlockSpec(w3.shape, const),
            pl.BlockSpec((1, 1, D), bvec),
            pl.BlockSpec((1, D), const),
            pl.BlockSpec((1, 1, D), bvec),
            pl.BlockSpec((1, 1, D), bvec),
            pl.BlockSpec((N_EXPERTS, D), const),
            pl.BlockSpec((N_EXPERTS, 1), const),
        ],
        out_specs=[
            pl.BlockSpec((tm, D), row),
            pl.BlockSpec((tm, D), row),
            pl.BlockSpec((tm, LANES), row),
        ],
        out_shape=[
            jax.ShapeDtypeStruct((N, D), F32),
            jax.ShapeDtypeStruct((N, D), BF16),
            jax.ShapeDtypeStruct((N, LANES), F32),
        ],
        compiler_params=_cparams(("arbitrary",)),
        name="outproj_router",
    )(yr, yt, yp, x2, w1, w2, w3, gate1.reshape(B, 1, D), ng.reshape(1, D),
      scale2.reshape(B, 1, D), shift2.reshape(B, 1, D), w_router.T,
      router_bias.reshape(N_EXPERTS, 1))


def _moe_kernel(h_ref, wc_ref, wg_ref, wu_ref, wd_ref, sg_ref, su_ref, sd_ref, x1_ref, g2_ref,
                fg_ref, o_ref, acc_ref, *, final_norm):
    e = pl.program_id(1)
    h = h_ref[...]

    @pl.when(e == 0)
    def _():
        hid = _silu(jnp.dot(h, sg_ref[...], preferred_element_type=F32)) * jnp.dot(
            h, su_ref[...], preferred_element_type=F32)
        acc_ref[...] = _dot(hid, sd_ref[...])

    lane = lax.broadcasted_iota(jnp.int32, wc_ref.shape, 1)
    w = jnp.sum(jnp.where(lane == e, wc_ref[...], 0.0), axis=1, keepdims=True)
    hid = _silu(jnp.dot(h, wg_ref[0], preferred_element_type=F32)) * jnp.dot(
        h, wu_ref[0], preferred_element_type=F32)
    acc_ref[...] += _dot(hid * w, wd_ref[0])

    @pl.when(e == pl.num_programs(1) - 1)
    def _():
        xo = x1_ref[...] + g2_ref[0] * acc_ref[...]
        if final_norm:
            ms = jnp.mean(xo * xo, axis=-1, keepdims=True)
            xo = xo * lax.rsqrt(ms + NORM_EPS) * fg_ref[...]
        o_ref[...] = xo


def _moe(h, wc, wg, wu, wd, sg, su, sd, x1, gate2, final_g, T, final_norm):
    N, D = x1.shape
    B = N // T
    tm = min(1024, T)
    per_b = T // tm
    E, _, Hd = wg.shape
    row = lambda i, e: (i, 0)
    const = lambda i, e: (0, 0)
    return pl.pallas_call(
        functools.partial(_moe_kernel, final_norm=final_norm),
        grid=(N // tm, E),
        in_specs=[
            pl.BlockSpec((tm, D), row),
            pl.BlockSpec((tm, LANES), row),
            pl.BlockSpec((1, D, Hd), lambda i, e: (e, 0, 0)),
            pl.BlockSpec((1, D, Hd), lambda i, e: (e, 0, 0)),
            pl.BlockSpec((1, Hd, D), lambda i, e: (e, 0, 0)),
            pl.BlockSpec(sg.shape, const),
            pl.BlockSpec(su.shape, const),
            pl.BlockSpec(sd.shape, const),
            pl.BlockSpec((tm, D), row),
            pl.BlockSpec((1, 1, D), lambda i, e: (i // per_b, 0, 0)),
            pl.BlockSpec((1, D), const),
        ],
        out_specs=pl.BlockSpec((tm, D), row),
        out_shape=jax.ShapeDtypeStruct((N, D), F32),
        scratch_shapes=[pltpu.VMEM((tm, D), F32)],
        compiler_params=_cparams(("arbitrary", "arbitrary")),
        name="moe",
    )(h, wc, wg, wu, wd, sg, su, sd, x1, gate2.reshape(B, 1, D), final_g.reshape(1, D))


def _split_w_in(w_in):
    W = WIDTH
    half = HEAD_DIM // 2
    j = np.arange(W)
    swap = (j // HEAD_DIM) * HEAD_DIM + (j % HEAD_DIM + half) % HEAD_DIM
    w_rwkv = w_in[:, :RWKV_COLS]
    w_ret = w_in[:, RWKV_COLS:RWKV_COLS + RET_COLS]
    w_pool = w_in[:, RWKV_COLS + RET_COLS:]
    q_sw = w_ret[:, 0:W][:, swap]
    k_sw = w_ret[:, W:2 * W][:, swap]
    w_ret_ext = jnp.concatenate([w_ret, q_sw, k_sw], axis=1)
    return w_rwkv.astype(BF16), w_ret_ext.astype(BF16), w_pool.astype(BF16)


def kernel(x, c, norm1_g, norm2_g, w_ada, b_ada, w_in, w_out, rwkv_mu, rwkv_w_up, rwkv_w0,
           rwkv_a_up, rwkv_a0, rwkv_g_up, rwkv_k_k, rwkv_k_a, rwkv_r_k, rwkv_lnx_w, rwkv_lnx_b,
           pool_w, pool_scale, w_router, router_bias, we_gate, we_up, we_down,
           ws_gate, ws_up, ws_down, final_g):
    B, T, D = x.shape
    L = w_in.shape[0]
    N = B * T
    mod = _adaln(c, w_ada, b_ada)
    x2 = x.reshape(N, D)
    for l in range(L):
        shift1, scale1, gate1, shift2, scale2, gate2 = jnp.split(mod[l], 6, axis=-1)
        w1, w2, w3 = _split_w_in(w_in[l])
        p_rwkv, p_ret, p_pool = _inproj(x2, norm1_g[l], scale1, shift1, w1, w2, w3, T)
        y_rwkv = _rwkv(p_rwkv.reshape(B, T, -1), rwkv_mu[l], rwkv_w_up[l], rwkv_w0[l],
                       rwkv_a_up[l], rwkv_a0[l], rwkv_g_up[l], rwkv_k_k[l], rwkv_k_a[l],
                       rwkv_r_k[l], rwkv_lnx_w[l], rwkv_lnx_b[l])
        y_ret = _retention(p_ret.reshape(B, T, -1))
        y_pool = _pool(p_pool.reshape(B, T, -1), pool_w[l], pool_scale[l])
        x1, h, wc = _outproj(y_rwkv.reshape(N, -1), y_ret.reshape(N, -1), y_pool.reshape(N, -1),
                             x2, w_out[l], gate1, norm2_g[l], scale2, shift2,
                             w_router[l], router_bias[l], T)
        x2 = _moe(h, wc, we_gate[l].astype(BF16), we_up[l].astype(BF16),
                  we_down[l].astype(BF16), ws_gate[l].astype(BF16), ws_up[l].astype(BF16),
                  ws_down[l].astype(BF16), x1, gate2, final_g, T, final_norm=(l == L - 1))
    return x2.reshape(B, T, D)
```

```python
import functools
import math

import numpy as np
import jax
import jax.numpy as jnp
from jax import lax
from jax.experimental import pallas as pl
from jax.experimental.pallas import tpu as pltpu

F32 = jnp.float32
BF16 = jnp.bfloat16
HIGHEST = lax.Precision.HIGHEST

D_MODEL = 1024
HEADS = 6
HEAD_DIM = 64
WIDTH = HEADS * HEAD_DIM
W_LORA = 64
A_LORA = 64
G_LORA = 128
RWKV_COLS = 3 * WIDTH + W_LORA + A_LORA + G_LORA
RET_COLS = 4 * WIDTH
RET_EXT_COLS = 6 * WIDTH
RET_CHUNK = 128
RWKV_CHUNK = 64
RWKV_SUB = 16
RWKV_ROWS = 4
POOL_GROUPS = 4
POOL_GROUP_DIM = 64
POOL_WIDTH = POOL_GROUPS * POOL_GROUP_DIM
POOL_WINDOWS = (2, 4, 8, 16)
N_EXPERTS = 64
TOP_K = 8
N_GROUPS = 8
TOPK_GROUPS = 4
EXPERT_HIDDEN = 256
ROUTED_SCALE = 2.5
NORM_EPS = 1e-6
RWKV_LNX_EPS = 64e-5
RET_NORM_EPS = 1e-6
ROPE_BASE = 10000.0
LANES = 128
MOE_TILE_SEQ = 16
MOE_CAP = 80
VMEM_LIMIT = 48 * 1024 * 1024


def _cparams(sem):
    return pltpu.CompilerParams(dimension_semantics=sem, vmem_limit_bytes=VMEM_LIMIT)


def _dot(a, b):
    return jnp.dot(a.astype(BF16), b.astype(BF16), preferred_element_type=F32)


def _dot_nt(a, b):
    return lax.dot_general(a.astype(BF16), b.astype(BF16), (((1,), (1,)), ((), ())),
                           preferred_element_type=F32)


def _dot_tn(a, b):
    return lax.dot_general(a.astype(BF16), b.astype(BF16), (((0,), (0,)), ((), ())),
                           preferred_element_type=F32)


def _dot_f32(a, b):
    return jnp.dot(a, b, preferred_element_type=F32, precision=HIGHEST)


def _sigmoid(x):
    return 1.0 / (1.0 + jnp.exp(-x))


def _silu(x):
    return x * _sigmoid(x)


def _head_masks(width):
    lane = lax.broadcasted_iota(jnp.int32, (1, width), 1)
    return [(lane // HEAD_DIM == h).astype(F32) for h in range(width // HEAD_DIM)]


def _stack_heads(x, masks):
    return jnp.concatenate([x * m for m in masks], axis=0)


def _select_heads(stacked, masks, c):
    out = stacked[0:c] * masks[0]
    for h in range(1, len(masks)):
        out = out + stacked[h * c:(h + 1) * c] * masks[h]
    return out


def _adaln_kernel(c_ref, w_ref, b_ref, o_ref):
    o_ref[0] = _dot_f32(_silu(c_ref[...]), w_ref[0]) + b_ref[0]


def _adaln(c, w_ada, b_ada):
    L, D, M = w_ada.shape
    B = c.shape[0]
    tn = 1536
    return pl.pallas_call(
        _adaln_kernel,
        grid=(L, M // tn),
        in_specs=[
            pl.BlockSpec((B, D), lambda l, j: (0, 0)),
            pl.BlockSpec((1, D, tn), lambda l, j: (l, 0, j)),
            pl.BlockSpec((1, 1, tn), lambda l, j: (l, 0, j)),
        ],
        out_specs=pl.BlockSpec((1, B, tn), lambda l, j: (l, 0, j)),
        out_shape=jax.ShapeDtypeStruct((L, B, M), F32),
        compiler_params=_cparams(("arbitrary", "arbitrary")),
        name="adaln",
    )(c, w_ada, b_ada.reshape(L, 1, M))


def _modulated_norm(x, g, scale, shift):
    ms = jnp.mean(x * x, axis=-1, keepdims=True)
    return x * lax.rsqrt(ms + NORM_EPS) * g * (1.0 + scale) + shift


def _inproj_kernel(x_ref, g_ref, sc_ref, sh_ref, w1_ref, w2_ref, w3_ref, o1_ref, o2_ref, o3_ref):
    h = _modulated_norm(x_ref[...], g_ref[...], sc_ref[0], sh_ref[0]).astype(BF16)
    o1_ref[...] = jnp.dot(h, w1_ref[...], preferred_element_type=F32)
    o2_ref[...] = jnp.dot(h, w2_ref[...], preferred_element_type=F32)
    o3_ref[...] = jnp.dot(h, w3_ref[...], preferred_element_type=F32)


def _inproj(x2, g, scale, shift, w1, w2, w3, T):
    N, D = x2.shape
    B = N // T
    tm = min(256, T)
    per_b = T // tm
    row = lambda i: (i, 0)
    const = lambda i: (0, 0)
    bvec = lambda i: (i // per_b, 0, 0)
    return pl.pallas_call(
        _inproj_kernel,
        grid=(N // tm,),
        in_specs=[
            pl.BlockSpec((tm, D), row),
            pl.BlockSpec((1, D), const),
            pl.BlockSpec((1, 1, D), bvec),
            pl.BlockSpec((1, 1, D), bvec),
            pl.BlockSpec(w1.shape, const),
            pl.BlockSpec(w2.shape, const),
            pl.BlockSpec(w3.shape, const),
        ],
        out_specs=[
            pl.BlockSpec((tm, w1.shape[1]), row),
            pl.BlockSpec((tm, w2.shape[1]), row),
            pl.BlockSpec((tm, w3.shape[1]), row),
        ],
        out_shape=[
            jax.ShapeDtypeStruct((N, w1.shape[1]), F32),
            jax.ShapeDtypeStruct((N, w2.shape[1]), F32),
            jax.ShapeDtypeStruct((N, w3.shape[1]), F32),
        ],
        compiler_params=_cparams(("arbitrary",)),
        name="inproj",
    )(x2, g.reshape(1, D), scale.reshape(B, 1, D), shift.reshape(B, 1, D), w1, w2, w3)


def _unit_lower_inverse(a3):
    H, C, _ = a3.shape
    ri = lax.broadcasted_iota(jnp.int32, (H, C, C), 1)
    ci = lax.broadcasted_iota(jnp.int32, (H, C, C), 2)
    eye = (ri == ci).astype(F32)
    same = (ri // RWKV_SUB) == (ci // RWKV_SUB)
    dm = jnp.where(same, a3, 0.0)
    off = jnp.where(same, 0.0, a3)

    def bmm(x, y):
        return jnp.einsum('hij,hjk->hik', x.astype(BF16), y.astype(BF16),
                          preferred_element_type=F32)

    d2 = bmm(dm, dm)
    d4 = bmm(d2, d2)
    d8 = bmm(d4, d4)
    x = eye - dm
    x = x + bmm(x, d2)
    x = x + bmm(x, d4)
    x = x + bmm(x, d8)
    n = bmm(x, off)
    n2 = bmm(n, n)
    y = eye - n
    y = y + bmm(y, n2)
    return bmm(y, x)


def _rwkv_prepare(p, carry, prm):
    (mu, wup, w0, aup, a0, gup, k_k, k_a, r_k, lnw, lnb, hsum) = prm
    C = RWKV_CHUNK
    W = WIDTH
    H = HEADS
    row = lax.broadcasted_iota(jnp.int32, (C, 1), 0)
    prev = jnp.where(row == 0, carry, pltpu.roll(p, 1, 0))
    xs = p + (prev - p) * mu

    r = xs[:, 0:W]
    k = xs[:, W:2 * W]
    v = xs[:, 2 * W:3 * W]
    xwa = xs[:, 3 * W:3 * W + W_LORA + A_LORA]
    xg = xs[:, 3 * W + W_LORA + A_LORA:]

    z = w0 + _dot(jnp.tanh(xwa), wup)
    log_w = -math.exp(-0.5) * _sigmoid(z)
    a = _sigmoid(a0 + _dot(xwa, aup))
    g = _dot(_sigmoid(xg), gup)
    kk = k * k_k
    kk = kk / jnp.maximum(jnp.sqrt(_dot(kk * kk, hsum)), 1e-12)
    k = k * (1.0 + (a - 1.0) * k_a)

    ti = lax.broadcasted_iota(jnp.int32, (C, C), 0)
    si = lax.broadcasted_iota(jnp.int32, (C, C), 1)
    tri = jnp.where(ti >= si, 1.0, 0.0).astype(BF16)
    log_w_hi = log_w.astype(BF16)
    cum = (jnp.dot(tri, log_w_hi, preferred_element_type=F32)
           + _dot(tri, log_w - log_w_hi.astype(F32)))
    g_in = jnp.exp(cum)
    g_inv = jnp.exp(-cum)
    g_end = g_in[C - 1:C, :]
    kt = kk * jnp.exp(cum - log_w)
    bt = kk * a * g_inv
    kq = k * g_inv
    rt = r * g_in

    masks = _head_masks(W)
    lhs = jnp.concatenate([_stack_heads(kt, masks), _stack_heads(rt, masks)], axis=0)
    gb = _dot_nt(lhs, bt).reshape(2 * H, C, C)
    gk = _dot_nt(lhs, kq).reshape(2 * H, C, C)
    t3 = lax.broadcasted_iota(jnp.int32, (H, C, C), 1)
    s3 = lax.broadcasted_iota(jnp.int32, (H, C, C), 2)
    a_ab = jnp.where(t3 > s3, gb[:H], 0.0)
    a_ak = jnp.where(t3 > s3, gk[:H], 0.0)
    p_rb = jnp.where(t3 >= s3, gb[H:], 0.0)
    p_rk = jnp.where(t3 >= s3, gk[H:], 0.0)
    return a_ab, (a_ak, p_rb, p_rk, kt, rt, bt, kq, g_end, r, k, v, g)


def _rwkv_finish(vals, t_inv, s0, prm):
    (a_ak, p_rb, p_rk, kt, rt, bt, kq, g_end, r, k, v, g) = vals
    (mu, wup, w0, aup, a0, gup, k_k, k_a, r_k, lnw, lnb, hsum) = prm
    C = RWKV_CHUNK
    W = WIDTH
    H = HEADS
    masks = _head_masks(W)
    ks = _dot_nt(kt, s0)
    rs = _dot_nt(rt, s0)
    av = _select_heads(_dot(a_ak.reshape(H * C, C), v), masks, C)
    u = _select_heads(_dot(t_inv.reshape(H * C, C), -(ks + av)), masks, C)
    y = (rs + _select_heads(_dot(p_rb.reshape(H * C, C), u), masks, C)
         + _select_heads(_dot(p_rk.reshape(H * C, C), v), masks, C))
    upd = _dot_tn(jnp.concatenate([u, v], axis=0),
                  jnp.concatenate([bt * g_end, kq * g_end], axis=0))
    hi = lax.broadcasted_iota(jnp.int32, (W, W), 0) // HEAD_DIM
    hj = lax.broadcasted_iota(jnp.int32, (W, W), 1) // HEAD_DIM
    s_new = s0 * g_end + jnp.where(hi == hj, upd, 0.0)

    inv_d = 1.0 / HEAD_DIM
    mean = _dot(y, hsum) * inv_d
    yc = y - mean
    var = _dot(yc * yc, hsum) * inv_d
    yn = yc * lax.rsqrt(var + RWKV_LNX_EPS) * lnw + lnb
    bonus = _dot(r * k * r_k, hsum) * v
    return (yn + bonus) * g, s_new


def _rwkv_kernel(p_ref, mu_ref, wup_ref, w0_ref, aup_ref, a0_ref, gup_ref, kk_ref, ka_ref,
                 rk_ref, lnw_ref, lnb_ref, hsum_ref, o_ref, carry_ref, s_ref):
    @pl.when(pl.program_id(1) == 0)
    def _():
        carry_ref[...] = jnp.zeros_like(carry_ref)
        s_ref[...] = jnp.zeros_like(s_ref)

    prm = tuple(ref[...] for ref in (mu_ref, wup_ref, w0_ref, aup_ref, a0_ref, gup_ref, kk_ref,
                                     ka_ref, rk_ref, lnw_ref, lnb_ref, hsum_ref))
    G = p_ref.shape[0]
    H = HEADS
    a_abs, vals = [], []
    for i in range(G):
        p = p_ref[i]
        a_ab, val = _rwkv_prepare(p, carry_ref[i], prm)
        carry_ref[i] = p[RWKV_CHUNK - 1:RWKV_CHUNK, :]
        a_abs.append(a_ab)
        vals.append(val)
    t_inv = _unit_lower_inverse(jnp.concatenate(a_abs, axis=0))
    for i in range(G):
        out, s_new = _rwkv_finish(vals[i], t_inv[i * H:(i + 1) * H], s_ref[i], prm)
        s_ref[i] = s_new
        o_ref[i] = out


def _rwkv(p, mu, wup, w0, aup, a0, gup, k_k, k_a, r_k, lnx_w, lnx_b):
    B, T, _ = p.shape
    C = RWKV_CHUNK
    W = WIDTH
    G = RWKV_ROWS if B % RWKV_ROWS == 0 else 1
    lora_in = W_LORA + A_LORA
    wup_pad = jnp.zeros((lora_in, W), F32).at[:W_LORA].set(wup).astype(BF16)
    aup_pad = jnp.zeros((lora_in, W), F32).at[W_LORA:].set(aup).astype(BF16)
    head = np.arange(W) // HEAD_DIM
    hsum = jnp.asarray((head[:, None] == head[None, :]).astype(np.float32)).astype(BF16)
    vec = lambda a: a.reshape(1, -1)
    const = lambda b, t: (0, 0)
    params = [vec(mu), wup_pad, vec(w0), aup_pad, vec(a0), gup.astype(BF16), vec(k_k), vec(k_a),
              vec(r_k), vec(lnx_w), vec(lnx_b), hsum]
    return pl.pallas_call(
        _rwkv_kernel,
        grid=(B // G, T // C),
        in_specs=[pl.BlockSpec((G, C, RWKV_COLS), lambda b, t: (b, t, 0))]
        + [pl.BlockSpec(a.shape, const) for a in params],
        out_specs=pl.BlockSpec((G, C, W), lambda b, t: (b, t, 0)),
        out_shape=jax.ShapeDtypeStruct((B, T, W), F32),
        scratch_shapes=[pltpu.VMEM((G, 1, RWKV_COLS), F32), pltpu.VMEM((G, W, W), F32)],
        compiler_params=_cparams(("arbitrary", "arbitrary")),
        name="rwkv7",
    )(p, *params)


def _retention_kernel(p_ref, cos_ref, sin_ref, dec_ref, xi_ref, zeta_ref, cd_ref, hsum_ref,
                      o_ref, s_ref):
    C = RET_CHUNK
    W = WIDTH

    @pl.when(pl.program_id(1) == 0)
    def _():
        s_ref[...] = jnp.zeros_like(s_ref)

    p = p_ref[0]
    cos = cos_ref[...]
    sin = sin_ref[...]
    q = p[:, 0:W] * cos + p[:, 4 * W:5 * W] * sin
    k = (p[:, W:2 * W] * cos + p[:, 5 * W:6 * W] * sin) * (HEAD_DIM ** -0.5)
    v = p[:, 2 * W:3 * W]
    gate = p[:, 3 * W:4 * W]

    masks = _head_masks(W)
    scores = _dot_nt(_stack_heads(q, masks), k) * dec_ref[...]
    y = _select_heads(_dot(scores, v), masks, C)
    s0 = s_ref[...]
    y = y + _dot(q * xi_ref[...], s0)
    kv = _dot_tn(k * zeta_ref[...], v)
    hi = lax.broadcasted_iota(jnp.int32, (W, W), 0) // HEAD_DIM
    hj = lax.broadcasted_iota(jnp.int32, (W, W), 1) // HEAD_DIM
    s_ref[...] = s0 * cd_ref[...] + jnp.where(hi == hj, kv, 0.0)

    ms = _dot(y * y, hsum_ref[...]) * (1.0 / HEAD_DIM)
    o_ref[0] = _silu(gate) * (y * lax.rsqrt(ms + RET_NORM_EPS))


def _retention_tables(T):
    C, H, d = RET_CHUNK, HEADS, HEAD_DIM
    pos = jnp.arange(T, dtype=F32)
    inv_freq = ROPE_BASE ** (-jnp.arange(0, d, 2, dtype=F32) / d)
    ang = pos[:, None] * inv_freq[None, :]
    cos = jnp.cos(ang)
    sin = jnp.sin(ang)
    cos_full = jnp.tile(jnp.concatenate([cos, cos], -1), (1, H))
    sin_full = jnp.tile(jnp.concatenate([-sin, sin], -1), (1, H))
    log_gamma = jnp.log1p(-(2.0 ** (-5.0 - jnp.arange(H, dtype=F32))))
    idx = jnp.arange(C, dtype=F32)
    diff = idx[:, None] - idx[None, :]
    dec = jnp.where(diff >= 0, jnp.exp(log_gamma[:, None, None] * jnp.maximum(diff, 0.0)), 0.0)
    xi = jnp.exp(log_gamma[:, None] * (idx + 1.0))
    zeta = jnp.exp(log_gamma[:, None] * (C - 1.0 - idx))
    cd = jnp.exp(log_gamma * C)
    per_lane = lambda a: jnp.repeat(a.T, d, axis=1)
    return (cos_full, sin_full, dec.reshape(H * C, C), per_lane(xi), per_lane(zeta),
            jnp.repeat(cd, d).reshape(1, H * d))


def _retention(p):
    B, T, cols = p.shape
    C = RET_CHUNK
    W = WIDTH
    cos, sin, dec, xi, zeta, cd = _retention_tables(T)
    head = np.arange(W) // HEAD_DIM
    hsum = jnp.asarray((head[:, None] == head[None, :]).astype(np.float32)).astype(BF16)
    const = lambda b, t: (0, 0)
    return pl.pallas_call(
        _retention_kernel,
        grid=(B, T // C),
        in_specs=[
            pl.BlockSpec((1, C, cols), lambda b, t: (b, t, 0)),
            pl.BlockSpec((C, W), lambda b, t: (t, 0)),
            pl.BlockSpec((C, W), lambda b, t: (t, 0)),
            pl.BlockSpec(dec.shape, const),
            pl.BlockSpec(xi.shape, const),
            pl.BlockSpec(zeta.shape, const),
            pl.BlockSpec(cd.shape, const),
            pl.BlockSpec(hsum.shape, const),
        ],
        out_specs=pl.BlockSpec((1, C, W), lambda b, t: (b, t, 0)),
        out_shape=jax.ShapeDtypeStruct((B, T, W), F32),
        scratch_shapes=[pltpu.VMEM((W, W), F32)],
        compiler_params=_cparams(("arbitrary", "arbitrary")),
        name="retention",
    )(p, cos, sin, dec, xi, zeta, cd, hsum)


def _pool_kernel(u_ref, w_ref, scale_ref, o_ref):
    u = u_ref[0]
    T = u.shape[0]
    row = lax.broadcasted_iota(jnp.int32, (T, 1), 0)

    def lag(x, k):
        return jnp.where(row >= k, pltpu.roll(x, k, 0), 0.0)

    s2 = u + lag(u, 1)
    s4 = s2 + lag(s2, 2)
    s8 = s4 + lag(s4, 4)
    s16 = s8 + lag(s8, 8)
    grp = lax.broadcasted_iota(jnp.int32, (1, POOL_WIDTH), 1) // POOL_GROUP_DIM
    s = jnp.where(grp == 0, s2, jnp.where(grp == 1, s4, jnp.where(grp == 2, s8, s16)))
    win = jnp.where(grp == 0, POOL_WINDOWS[0],
                    jnp.where(grp == 1, POOL_WINDOWS[1],
                              jnp.where(grp == 2, POOL_WINDOWS[2], POOL_WINDOWS[3])))
    count = jnp.minimum(row + 1, win).astype(F32)
    pooled = s / count - u
    o_ref[0] = _dot(pooled, w_ref[...]) * scale_ref[...]


def _pool(u, pool_w, pool_scale):
    B, T, Wp = u.shape
    G, d = POOL_GROUPS, POOL_GROUP_DIM
    wbd = jnp.zeros((Wp, Wp), F32)
    for gi in range(G):
        wbd = wbd.at[gi * d:(gi + 1) * d, gi * d:(gi + 1) * d].set(pool_w[gi])
    return pl.pallas_call(
        _pool_kernel,
        grid=(B,),
        in_specs=[
            pl.BlockSpec((1, T, Wp), lambda b: (b, 0, 0)),
            pl.BlockSpec((Wp, Wp), lambda b: (0, 0)),
            pl.BlockSpec((1, Wp), lambda b: (0, 0)),
        ],
        out_specs=pl.BlockSpec((1, T, Wp), lambda b: (b, 0, 0)),
        out_shape=jax.ShapeDtypeStruct((B, T, Wp), F32),
        compiler_params=_cparams(("arbitrary",)),
        name="pool",
    )(u, wbd.astype(BF16), pool_scale.reshape(1, Wp))


def _route(logits_t, bias_col):
    E, tm = logits_t.shape
    per_group = E // N_GROUPS
    neg_inf = -jnp.inf
    scores = _sigmoid(logits_t)
    choice = scores + bias_col
    c3 = choice.reshape(N_GROUPS, per_group, tm)
    sub = lax.broadcasted_iota(jnp.int32, c3.shape, 1)
    m1 = jnp.max(c3, axis=1, keepdims=True)
    first = jnp.min(jnp.where(c3 == m1, sub, per_group), axis=1, keepdims=True)
    m2 = jnp.max(jnp.where(sub == first, neg_inf, c3), axis=1, keepdims=True)
    gs = m1 + m2
    gidx = lax.broadcasted_iota(jnp.int32, gs.shape, 0)
    grank = jnp.zeros(gs.shape, jnp.int32)
    for j in range(N_GROUPS):
        other = gs[j:j + 1]
        ahead = jnp.where(other > gs, 1, jnp.where((other == gs) & (gidx > j), 1, 0))
        grank = grank + ahead
    gmask = jnp.broadcast_to(grank < TOPK_GROUPS, c3.shape)
    masked = jnp.where(gmask, c3, neg_inf).reshape(E, tm)
    eidx = lax.broadcasted_iota(jnp.int32, (E, tm), 0)
    rank = jnp.zeros((E, tm), jnp.int32)
    for j in range(E):
        other = masked[j:j + 1, :]
        ahead = jnp.where(other > masked, 1, jnp.where((other == masked) & (eidx > j), 1, 0))
        rank = rank + ahead
    top = jnp.where(rank < TOP_K, scores, 0.0)
    return top / jnp.sum(top, axis=0, keepdims=True) * ROUTED_SCALE


def _outproj_kernel(yr_ref, yt_ref, yp_ref, x_ref, w1_ref, w2_ref, w3_ref, g1_ref, ng_ref,
                    sc_ref, sh_ref, wr_ref, rb_ref, x1_ref, h_ref, wc_ref):
    mixed = (_dot(yr_ref[...], w1_ref[...]) + _dot(yt_ref[...], w2_ref[...])
             + _dot(yp_ref[...], w3_ref[...]))
    x1 = x_ref[...] + g1_ref[0] * mixed
    x1_ref[...] = x1
    h = _modulated_norm(x1, ng_ref[...], sc_ref[0], sh_ref[0])
    h_ref[...] = h.astype(BF16)
    logits_t = lax.dot_general(wr_ref[...], h, (((1,), (1,)), ((), ())),
                               preferred_element_type=F32, precision=HIGHEST)
    wc_t = _route(logits_t, rb_ref[...])
    pad = jnp.zeros((LANES - N_EXPERTS, wc_t.shape[1]), F32)
    wc_ref[...] = jnp.concatenate([wc_t, pad], axis=0).T


def _outproj(yr, yt, yp, x2, w_out, gate1, ng, scale2, shift2, w_router, router_bias, T):
    N, D = x2.shape
    B = N // T
    tm = min(256, T)
    per_b = T // tm
    W = WIDTH
    w1 = w_out[:W].astype(BF16)
    w2 = w_out[W:2 * W].astype(BF16)
    w3 = w_out[2 * W:].astype(BF16)
    row = lambda i: (i, 0)
    const = lambda i: (0, 0)
    bvec = lambda i: (i // per_b, 0, 0)
    return pl.pallas_call(
        _outproj_kernel,
        grid=(N // tm,),
        in_specs=[
            pl.BlockSpec((tm, W), row),
            pl.BlockSpec((tm, W), row),
            pl.BlockSpec((tm, POOL_WIDTH), row),
            pl.BlockSpec((tm, D), row),
            pl.BlockSpec(w1.shape, const),
            pl.BlockSpec(w2.shape, const),
            pl.BlockSpec(w3.shape, const),
            pl.BlockSpec((1, 1, D), bvec),
            pl.BlockSpec((1, D), const),
            pl.BlockSpec((1, 1, D), bvec),
            pl.BlockSpec((1, 1, D), bvec),
            pl.BlockSpec((N_EXPERTS, D), const),
            pl.BlockSpec((N_EXPERTS, 1), const),
        ],
        out_specs=[
            pl.BlockSpec((tm, D), row),
            pl.BlockSpec((tm, D), row),
            pl.BlockSpec((tm, LANES), row),
        ],
        out_shape=[
            jax.ShapeDtypeStruct((N, D), F32),
            jax.ShapeDtypeStruct((N, D), BF16),
            jax.ShapeDtypeStruct((N, LANES), F32),
        ],
        compiler_params=_cparams(("arbitrary",)),
        name="outproj_router",
    )(yr, yt, yp, x2, w1, w2, w3, gate1.reshape(B, 1, D), ng.reshape(1, D),
      scale2.reshape(B, 1, D), shift2.reshape(B, 1, D), w_router.T,
      router_bias.reshape(N_EXPERTS, 1))


def _slot_positions(wc):
    tm = wc.shape[0]
    sel = jnp.where(wc > 0.0, 1.0, 0.0).astype(BF16)
    ti = lax.broadcasted_iota(jnp.int32, (tm, tm), 0)
    si = lax.broadcasted_iota(jnp.int32, (tm, tm), 1)
    earlier = jnp.where(ti > si, 1.0, 0.0).astype(BF16)
    return sel, jnp.dot(earlier, sel, preferred_element_type=F32)


def _dispatch_kernel(h_ref, wc_ref, rept_ref, xs_ref, over_ref):
    nb, ts, D = h_ref.shape
    tm = nb * ts
    wc = wc_ref[...].reshape(tm, LANES)
    sel, pos = _slot_positions(wc)
    over_ref[...] = jnp.where(pos >= MOE_CAP, wc, 0.0).reshape(nb, ts, LANES)
    rept = rept_ref[...]
    pos_rows = _dot_nt(rept, pos)
    sel_rows = _dot_nt(rept, sel)
    slot = (lax.broadcasted_iota(jnp.int32, (rept.shape[0], 1), 0) % MOE_CAP).astype(F32)
    onehot = jnp.where(pos_rows == slot, sel_rows, 0.0).astype(BF16)
    xs_ref[...] = jnp.dot(onehot, h_ref[...].reshape(tm, D),
                          preferred_element_type=F32).astype(BF16)


def _dispatch(h, wc, B, T):
    N, D = h.shape
    ts = MOE_TILE_SEQ
    nt = T // ts
    L = N_EXPERTS * MOE_CAP
    rows = np.arange(L) // MOE_CAP
    rept = jnp.asarray((rows[:, None] == np.arange(LANES)[None, :]).astype(np.float32)).astype(BF16)
    tile = lambda i: (0, i, 0, 0)
    xs, over = pl.pallas_call(
        _dispatch_kernel,
        grid=(nt,),
        in_specs=[
            pl.BlockSpec((B, None, ts, D), tile),
            pl.BlockSpec((B, None, ts, LANES), tile),
            pl.BlockSpec((L, LANES), lambda i: (0, 0)),
        ],
        out_specs=[
            pl.BlockSpec((None, L, D), lambda i: (i, 0, 0)),
            pl.BlockSpec((B, None, ts, LANES), tile),
        ],
        out_shape=[
            jax.ShapeDtypeStruct((nt, L, D), BF16),
            jax.ShapeDtypeStruct((B, nt, ts, LANES), F32),
        ],
        compiler_params=_cparams(("arbitrary",)),
        name="moe_dispatch",
    )(h.reshape(B, nt, ts, D), wc.reshape(B, nt, ts, LANES), rept)
    return xs, over.reshape(N, LANES)


def _experts_kernel(x_ref, wg_ref, wu_ref, wd_ref, o_ref):
    tg, cap, D = x_ref.shape
    x = x_ref[...].reshape(tg * cap, D)
    hid = _silu(jnp.dot(x, wg_ref[0], preferred_element_type=F32)) * jnp.dot(
        x, wu_ref[0], preferred_element_type=F32)
    o_ref[...] = _dot(hid, wd_ref[0]).astype(BF16).reshape(tg, cap, D)


def _experts(xs, wg, wu, wd):
    nt, L, D = xs.shape
    E, _, Hd = wg.shape
    cap = L // E
    tg = math.gcd(nt, 16)
    slots = pl.BlockSpec((tg, None, cap, D), lambda e, g: (g, e, 0, 0))
    return pl.pallas_call(
        _experts_kernel,
        grid=(E, nt // tg),
        in_specs=[
            slots,
            pl.BlockSpec((1, D, Hd), lambda e, g: (e, 0, 0)),
            pl.BlockSpec((1, D, Hd), lambda e, g: (e, 0, 0)),
            pl.BlockSpec((1, Hd, D), lambda e, g: (e, 0, 0)),
        ],
        out_specs=slots,
        out_shape=jax.ShapeDtypeStruct((nt, E, cap, D), BF16),
        compiler_params=_cparams(("arbitrary", "arbitrary")),
        name="moe_experts",
    )(xs.reshape(nt, E, cap, D), wg, wu, wd).reshape(nt, L, D)


def _combine_kernel(y_ref, wc_ref, rep_ref, h_ref, x1_ref, ex_ref, g2_ref, sg_ref, su_ref, sd_ref,
                    fg_ref, o_ref, *, final_norm):
    nb, ts, D = h_ref.shape
    tm = nb * ts
    wc = wc_ref[...].reshape(tm, LANES)
    _, pos = _slot_positions(wc)
    rep = rep_ref[...]
    pos_lanes = _dot(pos, rep)
    w_lanes = _dot(wc, rep)
    slot = (lax.broadcasted_iota(jnp.int32, (1, rep.shape[1]), 1) % MOE_CAP).astype(F32)
    weighted = jnp.where(pos_lanes == slot, w_lanes, 0.0).astype(BF16)
    routed = jnp.dot(weighted, y_ref[...], preferred_element_type=F32)
    h = h_ref[...].reshape(tm, D)
    hid = _silu(jnp.dot(h, sg_ref[...], preferred_element_type=F32)) * jnp.dot(
        h, su_ref[...], preferred_element_type=F32)
    y = routed + ex_ref[...].reshape(tm, D) + _dot(hid, sd_ref[...])
    gate = jnp.broadcast_to(g2_ref[...], (nb, ts, D)).reshape(tm, D)
    xo = x1_ref[...].reshape(tm, D) + gate * y
    if final_norm:
        ms = jnp.mean(xo * xo, axis=-1, keepdims=True)
        xo = xo * lax.rsqrt(ms + NORM_EPS) * fg_ref[...]
    o_ref[...] = xo.reshape(nb, ts, D)


def _combine(ys, wc, h, x1, extra, gate2, sg, su, sd, final_g, B, T, final_norm):
    N, D = x1.shape
    ts = MOE_TILE_SEQ
    nt = T // ts
    L = N_EXPERTS * MOE_CAP
    cols = np.arange(L) // MOE_CAP
    rep = jnp.asarray((np.arange(LANES)[:, None] == cols[None, :]).astype(np.float32)).astype(BF16)
    tile = lambda i: (0, i, 0, 0)
    const = lambda i: (0, 0)
    tok = lambda a: a.reshape(B, nt, ts, a.shape[-1])
    out = pl.pallas_call(
        functools.partial(_combine_kernel, final_norm=final_norm),
        grid=(nt,),
        in_specs=[
            pl.BlockSpec((None, L, D), lambda i: (i, 0, 0)),
            pl.BlockSpec((B, None, ts, LANES), tile),
            pl.BlockSpec((LANES, L), const),
            pl.BlockSpec((B, None, ts, D), tile),
            pl.BlockSpec((B, None, ts, D), tile),
            pl.BlockSpec((B, None, ts, D), tile),
            pl.BlockSpec((B, 1, D), lambda i: (0, 0, 0)),
            pl.BlockSpec(sg.shape, const),
            pl.BlockSpec(su.shape, const),
            pl.BlockSpec(sd.shape, const),
            pl.BlockSpec((1, D), const),
        ],
        out_specs=pl.BlockSpec((B, None, ts, D), tile),
        out_shape=jax.ShapeDtypeStruct((B, nt, ts, D), F32),
        compiler_params=_cparams(("arbitrary",)),
        name="moe_combine",
    )(ys, tok(wc), rep, tok(h), tok(x1), tok(extra), gate2.reshape(B, 1, D), sg, su, sd,
      final_g.reshape(1, D))
    return out.reshape(N, D)


def _overflow_kernel(h_ref, wc_ref, wg_ref, wu_ref, wd_ref, o_ref):
    e = pl.program_id(1)

    @pl.when(e == 0)
    def _():
        o_ref[...] = jnp.zeros_like(o_ref)

    h = h_ref[...]
    lane = lax.broadcasted_iota(jnp.int32, wc_ref.shape, 1)
    w = jnp.sum(jnp.where(lane == e, wc_ref[...], 0.0), axis=1, keepdims=True)
    hid = _silu(jnp.dot(h, wg_ref[0], preferred_element_type=F32)) * jnp.dot(
        h, wu_ref[0], preferred_element_type=F32)
    o_ref[...] += _dot(hid * w, wd_ref[0])


def _overflow(h, wc_over, wg, wu, wd):
    N, D = h.shape
    tm = math.gcd(N, 1024)
    E, _, Hd = wg.shape
    row = lambda i, e: (i, 0)
    return pl.pallas_call(
        _overflow_kernel,
        grid=(N // tm, E),
        in_specs=[
            pl.BlockSpec((tm, D), row),
            pl.BlockSpec((tm, LANES), row),
            pl.BlockSpec((1, D, Hd), lambda i, e: (e, 0, 0)),
            pl.BlockSpec((1, D, Hd), lambda i, e: (e, 0, 0)),
            pl.BlockSpec((1, Hd, D), lambda i, e: (e, 0, 0)),
        ],
        out_specs=pl.BlockSpec((tm, D), row),
        out_shape=jax.ShapeDtypeStruct((N, D), F32),
        compiler_params=_cparams(("arbitrary", "arbitrary")),
        name="moe_overflow",
    )(h, wc_over, wg, wu, wd)


def _moe(h, wc, wg, wu, wd, sg, su, sd, x1, gate2, final_g, B, T, final_norm):
    xs, wc_over = _dispatch(h, wc, B, T)
    ys = _experts(xs, wg, wu, wd)
    extra = lax.cond(jnp.any(wc_over != 0.0),
                     lambda: _overflow(h, wc_over, wg, wu, wd),
                     lambda: jnp.zeros(x1.shape, F32))
    return _combine(ys, wc, h, x1, extra, gate2, sg, su, sd, final_g, B, T, final_norm)


def _split_w_in(w_in):
    W = WIDTH
    half = HEAD_DIM // 2
    j = np.arange(W)
    swap = (j // HEAD_DIM) * HEAD_DIM + (j % HEAD_DIM + half) % HEAD_DIM
    w_rwkv = w_in[:, :RWKV_COLS]
    w_ret = w_in[:, RWKV_COLS:RWKV_COLS + RET_COLS]
    w_pool = w_in[:, RWKV_COLS + RET_COLS:]
    q_sw = w_ret[:, 0:W][:, swap]
    k_sw = w_ret[:, W:2 * W][:, swap]
    w_ret_ext = jnp.concatenate([w_ret, q_sw, k_sw], axis=1)
    return w_rwkv.astype(BF16), w_ret_ext.astype(BF16), w_pool.astype(BF16)


def kernel(x, c, norm1_g, norm2_g, w_ada, b_ada, w_in, w_out, rwkv_mu, rwkv_w_up, rwkv_w0,
           rwkv_a_up, rwkv_a0, rwkv_g_up, rwkv_k_k, rwkv_k_a, rwkv_r_k, rwkv_lnx_w, rwkv_lnx_b,
           pool_w, pool_scale, w_router, router_bias, we_gate, we_up, we_down,
           ws_gate, ws_up, ws_down, final_g):
    B, T, D = x.shape
    L = w_in.shape[0]
    N = B * T
    mod = _adaln(c, w_ada, b_ada)
    x2 = x.reshape(N, D)
    for l in range(L):
        shift1, scale1, gate1, shift2, scale2, gate2 = jnp.split(mod[l], 6, axis=-1)
        w1, w2, w3 = _split_w_in(w_in[l])
        p_rwkv, p_ret, p_pool = _inproj(x2, norm1_g[l], scale1, shift1, w1, w2, w3, T)
        y_rwkv = _rwkv(p_rwkv.reshape(B, T, -1), rwkv_mu[l], rwkv_w_up[l], rwkv_w0[l],
                       rwkv_a_up[l], rwkv_a0[l], rwkv_g_up[l], rwkv_k_k[l], rwkv_k_a[l],
                       rwkv_r_k[l], rwkv_lnx_w[l], rwkv_lnx_b[l])
        y_ret = _retention(p_ret.reshape(B, T, -1))
        y_pool = _pool(p_pool.reshape(B, T, -1), pool_w[l], pool_scale[l])
        x1, h, wc = _outproj(y_rwkv.reshape(N, -1), y_ret.reshape(N, -1), y_pool.reshape(N, -1),
                             x2, w_out[l], gate1, norm2_g[l], scale2, shift2,
                             w_router[l], router_bias[l], T)
        x2 = _moe(h, wc, we_gate[l].astype(BF16), we_up[l].astype(BF16),
                  we_down[l].astype(BF16), ws_gate[l].astype(BF16), ws_up[l].astype(BF16),
                  ws_down[l].astype(BF16), x1, gate2, final_g, B, T,
                  final_norm=(l == L - 1))
    return x2.reshape(B, T, D)
```

```python
import functools
import math

import numpy as np
import jax
import jax.numpy as jnp
from jax import lax
from jax.experimental import pallas as pl
from jax.experimental.pallas import tpu as pltpu

F32 = jnp.float32
BF16 = jnp.bfloat16
HIGHEST = lax.Precision.HIGHEST

D_MODEL = 1024
HEADS = 6
HEAD_DIM = 64
WIDTH = HEADS * HEAD_DIM
W_LORA = 64
A_LORA = 64
G_LORA = 128
RWKV_COLS = 3 * WIDTH + W_LORA + A_LORA + G_LORA
RET_COLS = 4 * WIDTH
RET_EXT_COLS = 6 * WIDTH
RET_CHUNK = 128
RWKV_CHUNK = 128
RWKV_SUB = 16
RWKV_ROWS = 4
POOL_GROUPS = 4
POOL_GROUP_DIM = 64
POOL_WIDTH = POOL_GROUPS * POOL_GROUP_DIM
POOL_WINDOWS = (2, 4, 8, 16)
N_EXPERTS = 64
TOP_K = 8
N_GROUPS = 8
TOPK_GROUPS = 4
EXPERT_HIDDEN = 256
ROUTED_SCALE = 2.5
NORM_EPS = 1e-6
RWKV_LNX_EPS = 64e-5
RET_NORM_EPS = 1e-6
ROPE_BASE = 10000.0
LANES = 128
MOE_TILE_SEQ = 16
MOE_CAP = 80
VMEM_LIMIT = 48 * 1024 * 1024


def _cparams(sem):
    return pltpu.CompilerParams(dimension_semantics=sem, vmem_limit_bytes=VMEM_LIMIT)


def _dot(a, b):
    return jnp.dot(a.astype(BF16), b.astype(BF16), preferred_element_type=F32)


def _dot_nt(a, b):
    return lax.dot_general(a.astype(BF16), b.astype(BF16), (((1,), (1,)), ((), ())),
                           preferred_element_type=F32)


def _dot_tn(a, b):
    return lax.dot_general(a.astype(BF16), b.astype(BF16), (((0,), (0,)), ((), ())),
                           preferred_element_type=F32)


def _dot_f32(a, b):
    return jnp.dot(a, b, preferred_element_type=F32, precision=HIGHEST)


def _sigmoid(x):
    return 1.0 / (1.0 + jnp.exp(-x))


def _silu(x):
    return x * _sigmoid(x)


def _head_masks(rows, width):
    lane = lax.broadcasted_iota(jnp.int32, (rows, width), 1)
    return [lane // HEAD_DIM == h for h in range(width // HEAD_DIM)]


def _stack_heads(x, masks):
    return jnp.concatenate([jnp.where(m, x, 0.0) for m in masks], axis=0).astype(BF16)


def _select_heads(stacked, masks, c):
    out = stacked[0:c]
    for h in range(1, len(masks)):
        out = jnp.where(masks[h], stacked[h * c:(h + 1) * c], out)
    return out


def _adaln_kernel(c_ref, w_ref, b_ref, o_ref):
    o_ref[0] = _dot_f32(_silu(c_ref[...]), w_ref[0]) + b_ref[0]


def _adaln(c, w_ada, b_ada):
    L, D, M = w_ada.shape
    B = c.shape[0]
    tn = 1536
    return pl.pallas_call(
        _adaln_kernel,
        grid=(L, M // tn),
        in_specs=[
            pl.BlockSpec((B, D), lambda l, j: (0, 0)),
            pl.BlockSpec((1, D, tn), lambda l, j: (l, 0, j)),
            pl.BlockSpec((1, 1, tn), lambda l, j: (l, 0, j)),
        ],
        out_specs=pl.BlockSpec((1, B, tn), lambda l, j: (l, 0, j)),
        out_shape=jax.ShapeDtypeStruct((L, B, M), F32),
        compiler_params=_cparams(("arbitrary", "arbitrary")),
        name="adaln",
    )(c, w_ada, b_ada.reshape(L, 1, M))


def _modulated_norm(x, g, scale, shift):
    ms = jnp.mean(x * x, axis=-1, keepdims=True)
    return x * lax.rsqrt(ms + NORM_EPS) * g * (1.0 + scale) + shift


def _inproj_kernel(x_ref, g_ref, sc_ref, sh_ref, w1_ref, w2_ref, w3_ref, o1_ref, o2_ref, o3_ref):
    h = _modulated_norm(x_ref[...], g_ref[...], sc_ref[0], sh_ref[0]).astype(BF16)
    o1_ref[...] = jnp.dot(h, w1_ref[...], preferred_element_type=F32)
    o2_ref[...] = jnp.dot(h, w2_ref[...], preferred_element_type=F32)
    o3_ref[...] = jnp.dot(h, w3_ref[...], preferred_element_type=F32)


def _inproj(x2, g, scale, shift, w1, w2, w3, T):
    N, D = x2.shape
    B = N // T
    tm = min(256, T)
    per_b = T // tm
    row = lambda i: (i, 0)
    const = lambda i: (0, 0)
    bvec = lambda i: (i // per_b, 0, 0)
    return pl.pallas_call(
        _inproj_kernel,
        grid=(N // tm,),
        in_specs=[
            pl.BlockSpec((tm, D), row),
            pl.BlockSpec((1, D), const),
            pl.BlockSpec((1, 1, D), bvec),
            pl.BlockSpec((1, 1, D), bvec),
            pl.BlockSpec(w1.shape, const),
            pl.BlockSpec(w2.shape, const),
            pl.BlockSpec(w3.shape, const),
        ],
        out_specs=[
            pl.BlockSpec((tm, w1.shape[1]), row),
            pl.BlockSpec((tm, w2.shape[1]), row),
            pl.BlockSpec((tm, w3.shape[1]), row),
        ],
        out_shape=[
            jax.ShapeDtypeStruct((N, w1.shape[1]), F32),
            jax.ShapeDtypeStruct((N, w2.shape[1]), F32),
            jax.ShapeDtypeStruct((N, w3.shape[1]), F32),
        ],
        compiler_params=_cparams(("arbitrary",)),
        name="inproj",
    )(x2, g.reshape(1, D), scale.reshape(B, 1, D), shift.reshape(B, 1, D), w1, w2, w3)


def _unit_lower_inverse(a3):
    H, C, _ = a3.shape
    ri = lax.broadcasted_iota(jnp.int32, (H, C, C), 1)
    ci = lax.broadcasted_iota(jnp.int32, (H, C, C), 2)
    eye = (ri == ci).astype(F32)
    same = (ri // RWKV_SUB) == (ci // RWKV_SUB)
    dm = jnp.where(same, a3, 0.0)
    off = jnp.where(same, 0.0, a3)

    def bmm(x, y):
        return jnp.einsum('hij,hjk->hik', x.astype(BF16), y.astype(BF16),
                          preferred_element_type=F32)

    d2 = bmm(dm, dm)
    d4 = bmm(d2, d2)
    d8 = bmm(d4, d4)
    x = eye - dm
    x = x + bmm(x, d2)
    x = x + bmm(x, d4)
    x = x + bmm(x, d8)
    n = bmm(x, off)
    y = eye - n
    power = bmm(n, n)
    order = 2
    while order < C // RWKV_SUB:
        y = y + bmm(y, power)
        order *= 2
        if order < C // RWKV_SUB:
            power = bmm(power, power)
    return bmm(y, x)


def _rwkv_prepare(p, carry, prm):
    (mu, wup, w0, aup, a0, gup, k_k, k_a, r_k, lnw, lnb, hsum) = prm
    C = RWKV_CHUNK
    W = WIDTH
    H = HEADS
    row = lax.broadcasted_iota(jnp.int32, (C, 1), 0)
    prev = jnp.where(row == 0, carry, pltpu.roll(p, 1, 0))
    xs = p + (prev - p) * mu

    r = xs[:, 0:W]
    k = xs[:, W:2 * W]
    v = xs[:, 2 * W:3 * W]
    xwa = xs[:, 3 * W:3 * W + W_LORA + A_LORA]
    xg = xs[:, 3 * W + W_LORA + A_LORA:]

    z = w0 + _dot(jnp.tanh(xwa), wup)
    log_w = -math.exp(-0.5) * _sigmoid(z)
    a = _sigmoid(a0 + _dot(xwa, aup))
    g = _dot(_sigmoid(xg), gup)
    kk = k * k_k
    kk = kk / jnp.maximum(jnp.sqrt(_dot(kk * kk, hsum)), 1e-12)
    k = k * (1.0 + (a - 1.0) * k_a)

    ti = lax.broadcasted_iota(jnp.int32, (C, C), 0)
    si = lax.broadcasted_iota(jnp.int32, (C, C), 1)
    tri = jnp.where(ti >= si, 1.0, 0.0).astype(BF16)
    log_w_hi = log_w.astype(BF16)
    cum = (jnp.dot(tri, log_w_hi, preferred_element_type=F32)
           + _dot(tri, log_w - log_w_hi.astype(F32)))
    g_in = jnp.exp(cum)
    g_inv = jnp.exp(-cum)
    g_end = g_in[C - 1:C, :]
    kt = kk * jnp.exp(cum - log_w)
    bt = kk * a * g_inv
    kq = k * g_inv
    rt = r * g_in

    masks = _head_masks(C, W)
    t3 = lax.broadcasted_iota(jnp.int32, (H, C, C), 1)
    s3 = lax.broadcasted_iota(jnp.int32, (H, C, C), 2)
    a_ab = jnp.where(t3 > s3, _dot_nt(_stack_heads(kt, masks), bt).reshape(H, C, C), 0.0)
    bt_rows = _stack_heads(bt, masks)
    kq_rows = _stack_heads(kq, masks)
    t1 = lax.broadcasted_iota(jnp.int32, (C, H * C), 0)
    s1 = lax.broadcasted_iota(jnp.int32, (C, H * C), 1) % C
    a_ak = jnp.where(t1 > s1, _dot_nt(kt, kq_rows), 0.0)
    t2 = lax.broadcasted_iota(jnp.int32, (C, 2 * H * C), 0)
    s2 = lax.broadcasted_iota(jnp.int32, (C, 2 * H * C), 1) % C
    p_r = jnp.where(t2 >= s2, _dot_nt(rt, jnp.concatenate([bt_rows, kq_rows], axis=0)), 0.0)
    return a_ab, (a_ak, p_r, kt, rt, bt, kq, g_end, r, k, v, g)


def _rwkv_finish(vals, t_inv, s0, prm):
    (a_ak, p_r, kt, rt, bt, kq, g_end, r, k, v, g) = vals
    (mu, wup, w0, aup, a0, gup, k_k, k_a, r_k, lnw, lnb, hsum) = prm
    C = RWKV_CHUNK
    W = WIDTH
    H = HEADS
    masks = _head_masks(C, W)
    ks = _dot_nt(kt, s0)
    rs = _dot_nt(rt, s0)
    v_rows = _stack_heads(v, masks)
    av = _dot(a_ak, v_rows)
    u = _select_heads(_dot(t_inv.reshape(H * C, C), -(ks + av)), masks, C)
    y = rs + _dot(p_r, jnp.concatenate([_stack_heads(u, masks), v_rows], axis=0))
    upd = _dot_tn(jnp.concatenate([u, v], axis=0),
                  jnp.concatenate([bt * g_end, kq * g_end], axis=0))
    hi = lax.broadcasted_iota(jnp.int32, (W, W), 0) // HEAD_DIM
    hj = lax.broadcasted_iota(jnp.int32, (W, W), 1) // HEAD_DIM
    s_new = s0 * g_end + jnp.where(hi == hj, upd, 0.0)

    inv_d = 1.0 / HEAD_DIM
    mean = _dot(y, hsum) * inv_d
    yc = y - mean
    var = _dot(yc * yc, hsum) * inv_d
    yn = yc * lax.rsqrt(var + RWKV_LNX_EPS) * lnw + lnb
    bonus = _dot(r * k * r_k, hsum) * v
    return (yn + bonus) * g, s_new


def _rwkv_kernel(p_ref, mu_ref, wup_ref, w0_ref, aup_ref, a0_ref, gup_ref, kk_ref, ka_ref,
                 rk_ref, lnw_ref, lnb_ref, hsum_ref, o_ref, carry_ref, s_ref):
    @pl.when(pl.program_id(1) == 0)
    def _():
        carry_ref[...] = jnp.zeros_like(carry_ref)
        s_ref[...] = jnp.zeros_like(s_ref)

    prm = tuple(ref[...] for ref in (mu_ref, wup_ref, w0_ref, aup_ref, a0_ref, gup_ref, kk_ref,
                                     ka_ref, rk_ref, lnw_ref, lnb_ref, hsum_ref))
    G = p_ref.shape[0]
    H = HEADS
    a_abs, vals = [], []
    for i in range(G):
        p = p_ref[i]
        a_ab, val = _rwkv_prepare(p, carry_ref[i], prm)
        carry_ref[i] = p[RWKV_CHUNK - 1:RWKV_CHUNK, :]
        a_abs.append(a_ab)
        vals.append(val)
    t_inv = _unit_lower_inverse(jnp.concatenate(a_abs, axis=0))
    for i in range(G):
        out, s_new = _rwkv_finish(vals[i], t_inv[i * H:(i + 1) * H], s_ref[i], prm)
        s_ref[i] = s_new
        o_ref[i] = out


def _rwkv(p, mu, wup, w0, aup, a0, gup, k_k, k_a, r_k, lnx_w, lnx_b):
    B, T, _ = p.shape
    C = RWKV_CHUNK
    W = WIDTH
    G = RWKV_ROWS if B % RWKV_ROWS == 0 else 1
    lora_in = W_LORA + A_LORA
    wup_pad = jnp.zeros((lora_in, W), F32).at[:W_LORA].set(wup).astype(BF16)
    aup_pad = jnp.zeros((lora_in, W), F32).at[W_LORA:].set(aup).astype(BF16)
    head = np.arange(W) // HEAD_DIM
    hsum = jnp.asarray((head[:, None] == head[None, :]).astype(np.float32)).astype(BF16)
    vec = lambda a: a.reshape(1, -1)
    const = lambda b, t: (0, 0)
    params = [vec(mu), wup_pad, vec(w0), aup_pad, vec(a0), gup.astype(BF16), vec(k_k), vec(k_a),
              vec(r_k), vec(lnx_w), vec(lnx_b), hsum]
    return pl.pallas_call(
        _rwkv_kernel,
        grid=(B // G, T // C),
        in_specs=[pl.BlockSpec((G, C, RWKV_COLS), lambda b, t: (b, t, 0))]
        + [pl.BlockSpec(a.shape, const) for a in params],
        out_specs=pl.BlockSpec((G, C, W), lambda b, t: (b, t, 0)),
        out_shape=jax.ShapeDtypeStruct((B, T, W), F32),
        scratch_shapes=[pltpu.VMEM((G, 1, RWKV_COLS), F32), pltpu.VMEM((G, W, W), F32)],
        compiler_params=_cparams(("arbitrary", "arbitrary")),
        name="rwkv7",
    )(p, *params)


def _retention_kernel(p_ref, cos_ref, sin_ref, dec_ref, xi_ref, zeta_ref, cd_ref, hsum_ref,
                      o_ref, s_ref):
    C = RET_CHUNK
    W = WIDTH

    @pl.when(pl.program_id(1) == 0)
    def _():
        s_ref[...] = jnp.zeros_like(s_ref)

    p = p_ref[0]
    cos = cos_ref[...]
    sin = sin_ref[...]
    q = p[:, 0:W] * cos + p[:, 4 * W:5 * W] * sin
    k = (p[:, W:2 * W] * cos + p[:, 5 * W:6 * W] * sin) * (HEAD_DIM ** -0.5)
    v = p[:, 2 * W:3 * W]
    gate = p[:, 3 * W:4 * W]

    masks = _head_masks(C, W)
    scores = _dot_nt(q, _stack_heads(k, masks)) * dec_ref[...]
    y = _dot(scores, _stack_heads(v, masks))
    s0 = s_ref[...]
    y = y + _dot(q * xi_ref[...], s0)
    kv = _dot_tn(k * zeta_ref[...], v)
    hi = lax.broadcasted_iota(jnp.int32, (W, W), 0) // HEAD_DIM
    hj = lax.broadcasted_iota(jnp.int32, (W, W), 1) // HEAD_DIM
    s_ref[...] = s0 * cd_ref[...] + jnp.where(hi == hj, kv, 0.0)

    ms = _dot(y * y, hsum_ref[...]) * (1.0 / HEAD_DIM)
    o_ref[0] = _silu(gate) * (y * lax.rsqrt(ms + RET_NORM_EPS))


def _retention_tables(T):
    C, H, d = RET_CHUNK, HEADS, HEAD_DIM
    pos = jnp.arange(T, dtype=F32)
    inv_freq = ROPE_BASE ** (-jnp.arange(0, d, 2, dtype=F32) / d)
    ang = pos[:, None] * inv_freq[None, :]
    cos = jnp.cos(ang)
    sin = jnp.sin(ang)
    cos_full = jnp.tile(jnp.concatenate([cos, cos], -1), (1, H))
    sin_full = jnp.tile(jnp.concatenate([-sin, sin], -1), (1, H))
    log_gamma = jnp.log1p(-(2.0 ** (-5.0 - jnp.arange(H, dtype=F32))))
    idx = jnp.arange(C, dtype=F32)
    diff = idx[:, None] - idx[None, :]
    dec = jnp.where(diff >= 0, jnp.exp(log_gamma[:, None, None] * jnp.maximum(diff, 0.0)), 0.0)
    xi = jnp.exp(log_gamma[:, None] * (idx + 1.0))
    zeta = jnp.exp(log_gamma[:, None] * (C - 1.0 - idx))
    cd = jnp.exp(log_gamma * C)
    per_lane = lambda a: jnp.repeat(a.T, d, axis=1)
    return (cos_full, sin_full, dec.transpose(1, 0, 2).reshape(C, H * C), per_lane(xi), per_lane(zeta),
            jnp.repeat(cd, d).reshape(1, H * d))


def _retention(p):
    B, T, cols = p.shape
    C = RET_CHUNK
    W = WIDTH
    cos, sin, dec, xi, zeta, cd = _retention_tables(T)
    head = np.arange(W) // HEAD_DIM
    hsum = jnp.asarray((head[:, None] == head[None, :]).astype(np.float32)).astype(BF16)
    const = lambda b, t: (0, 0)
    return pl.pallas_call(
        _retention_kernel,
        grid=(B, T // C),
        in_specs=[
            pl.BlockSpec((1, C, cols), lambda b, t: (b, t, 0)),
            pl.BlockSpec((C, W), lambda b, t: (t, 0)),
            pl.BlockSpec((C, W), lambda b, t: (t, 0)),
            pl.BlockSpec(dec.shape, const),
            pl.BlockSpec(xi.shape, const),
            pl.BlockSpec(zeta.shape, const),
            pl.BlockSpec(cd.shape, const),
            pl.BlockSpec(hsum.shape, const),
        ],
        out_specs=pl.BlockSpec((1, C, W), lambda b, t: (b, t, 0)),
        out_shape=jax.ShapeDtypeStruct((B, T, W), F32),
        scratch_shapes=[pltpu.VMEM((W, W), F32)],
        compiler_params=_cparams(("arbitrary", "arbitrary")),
        name="retention",
    )(p, cos, sin, dec, xi, zeta, cd, hsum)


def _pool_kernel(u_ref, w_ref, scale_ref, o_ref):
    u = u_ref[0]
    T = u.shape[0]
    row = lax.broadcasted_iota(jnp.int32, (T, 1), 0)

    def lag(x, k):
        return jnp.where(row >= k, pltpu.roll(x, k, 0), 0.0)

    s2 = u + lag(u, 1)
    s4 = s2 + lag(s2, 2)
    s8 = s4 + lag(s4, 4)
    s16 = s8 + lag(s8, 8)
    grp = lax.broadcasted_iota(jnp.int32, (1, POOL_WIDTH), 1) // POOL_GROUP_DIM
    s = jnp.where(grp == 0, s2, jnp.where(grp == 1, s4, jnp.where(grp == 2, s8, s16)))
    win = jnp.where(grp == 0, POOL_WINDOWS[0],
                    jnp.where(grp == 1, POOL_WINDOWS[1],
                              jnp.where(grp == 2, POOL_WINDOWS[2], POOL_WINDOWS[3])))
    count = jnp.minimum(row + 1, win).astype(F32)
    pooled = s / count - u
    o_ref[0] = _dot(pooled, w_ref[...]) * scale_ref[...]


def _pool(u, pool_w, pool_scale):
    B, T, Wp = u.shape
    G, d = POOL_GROUPS, POOL_GROUP_DIM
    wbd = jnp.zeros((Wp, Wp), F32)
    for gi in range(G):
        wbd = wbd.at[gi * d:(gi + 1) * d, gi * d:(gi + 1) * d].set(pool_w[gi])
    return pl.pallas_call(
        _pool_kernel,
        grid=(B,),
        in_specs=[
            pl.BlockSpec((1, T, Wp), lambda b: (b, 0, 0)),
            pl.BlockSpec((Wp, Wp), lambda b: (0, 0)),
            pl.BlockSpec((1, Wp), lambda b: (0, 0)),
        ],
        out_specs=pl.BlockSpec((1, T, Wp), lambda b: (b, 0, 0)),
        out_shape=jax.ShapeDtypeStruct((B, T, Wp), F32),
        compiler_params=_cparams(("arbitrary",)),
        name="pool",
    )(u, wbd.astype(BF16), pool_scale.reshape(1, Wp))


def _route(logits_t, bias_col):
    E, tm = logits_t.shape
    per_group = E // N_GROUPS
    neg_inf = -jnp.inf
    scores = _sigmoid(logits_t)
    choice = scores + bias_col
    c3 = choice.reshape(N_GROUPS, per_group, tm)
    sub = lax.broadcasted_iota(jnp.int32, c3.shape, 1)
    m1 = jnp.max(c3, axis=1, keepdims=True)
    first = jnp.min(jnp.where(c3 == m1, sub, per_group), axis=1, keepdims=True)
    m2 = jnp.max(jnp.where(sub == first, neg_inf, c3), axis=1, keepdims=True)
    gs = m1 + m2
    gidx = lax.broadcasted_iota(jnp.int32, gs.shape, 0)
    grank = jnp.zeros(gs.shape, jnp.int32)
    for j in range(N_GROUPS):
        other = gs[j:j + 1]
        ahead = jnp.where(other > gs, 1, jnp.where((other == gs) & (gidx > j), 1, 0))
        grank = grank + ahead
    gmask = jnp.broadcast_to(grank < TOPK_GROUPS, c3.shape)
    masked = jnp.where(gmask, c3, neg_inf).reshape(E, tm)
    eidx = lax.broadcasted_iota(jnp.int32, (E, tm), 0)
    top = jnp.zeros((E, tm), F32)
    for _ in range(TOP_K):
        best = jnp.max(masked, axis=0, keepdims=True)
        first = jnp.min(jnp.where(masked == best, eidx, E), axis=0, keepdims=True)
        hit = eidx == first
        top = jnp.where(hit, scores, top)
        masked = jnp.where(hit, neg_inf, masked)
    return top / jnp.sum(top, axis=0, keepdims=True) * ROUTED_SCALE


def _outproj_kernel(yr_ref, yt_ref, yp_ref, x_ref, w1_ref, w2_ref, w3_ref, g1_ref, ng_ref,
                    sc_ref, sh_ref, wr_ref, rb_ref, x1_ref, h_ref, wc_ref):
    mixed = (_dot(yr_ref[...], w1_ref[...]) + _dot(yt_ref[...], w2_ref[...])
             + _dot(yp_ref[...], w3_ref[...]))
    x1 = x_ref[...] + g1_ref[0] * mixed
    x1_ref[...] = x1
    h = _modulated_norm(x1, ng_ref[...], sc_ref[0], sh_ref[0])
    h_ref[...] = h.astype(BF16)
    logits_t = lax.dot_general(wr_ref[...], h, (((1,), (1,)), ((), ())),
                               preferred_element_type=F32, precision=HIGHEST)
    wc_t = _route(logits_t, rb_ref[...])
    pad = jnp.zeros((LANES - N_EXPERTS, wc_t.shape[1]), F32)
    wc_ref[...] = jnp.concatenate([wc_t, pad], axis=0).T


def _outproj(yr, yt, yp, x2, w_out, gate1, ng, scale2, shift2, w_router, router_bias, T):
    N, D = x2.shape
    B = N // T
    tm = min(256, T)
    per_b = T // tm
    W = WIDTH
    w1 = w_out[:W].astype(BF16)
    w2 = w_out[W:2 * W].astype(BF16)
    w3 = w_out[2 * W:].astype(BF16)
    row = lambda i: (i, 0)
    const = lambda i: (0, 0)
    bvec = lambda i: (i // per_b, 0, 0)
    return pl.pallas_call(
        _outproj_kernel,
        grid=(N // tm,),
        in_specs=[
            pl.BlockSpec((tm, W), row),
            pl.BlockSpec((tm, W), row),
            pl.BlockSpec((tm, POOL_WIDTH), row),
            pl.BlockSpec((tm, D), row),
            pl.BlockSpec(w1.shape, const),
            pl.BlockSpec(w2.shape, const),
            pl.BlockSpec(w3.shape, const),
            pl.BlockSpec((1, 1, D), bvec),
            pl.BlockSpec((1, D), const),
            pl.BlockSpec((1, 1, D), bvec),
            pl.BlockSpec((1, 1, D), bvec),
            pl.BlockSpec((N_EXPERTS, D), const),
            pl.BlockSpec((N_EXPERTS, 1), const),
        ],
        out_specs=[
            pl.BlockSpec((tm, D), row),
            pl.BlockSpec((tm, D), row),
            pl.BlockSpec((tm, LANES), row),
        ],
        out_shape=[
            jax.ShapeDtypeStruct((N, D), F32),
            jax.ShapeDtypeStruct((N, D), BF16),
            jax.ShapeDtypeStruct((N, LANES), F32),
        ],
        compiler_params=_cparams(("arbitrary",)),
        name="outproj_router",
    )(yr, yt, yp, x2, w1, w2, w3, gate1.reshape(B, 1, D), ng.reshape(1, D),
      scale2.reshape(B, 1, D), shift2.reshape(B, 1, D), w_router.T,
      router_bias.reshape(N_EXPERTS, 1))


def _slot_positions(wc):
    tm = wc.shape[0]
    sel = jnp.where(wc > 0.0, 1.0, 0.0).astype(BF16)
    ti = lax.broadcasted_iota(jnp.int32, (tm, tm), 0)
    si = lax.broadcasted_iota(jnp.int32, (tm, tm), 1)
    earlier = jnp.where(ti > si, 1.0, 0.0).astype(BF16)
    return sel, jnp.dot(earlier, sel, preferred_element_type=F32)


def _dispatch_kernel(h_ref, wc_ref, rept_ref, xs_ref, over_ref):
    nb, ts, D = h_ref.shape
    tm = nb * ts
    wc = wc_ref[...].reshape(tm, LANES)
    sel, pos = _slot_positions(wc)
    over_ref[...] = jnp.where(pos >= MOE_CAP, wc, 0.0).reshape(nb, ts, LANES)
    rept = rept_ref[...]
    pos_rows = _dot_nt(rept, pos)
    sel_rows = _dot_nt(rept, sel)
    slot = (lax.broadcasted_iota(jnp.int32, (rept.shape[0], 1), 0) % MOE_CAP).astype(F32)
    onehot = jnp.where(pos_rows == slot, sel_rows, 0.0).astype(BF16)
    xs_ref[...] = jnp.dot(onehot, h_ref[...].reshape(tm, D),
                          preferred_element_type=F32).astype(BF16)


def _dispatch(h, wc, B, T):
    N, D = h.shape
    ts = MOE_TILE_SEQ
    nt = T // ts
    L = N_EXPERTS * MOE_CAP
    rows = np.arange(L) // MOE_CAP
    rept = jnp.asarray((rows[:, None] == np.arange(LANES)[None, :]).astype(np.float32)).astype(BF16)
    tile = lambda i: (0, i, 0, 0)
    xs, over = pl.pallas_call(
        _dispatch_kernel,
        grid=(nt,),
        in_specs=[
            pl.BlockSpec((B, None, ts, D), tile),
            pl.BlockSpec((B, None, ts, LANES), tile),
            pl.BlockSpec((L, LANES), lambda i: (0, 0)),
        ],
        out_specs=[
            pl.BlockSpec((None, L, D), lambda i: (i, 0, 0)),
            pl.BlockSpec((B, None, ts, LANES), tile),
        ],
        out_shape=[
            jax.ShapeDtypeStruct((nt, L, D), BF16),
            jax.ShapeDtypeStruct((B, nt, ts, LANES), F32),
        ],
        compiler_params=_cparams(("arbitrary",)),
        name="moe_dispatch",
    )(h.reshape(B, nt, ts, D), wc.reshape(B, nt, ts, LANES), rept)
    return xs, over.reshape(N, LANES)


def _experts_kernel(x_ref, wg_ref, wu_ref, wd_ref, o_ref):
    tg, cap, D = x_ref.shape
    x = x_ref[...].reshape(tg * cap, D)
    hid = _silu(jnp.dot(x, wg_ref[0], preferred_element_type=F32)) * jnp.dot(
        x, wu_ref[0], preferred_element_type=F32)
    o_ref[...] = _dot(hid, wd_ref[0]).astype(BF16).reshape(tg, cap, D)


def _experts(xs, wg, wu, wd):
    nt, L, D = xs.shape
    E, _, Hd = wg.shape
    cap = L // E
    tg = math.gcd(nt, 16)
    slots = pl.BlockSpec((tg, None, cap, D), lambda e, g: (g, e, 0, 0))
    return pl.pallas_call(
        _experts_kernel,
        grid=(E, nt // tg),
        in_specs=[
            slots,
            pl.BlockSpec((1, D, Hd), lambda e, g: (e, 0, 0)),
            pl.BlockSpec((1, D, Hd), lambda e, g: (e, 0, 0)),
            pl.BlockSpec((1, Hd, D), lambda e, g: (e, 0, 0)),
        ],
        out_specs=slots,
        out_shape=jax.ShapeDtypeStruct((nt, E, cap, D), BF16),
        compiler_params=_cparams(("arbitrary", "arbitrary")),
        name="moe_experts",
    )(xs.reshape(nt, E, cap, D), wg, wu, wd).reshape(nt, L, D)


def _combine_kernel(y_ref, wc_ref, rep_ref, h_ref, x1_ref, ex_ref, g2_ref, sg_ref, su_ref, sd_ref,
                    fg_ref, o_ref, *, final_norm):
    nb, ts, D = h_ref.shape
    tm = nb * ts
    wc = wc_ref[...].reshape(tm, LANES)
    _, pos = _slot_positions(wc)
    rep = rep_ref[...]
    pos_lanes = _dot(pos, rep)
    w_lanes = _dot(wc, rep)
    slot = (lax.broadcasted_iota(jnp.int32, (1, rep.shape[1]), 1) % MOE_CAP).astype(F32)
    weighted = jnp.where(pos_lanes == slot, w_lanes, 0.0).astype(BF16)
    routed = jnp.dot(weighted, y_ref[...], preferred_element_type=F32)
    h = h_ref[...].reshape(tm, D)
    hid = _silu(jnp.dot(h, sg_ref[...], preferred_element_type=F32)) * jnp.dot(
        h, su_ref[...], preferred_element_type=F32)
    y = routed + ex_ref[...].reshape(tm, D) + _dot(hid, sd_ref[...])
    gate = jnp.broadcast_to(g2_ref[...], (nb, ts, D)).reshape(tm, D)
    xo = x1_ref[...].reshape(tm, D) + gate * y
    if final_norm:
        ms = jnp.mean(xo * xo, axis=-1, keepdims=True)
        xo = xo * lax.rsqrt(ms + NORM_EPS) * fg_ref[...]
    o_ref[...] = xo.reshape(nb, ts, D)


def _combine(ys, wc, h, x1, extra, gate2, sg, su, sd, final_g, B, T, final_norm):
    N, D = x1.shape
    ts = MOE_TILE_SEQ
    nt = T // ts
    L = N_EXPERTS * MOE_CAP
    cols = np.arange(L) // MOE_CAP
    rep = jnp.asarray((np.arange(LANES)[:, None] == cols[None, :]).astype(np.float32)).astype(BF16)
    tile = lambda i: (0, i, 0, 0)
    const = lambda i: (0, 0)
    tok = lambda a: a.reshape(B, nt, ts, a.shape[-1])
    out = pl.pallas_call(
        functools.partial(_combine_kernel, final_norm=final_norm),
        grid=(nt,),
        in_specs=[
            pl.BlockSpec((None, L, D), lambda i: (i, 0, 0)),
            pl.BlockSpec((B, None, ts, LANES), tile),
            pl.BlockSpec((LANES, L), const),
            pl.BlockSpec((B, None, ts, D), tile),
            pl.BlockSpec((B, None, ts, D), tile),
            pl.BlockSpec((B, None, ts, D), tile),
            pl.BlockSpec((B, 1, D), lambda i: (0, 0, 0)),
            pl.BlockSpec(sg.shape, const),
            pl.BlockSpec(su.shape, const),
            pl.BlockSpec(sd.shape, const),
            pl.BlockSpec((1, D), const),
        ],
        out_specs=pl.BlockSpec((B, None, ts, D), tile),
        out_shape=jax.ShapeDtypeStruct((B, nt, ts, D), F32),
        compiler_params=_cparams(("arbitrary",)),
        name="moe_combine",
    )(ys, tok(wc), rep, tok(h), tok(x1), tok(extra), gate2.reshape(B, 1, D), sg, su, sd,
      final_g.reshape(1, D))
    return out.reshape(N, D)


def _overflow_kernel(h_ref, wc_ref, wg_ref, wu_ref, wd_ref, o_ref):
    e = pl.program_id(1)

    @pl.when(e == 0)
    def _():
        o_ref[...] = jnp.zeros_like(o_ref)

    h = h_ref[...]
    lane = lax.broadcasted_iota(jnp.int32, wc_ref.shape, 1)
    w = jnp.sum(jnp.where(lane == e, wc_ref[...], 0.0), axis=1, keepdims=True)
    hid = _silu(jnp.dot(h, wg_ref[0], preferred_element_type=F32)) * jnp.dot(
        h, wu_ref[0], preferred_element_type=F32)
    o_ref[...] += _dot(hid * w, wd_ref[0])


def _overflow(h, wc_over, wg, wu, wd):
    N, D = h.shape
    tm = math.gcd(N, 1024)
    E, _, Hd = wg.shape
    row = lambda i, e: (i, 0)
    return pl.pallas_call(
        _overflow_kernel,
        grid=(N // tm, E),
        in_specs=[
            pl.BlockSpec((tm, D), row),
            pl.BlockSpec((tm, LANES), row),
            pl.BlockSpec((1, D, Hd), lambda i, e: (e, 0, 0)),
            pl.BlockSpec((1, D, Hd), lambda i, e: (e, 0, 0)),
            pl.BlockSpec((1, Hd, D), lambda i, e: (e, 0, 0)),
        ],
        out_specs=pl.BlockSpec((tm, D), row),
        out_shape=jax.ShapeDtypeStruct((N, D), F32),
        compiler_params=_cparams(("arbitrary", "arbitrary")),
        name="moe_overflow",
    )(h, wc_over, wg, wu, wd)


def _moe(h, wc, wg, wu, wd, sg, su, sd, x1, gate2, final_g, B, T, final_norm):
    xs, wc_over = _dispatch(h, wc, B, T)
    ys = _experts(xs, wg, wu, wd)
    extra = lax.cond(jnp.any(wc_over != 0.0),
                     lambda: _overflow(h, wc_over, wg, wu, wd),
                     lambda: jnp.zeros(x1.shape, F32))
    return _combine(ys, wc, h, x1, extra, gate2, sg, su, sd, final_g, B, T, final_norm)


def _split_w_in(w_in):
    W = WIDTH
    half = HEAD_DIM // 2
    j = np.arange(W)
    swap = (j // HEAD_DIM) * HEAD_DIM + (j % HEAD_DIM + half) % HEAD_DIM
    w_rwkv = w_in[:, :RWKV_COLS]
    w_ret = w_in[:, RWKV_COLS:RWKV_COLS + RET_COLS]
    w_pool = w_in[:, RWKV_COLS + RET_COLS:]
    q_sw = w_ret[:, 0:W][:, swap]
    k_sw = w_ret[:, W:2 * W][:, swap]
    w_ret_ext = jnp.concatenate([w_ret, q_sw, k_sw], axis=1)
    return w_rwkv.astype(BF16), w_ret_ext.astype(BF16), w_pool.astype(BF16)


def kernel(x, c, norm1_g, norm2_g, w_ada, b_ada, w_in, w_out, rwkv_mu, rwkv_w_up, rwkv_w0,
           rwkv_a_up, rwkv_a0, rwkv_g_up, rwkv_k_k, rwkv_k_a, rwkv_r_k, rwkv_lnx_w, rwkv_lnx_b,
           pool_w, pool_scale, w_router, router_bias, we_gate, we_up, we_down,
           ws_gate, ws_up, ws_down, final_g):
    B, T, D = x.shape
    L = w_in.shape[0]
    N = B * T
    mod = _adaln(c, w_ada, b_ada)
    x2 = x.reshape(N, D)
    for l in range(L):
        shift1, scale1, gate1, shift2, scale2, gate2 = jnp.split(mod[l], 6, axis=-1)
        w1, w2, w3 = _split_w_in(w_in[l])
        p_rwkv, p_ret, p_pool = _inproj(x2, norm1_g[l], scale1, shift1, w1, w2, w3, T)
        y_rwkv = _rwkv(p_rwkv.reshape(B, T, -1), rwkv_mu[l], rwkv_w_up[l], rwkv_w0[l],
                       rwkv_a_up[l], rwkv_a0[l], rwkv_g_up[l], rwkv_k_k[l], rwkv_k_a[l],
                       rwkv_r_k[l], rwkv_lnx_w[l], rwkv_lnx_b[l])
        y_ret = _retention(p_ret.reshape(B, T, -1))
        y_pool = _pool(p_pool.reshape(B, T, -1), pool_w[l], pool_scale[l])
        x1, h, wc = _outproj(y_rwkv.reshape(N, -1), y_ret.reshape(N, -1), y_pool.reshape(N, -1),
                             x2, w_out[l], gate1, norm2_g[l], scale2, shift2,
                             w_router[l], router_bias[l], T)
        x2 = _moe(h, wc, we_gate[l].astype(BF16), we_up[l].astype(BF16),
                  we_down[l].astype(BF16), ws_gate[l].astype(BF16), ws_up[l].astype(BF16),
                  ws_down[l].astype(BF16), x1, gate2, final_g, B, T,
                  final_norm=(l == L - 1))
    return x2.reshape(B, T, D)
```

```python
import functools
import math

import numpy as np
import jax
import jax.numpy as jnp
from jax import lax
from jax.experimental import pallas as pl
from jax.experimental.pallas import tpu as pltpu

F32 = jnp.float32
BF16 = jnp.bfloat16
HIGHEST = lax.Precision.HIGHEST

D_MODEL = 1024
HEADS = 6
HEAD_DIM = 64
WIDTH = HEADS * HEAD_DIM
W_LORA = 64
A_LORA = 64
G_LORA = 128
RWKV_COLS = 3 * WIDTH + W_LORA + A_LORA + G_LORA
RET_COLS = 4 * WIDTH
RET_EXT_COLS = 6 * WIDTH
RET_CHUNK = 128
RWKV_CHUNK = 128
RWKV_SUB = 16
RET_ROWS = 4
RWKV_ROWS = 4
POOL_GROUPS = 4
POOL_GROUP_DIM = 64
POOL_WIDTH = POOL_GROUPS * POOL_GROUP_DIM
POOL_WINDOWS = (2, 4, 8, 16)
N_EXPERTS = 64
TOP_K = 8
N_GROUPS = 8
TOPK_GROUPS = 4
EXPERT_HIDDEN = 256
ROUTED_SCALE = 2.5
NORM_EPS = 1e-6
RWKV_LNX_EPS = 64e-5
RET_NORM_EPS = 1e-6
ROPE_BASE = 10000.0
LANES = 128
MOE_TILE_SEQ = 16
MOE_CAP = 80
VMEM_LIMIT = 48 * 1024 * 1024


def _cparams(sem):
    return pltpu.CompilerParams(dimension_semantics=sem, vmem_limit_bytes=VMEM_LIMIT)


def _dot(a, b):
    return jnp.dot(a.astype(BF16), b.astype(BF16), preferred_element_type=F32)


def _dot_nt(a, b):
    return lax.dot_general(a.astype(BF16), b.astype(BF16), (((1,), (1,)), ((), ())),
                           preferred_element_type=F32)


def _dot_tn(a, b):
    return lax.dot_general(a.astype(BF16), b.astype(BF16), (((0,), (0,)), ((), ())),
                           preferred_element_type=F32)


def _dot_f32(a, b):
    return jnp.dot(a, b, preferred_element_type=F32, precision=HIGHEST)


def _sigmoid(x):
    return 1.0 / (1.0 + jnp.exp(-x))


def _silu(x):
    return x * _sigmoid(x)


def _head_masks(rows, width):
    lane = lax.broadcasted_iota(jnp.int32, (rows, width), 1)
    return [lane // HEAD_DIM == h for h in range(width // HEAD_DIM)]


def _stack_heads(x, masks):
    return jnp.concatenate([jnp.where(m, x, 0.0) for m in masks], axis=0).astype(BF16)


def _select_heads(stacked, masks, c):
    out = stacked[0:c]
    for h in range(1, len(masks)):
        out = jnp.where(masks[h], stacked[h * c:(h + 1) * c], out)
    return out


def _adaln_kernel(c_ref, w_ref, b_ref, o_ref):
    o_ref[0] = _dot_f32(_silu(c_ref[...]), w_ref[0]) + b_ref[0]


def _adaln(c, w_ada, b_ada):
    L, D, M = w_ada.shape
    B = c.shape[0]
    tn = 1536
    return pl.pallas_call(
        _adaln_kernel,
        grid=(L, M // tn),
        in_specs=[
            pl.BlockSpec((B, D), lambda l, j: (0, 0)),
            pl.BlockSpec((1, D, tn), lambda l, j: (l, 0, j)),
            pl.BlockSpec((1, 1, tn), lambda l, j: (l, 0, j)),
        ],
        out_specs=pl.BlockSpec((1, B, tn), lambda l, j: (l, 0, j)),
        out_shape=jax.ShapeDtypeStruct((L, B, M), F32),
        compiler_params=_cparams(("arbitrary", "arbitrary")),
        name="adaln",
    )(c, w_ada, b_ada.reshape(L, 1, M))


def _modulated_norm(x, g, scale, shift):
    ms = jnp.mean(x * x, axis=-1, keepdims=True)
    return x * lax.rsqrt(ms + NORM_EPS) * g * (1.0 + scale) + shift


def _inproj_kernel(x_ref, g_ref, sc_ref, sh_ref, w1_ref, w2_ref, w3_ref, o1_ref, o2_ref, o3_ref):
    h = _modulated_norm(x_ref[...], g_ref[...], sc_ref[0], sh_ref[0]).astype(BF16)
    o1_ref[...] = jnp.dot(h, w1_ref[...], preferred_element_type=F32)
    o2_ref[...] = jnp.dot(h, w2_ref[...], preferred_element_type=F32)
    o3_ref[...] = jnp.dot(h, w3_ref[...], preferred_element_type=F32)


def _inproj(x2, g, scale, shift, w1, w2, w3, T):
    N, D = x2.shape
    B = N // T
    tm = min(256, T)
    per_b = T // tm
    row = lambda i: (i, 0)
    const = lambda i: (0, 0)
    bvec = lambda i: (i // per_b, 0, 0)
    return pl.pallas_call(
        _inproj_kernel,
        grid=(N // tm,),
        in_specs=[
            pl.BlockSpec((tm, D), row),
            pl.BlockSpec((1, D), const),
            pl.BlockSpec((1, 1, D), bvec),
            pl.BlockSpec((1, 1, D), bvec),
            pl.BlockSpec(w1.shape, const),
            pl.BlockSpec(w2.shape, const),
            pl.BlockSpec(w3.shape, const),
        ],
        out_specs=[
            pl.BlockSpec((tm, w1.shape[1]), row),
            pl.BlockSpec((tm, w2.shape[1]), row),
            pl.BlockSpec((tm, w3.shape[1]), row),
        ],
        out_shape=[
            jax.ShapeDtypeStruct((N, w1.shape[1]), F32),
            jax.ShapeDtypeStruct((N, w2.shape[1]), F32),
            jax.ShapeDtypeStruct((N, w3.shape[1]), F32),
        ],
        compiler_params=_cparams(("arbitrary",)),
        name="inproj",
    )(x2, g.reshape(1, D), scale.reshape(B, 1, D), shift.reshape(B, 1, D), w1, w2, w3)


def _unit_lower_inverse(a3):
    H, C, _ = a3.shape
    ri = lax.broadcasted_iota(jnp.int32, (H, C, C), 1)
    ci = lax.broadcasted_iota(jnp.int32, (H, C, C), 2)
    eye = (ri == ci).astype(F32)
    same = (ri // RWKV_SUB) == (ci // RWKV_SUB)
    dm = jnp.where(same, a3, 0.0)
    off = jnp.where(same, 0.0, a3)

    def bmm(x, y):
        return jnp.einsum('hij,hjk->hik', x.astype(BF16), y.astype(BF16),
                          preferred_element_type=F32)

    d2 = bmm(dm, dm)
    d4 = bmm(d2, d2)
    d8 = bmm(d4, d4)
    x = eye - dm
    x = x + bmm(x, d2)
    x = x + bmm(x, d4)
    x = x + bmm(x, d8)
    n = bmm(x, off)
    y = eye - n
    power = bmm(n, n)
    order = 2
    while order < C // RWKV_SUB:
        y = y + bmm(y, power)
        order *= 2
        if order < C // RWKV_SUB:
            power = bmm(power, power)
    return bmm(y, x)


def _rwkv_prepare(p, carry, prm):
    (mu, wup, w0, aup, a0, gup, k_k, k_a, r_k, lnw, lnb, hsum) = prm
    C = RWKV_CHUNK
    W = WIDTH
    H = HEADS
    row = lax.broadcasted_iota(jnp.int32, (C, 1), 0)
    prev = jnp.where(row == 0, carry, pltpu.roll(p, 1, 0))
    xs = p + (prev - p) * mu

    r = xs[:, 0:W]
    k = xs[:, W:2 * W]
    v = xs[:, 2 * W:3 * W]
    xwa = xs[:, 3 * W:3 * W + W_LORA + A_LORA]
    xg = xs[:, 3 * W + W_LORA + A_LORA:]

    z = w0 + _dot(jnp.tanh(xwa), wup)
    log_w = -math.exp(-0.5) * _sigmoid(z)
    a = _sigmoid(a0 + _dot(xwa, aup))
    g = _dot(_sigmoid(xg), gup)
    kk = k * k_k
    kk = kk / jnp.maximum(jnp.sqrt(_dot(kk * kk, hsum)), 1e-12)
    k = k * (1.0 + (a - 1.0) * k_a)

    ti = lax.broadcasted_iota(jnp.int32, (C, C), 0)
    si = lax.broadcasted_iota(jnp.int32, (C, C), 1)
    tri = jnp.where(ti >= si, 1.0, 0.0).astype(BF16)
    log_w_hi = log_w.astype(BF16)
    cum = (jnp.dot(tri, log_w_hi, preferred_element_type=F32)
           + _dot(tri, log_w - log_w_hi.astype(F32)))
    g_in = jnp.exp(cum)
    g_inv = jnp.exp(-cum)
    g_end = g_in[C - 1:C, :]
    kt = kk * jnp.exp(cum - log_w)
    bt = kk * a * g_inv
    kq = k * g_inv
    rt = r * g_in

    masks = _head_masks(C, W)
    t3 = lax.broadcasted_iota(jnp.int32, (H, C, C), 1)
    s3 = lax.broadcasted_iota(jnp.int32, (H, C, C), 2)
    a_ab = jnp.where(t3 > s3, _dot_nt(_stack_heads(kt, masks), bt).reshape(H, C, C), 0.0)
    bt_rows = _stack_heads(bt, masks)
    kq_rows = _stack_heads(kq, masks)
    t1 = lax.broadcasted_iota(jnp.int32, (C, H * C), 0)
    s1 = lax.broadcasted_iota(jnp.int32, (C, H * C), 1) % C
    a_ak = jnp.where(t1 > s1, _dot_nt(kt, kq_rows), 0.0)
    t2 = lax.broadcasted_iota(jnp.int32, (C, 2 * H * C), 0)
    s2 = lax.broadcasted_iota(jnp.int32, (C, 2 * H * C), 1) % C
    p_r = jnp.where(t2 >= s2, _dot_nt(rt, jnp.concatenate([bt_rows, kq_rows], axis=0)), 0.0)
    return a_ab, (a_ak, p_r, kt, rt, bt, kq, g_end, r, k, v, g)


def _rwkv_finish(vals, t_inv, s0, prm):
    (a_ak, p_r, kt, rt, bt, kq, g_end, r, k, v, g) = vals
    (mu, wup, w0, aup, a0, gup, k_k, k_a, r_k, lnw, lnb, hsum) = prm
    C = RWKV_CHUNK
    W = WIDTH
    H = HEADS
    masks = _head_masks(C, W)
    ks = _dot_nt(kt, s0)
    rs = _dot_nt(rt, s0)
    v_rows = _stack_heads(v, masks)
    av = _dot(a_ak, v_rows)
    u = _select_heads(_dot(t_inv.reshape(H * C, C), -(ks + av)), masks, C)
    y = rs + _dot(p_r, jnp.concatenate([_stack_heads(u, masks), v_rows], axis=0))
    upd = _dot_tn(jnp.concatenate([u, v], axis=0),
                  jnp.concatenate([bt * g_end, kq * g_end], axis=0))
    hi = lax.broadcasted_iota(jnp.int32, (W, W), 0) // HEAD_DIM
    hj = lax.broadcasted_iota(jnp.int32, (W, W), 1) // HEAD_DIM
    s_new = s0 * g_end + jnp.where(hi == hj, upd, 0.0)

    inv_d = 1.0 / HEAD_DIM
    mean = _dot(y, hsum) * inv_d
    yc = y - mean
    var = _dot(yc * yc, hsum) * inv_d
    yn = yc * lax.rsqrt(var + RWKV_LNX_EPS) * lnw + lnb
    bonus = _dot(r * k * r_k, hsum) * v
    return (yn + bonus) * g, s_new


def _rwkv_kernel(p_ref, mu_ref, wup_ref, w0_ref, aup_ref, a0_ref, gup_ref, kk_ref, ka_ref,
                 rk_ref, lnw_ref, lnb_ref, hsum_ref, o_ref, carry_ref, s_ref):
    @pl.when(pl.program_id(1) == 0)
    def _():
        carry_ref[...] = jnp.zeros_like(carry_ref)
        s_ref[...] = jnp.zeros_like(s_ref)

    prm = tuple(ref[...] for ref in (mu_ref, wup_ref, w0_ref, aup_ref, a0_ref, gup_ref, kk_ref,
                                     ka_ref, rk_ref, lnw_ref, lnb_ref, hsum_ref))
    G = p_ref.shape[0]
    H = HEADS
    a_abs, vals = [], []
    for i in range(G):
        p = p_ref[i]
        a_ab, val = _rwkv_prepare(p, carry_ref[i], prm)
        carry_ref[i] = p[RWKV_CHUNK - 1:RWKV_CHUNK, :]
        a_abs.append(a_ab)
        vals.append(val)
    t_inv = _unit_lower_inverse(jnp.concatenate(a_abs, axis=0))
    for i in range(G):
        out, s_new = _rwkv_finish(vals[i], t_inv[i * H:(i + 1) * H], s_ref[i], prm)
        s_ref[i] = s_new
        o_ref[i] = out


def _rwkv(p, mu, wup, w0, aup, a0, gup, k_k, k_a, r_k, lnx_w, lnx_b):
    B, T, _ = p.shape
    C = RWKV_CHUNK
    W = WIDTH
    G = RWKV_ROWS if B % RWKV_ROWS == 0 else 1
    lora_in = W_LORA + A_LORA
    wup_pad = jnp.zeros((lora_in, W), F32).at[:W_LORA].set(wup).astype(BF16)
    aup_pad = jnp.zeros((lora_in, W), F32).at[W_LORA:].set(aup).astype(BF16)
    head = np.arange(W) // HEAD_DIM
    hsum = jnp.asarray((head[:, None] == head[None, :]).astype(np.float32)).astype(BF16)
    vec = lambda a: a.reshape(1, -1)
    const = lambda b, t: (0, 0)
    params = [vec(mu), wup_pad, vec(w0), aup_pad, vec(a0), gup.astype(BF16), vec(k_k), vec(k_a),
              vec(r_k), vec(lnx_w), vec(lnx_b), hsum]
    return pl.pallas_call(
        _rwkv_kernel,
        grid=(B // G, T // C),
        in_specs=[pl.BlockSpec((G, C, RWKV_COLS), lambda b, t: (b, t, 0))]
        + [pl.BlockSpec(a.shape, const) for a in params],
        out_specs=pl.BlockSpec((G, C, W), lambda b, t: (b, t, 0)),
        out_shape=jax.ShapeDtypeStruct((B, T, W), F32),
        scratch_shapes=[pltpu.VMEM((G, 1, RWKV_COLS), F32), pltpu.VMEM((G, W, W), F32)],
        compiler_params=_cparams(("arbitrary", "arbitrary")),
        name="rwkv7",
    )(p, *params)


def _retention_kernel(p_ref, cos_ref, sin_ref, dec_ref, xi_ref, zeta_ref, cd_ref, hsum_ref,
                      o_ref, s_ref):
    C = RET_CHUNK
    W = WIDTH

    @pl.when(pl.program_id(1) == 0)
    def _():
        s_ref[...] = jnp.zeros_like(s_ref)

    cos = cos_ref[...]
    sin = sin_ref[...]
    masks = _head_masks(C, W)
    hi = lax.broadcasted_iota(jnp.int32, (W, W), 0) // HEAD_DIM
    hj = lax.broadcasted_iota(jnp.int32, (W, W), 1) // HEAD_DIM
    for i in range(p_ref.shape[0]):
        p = p_ref[i]
        q = p[:, 0:W] * cos + p[:, 4 * W:5 * W] * sin
        k = (p[:, W:2 * W] * cos + p[:, 5 * W:6 * W] * sin) * (HEAD_DIM ** -0.5)
        v = p[:, 2 * W:3 * W]
        gate = p[:, 3 * W:4 * W]
        scores = _dot_nt(q, _stack_heads(k, masks)) * dec_ref[...]
        y = _dot(scores, _stack_heads(v, masks))
        s0 = s_ref[i]
        y = y + _dot(q * xi_ref[...], s0)
        kv = _dot_tn(k * zeta_ref[...], v)
        s_ref[i] = s0 * cd_ref[...] + jnp.where(hi == hj, kv, 0.0)
        ms = _dot(y * y, hsum_ref[...]) * (1.0 / HEAD_DIM)
        o_ref[i] = _silu(gate) * (y * lax.rsqrt(ms + RET_NORM_EPS))


def _retention_tables(T):
    C, H, d = RET_CHUNK, HEADS, HEAD_DIM
    pos = jnp.arange(T, dtype=F32)
    inv_freq = ROPE_BASE ** (-jnp.arange(0, d, 2, dtype=F32) / d)
    ang = pos[:, None] * inv_freq[None, :]
    cos = jnp.cos(ang)
    sin = jnp.sin(ang)
    cos_full = jnp.tile(jnp.concatenate([cos, cos], -1), (1, H))
    sin_full = jnp.tile(jnp.concatenate([-sin, sin], -1), (1, H))
    log_gamma = jnp.log1p(-(2.0 ** (-5.0 - jnp.arange(H, dtype=F32))))
    idx = jnp.arange(C, dtype=F32)
    diff = idx[:, None] - idx[None, :]
    dec = jnp.where(diff >= 0, jnp.exp(log_gamma[:, None, None] * jnp.maximum(diff, 0.0)), 0.0)
    xi = jnp.exp(log_gamma[:, None] * (idx + 1.0))
    zeta = jnp.exp(log_gamma[:, None] * (C - 1.0 - idx))
    cd = jnp.exp(log_gamma * C)
    per_lane = lambda a: jnp.repeat(a.T, d, axis=1)
    return (cos_full, sin_full, dec.transpose(1, 0, 2).reshape(C, H * C), per_lane(xi), per_lane(zeta),
            jnp.repeat(cd, d).reshape(1, H * d))


def _retention(p):
    B, T, cols = p.shape
    C = RET_CHUNK
    W = WIDTH
    cos, sin, dec, xi, zeta, cd = _retention_tables(T)
    head = np.arange(W) // HEAD_DIM
    hsum = jnp.asarray((head[:, None] == head[None, :]).astype(np.float32)).astype(BF16)
    const = lambda b, t: (0, 0)
    G = RET_ROWS if B % RET_ROWS == 0 else 1
    return pl.pallas_call(
        _retention_kernel,
        grid=(B // G, T // C),
        in_specs=[
            pl.BlockSpec((G, C, cols), lambda b, t: (b, t, 0)),
            pl.BlockSpec((C, W), lambda b, t: (t, 0)),
            pl.BlockSpec((C, W), lambda b, t: (t, 0)),
            pl.BlockSpec(dec.shape, const),
            pl.BlockSpec(xi.shape, const),
            pl.BlockSpec(zeta.shape, const),
            pl.BlockSpec(cd.shape, const),
            pl.BlockSpec(hsum.shape, const),
        ],
        out_specs=pl.BlockSpec((G, C, W), lambda b, t: (b, t, 0)),
        out_shape=jax.ShapeDtypeStruct((B, T, W), F32),
        scratch_shapes=[pltpu.VMEM((G, W, W), F32)],
        compiler_params=_cparams(("arbitrary", "arbitrary")),
        name="retention",
    )(p, cos, sin, dec, xi, zeta, cd, hsum)


def _pool_kernel(u_ref, w_ref, scale_ref, o_ref):
    u = u_ref[0]
    T = u.shape[0]
    row = lax.broadcasted_iota(jnp.int32, (T, 1), 0)

    def lag(x, k):
        return jnp.where(row >= k, pltpu.roll(x, k, 0), 0.0)

    s2 = u + lag(u, 1)
    s4 = s2 + lag(s2, 2)
    s8 = s4 + lag(s4, 4)
    s16 = s8 + lag(s8, 8)
    grp = lax.broadcasted_iota(jnp.int32, (1, POOL_WIDTH), 1) // POOL_GROUP_DIM
    s = jnp.where(grp == 0, s2, jnp.where(grp == 1, s4, jnp.where(grp == 2, s8, s16)))
    win = jnp.where(grp == 0, POOL_WINDOWS[0],
                    jnp.where(grp == 1, POOL_WINDOWS[1],
                              jnp.where(grp == 2, POOL_WINDOWS[2], POOL_WINDOWS[3])))
    count = jnp.minimum(row + 1, win).astype(F32)
    pooled = s / count - u
    o_ref[0] = _dot(pooled, w_ref[...]) * scale_ref[...]


def _pool(u, pool_w, pool_scale):
    B, T, Wp = u.shape
    G, d = POOL_GROUPS, POOL_GROUP_DIM
    wbd = jnp.zeros((Wp, Wp), F32)
    for gi in range(G):
        wbd = wbd.at[gi * d:(gi + 1) * d, gi * d:(gi + 1) * d].set(pool_w[gi])
    return pl.pallas_call(
        _pool_kernel,
        grid=(B,),
        in_specs=[
            pl.BlockSpec((1, T, Wp), lambda b: (b, 0, 0)),
            pl.BlockSpec((Wp, Wp), lambda b: (0, 0)),
            pl.BlockSpec((1, Wp), lambda b: (0, 0)),
        ],
        out_specs=pl.BlockSpec((1, T, Wp), lambda b: (b, 0, 0)),
        out_shape=jax.ShapeDtypeStruct((B, T, Wp), F32),
        compiler_params=_cparams(("arbitrary",)),
        name="pool",
    )(u, wbd.astype(BF16), pool_scale.reshape(1, Wp))


def _route(logits_t, bias_col):
    E, tm = logits_t.shape
    per_group = E // N_GROUPS
    neg_inf = -jnp.inf
    scores = _sigmoid(logits_t)
    choice = scores + bias_col
    c3 = choice.reshape(N_GROUPS, per_group, tm)
    sub = lax.broadcasted_iota(jnp.int32, c3.shape, 1)
    m1 = jnp.max(c3, axis=1, keepdims=True)
    first = jnp.min(jnp.where(c3 == m1, sub, per_group), axis=1, keepdims=True)
    m2 = jnp.max(jnp.where(sub == first, neg_inf, c3), axis=1, keepdims=True)
    gs = m1 + m2
    gidx = lax.broadcasted_iota(jnp.int32, gs.shape, 0)
    grank = jnp.zeros(gs.shape, jnp.int32)
    for j in range(N_GROUPS):
        other = gs[j:j + 1]
        ahead = jnp.where(other > gs, 1, jnp.where((other == gs) & (gidx > j), 1, 0))
        grank = grank + ahead
    gmask = jnp.broadcast_to(grank < TOPK_GROUPS, c3.shape)
    masked = jnp.where(gmask, c3, neg_inf).reshape(E, tm)
    eidx = lax.broadcasted_iota(jnp.int32, (E, tm), 0)
    top = jnp.zeros((E, tm), F32)
    for _ in range(TOP_K):
        best = jnp.max(masked, axis=0, keepdims=True)
        first = jnp.min(jnp.where(masked == best, eidx, E), axis=0, keepdims=True)
        hit = eidx == first
        top = jnp.where(hit, scores, top)
        masked = jnp.where(hit, neg_inf, masked)
    return top / jnp.sum(top, axis=0, keepdims=True) * ROUTED_SCALE


def _outproj_kernel(yr_ref, yt_ref, yp_ref, x_ref, w1_ref, w2_ref, w3_ref, g1_ref, ng_ref,
                    sc_ref, sh_ref, wr_ref, rb_ref, x1_ref, h_ref, wc_ref):
    mixed = (_dot(yr_ref[...], w1_ref[...]) + _dot(yt_ref[...], w2_ref[...])
             + _dot(yp_ref[...], w3_ref[...]))
    x1 = x_ref[...] + g1_ref[0] * mixed
    x1_ref[...] = x1
    h = _modulated_norm(x1, ng_ref[...], sc_ref[0], sh_ref[0])
    h_ref[...] = h.astype(BF16)
    logits_t = lax.dot_general(wr_ref[...], h, (((1,), (1,)), ((), ())),
                               preferred_element_type=F32, precision=HIGHEST)
    wc_t = _route(logits_t, rb_ref[...])
    pad = jnp.zeros((LANES - N_EXPERTS, wc_t.shape[1]), F32)
    wc_ref[...] = jnp.concatenate([wc_t, pad], axis=0).T


def _outproj(yr, yt, yp, x2, w_out, gate1, ng, scale2, shift2, w_router, router_bias, T):
    N, D = x2.shape
    B = N // T
    tm = min(512, T)
    per_b = T // tm
    W = WIDTH
    w1 = w_out[:W].astype(BF16)
    w2 = w_out[W:2 * W].astype(BF16)
    w3 = w_out[2 * W:].astype(BF16)
    row = lambda i: (i, 0)
    const = lambda i: (0, 0)
    bvec = lambda i: (i // per_b, 0, 0)
    return pl.pallas_call(
        _outproj_kernel,
        grid=(N // tm,),
        in_specs=[
            pl.BlockSpec((tm, W), row),
            pl.BlockSpec((tm, W), row),
            pl.BlockSpec((tm, POOL_WIDTH), row),
            pl.BlockSpec((tm, D), row),
            pl.BlockSpec(w1.shape, const),
            pl.BlockSpec(w2.shape, const),
            pl.BlockSpec(w3.shape, const),
            pl.BlockSpec((1, 1, D), bvec),
            pl.BlockSpec((1, D), const),
            pl.BlockSpec((1, 1, D), bvec),
            pl.BlockSpec((1, 1, D), bvec),
            pl.BlockSpec((N_EXPERTS, D), const),
            pl.BlockSpec((N_EXPERTS, 1), const),
        ],
        out_specs=[
            pl.BlockSpec((tm, D), row),
            pl.BlockSpec((tm, D), row),
            pl.BlockSpec((tm, LANES), row),
        ],
        out_shape=[
            jax.ShapeDtypeStruct((N, D), F32),
            jax.ShapeDtypeStruct((N, D), BF16),
            jax.ShapeDtypeStruct((N, LANES), F32),
        ],
        compiler_params=_cparams(("arbitrary",)),
        name="outproj_router",
    )(yr, yt, yp, x2, w1, w2, w3, gate1.reshape(B, 1, D), ng.reshape(1, D),
      scale2.reshape(B, 1, D), shift2.reshape(B, 1, D), w_router.T,
      router_bias.reshape(N_EXPERTS, 1))


def _slot_positions(wc):
    tm = wc.shape[0]
    sel = jnp.where(wc > 0.0, 1.0, 0.0).astype(BF16)
    ti = lax.broadcasted_iota(jnp.int32, (tm, tm), 0)
    si = lax.broadcasted_iota(jnp.int32, (tm, tm), 1)
    earlier = jnp.where(ti > si, 1.0, 0.0).astype(BF16)
    return sel, jnp.dot(earlier, sel, preferred_element_type=F32)


def _dispatch_kernel(h_ref, wc_ref, rept_ref, xs_ref, over_ref):
    nb, ts, D = h_ref.shape
    tm = nb * ts
    wc = wc_ref[...].reshape(tm, LANES)
    sel, pos = _slot_positions(wc)
    over_ref[...] = jnp.where(pos >= MOE_CAP, wc, 0.0).reshape(nb, ts, LANES)
    rept = rept_ref[...]
    code = jnp.where(wc > 0.0, pos, -1.0)
    code_rows = _dot_nt(rept, code)
    slot = (lax.broadcasted_iota(jnp.int32, (rept.shape[0], 1), 0) % MOE_CAP).astype(F32)
    onehot = jnp.where(code_rows == slot, 1.0, 0.0).astype(BF16)
    xs_ref[...] = jnp.dot(onehot, h_ref[...].reshape(tm, D),
                          preferred_element_type=F32).astype(BF16)


def _dispatch(h, wc, B, T):
    N, D = h.shape
    ts = MOE_TILE_SEQ
    nt = T // ts
    L = N_EXPERTS * MOE_CAP
    rows = np.arange(L) // MOE_CAP
    rept = jnp.asarray((rows[:, None] == np.arange(LANES)[None, :]).astype(np.float32)).astype(BF16)
    tile = lambda i: (0, i, 0, 0)
    xs, over = pl.pallas_call(
        _dispatch_kernel,
        grid=(nt,),
        in_specs=[
            pl.BlockSpec((B, None, ts, D), tile),
            pl.BlockSpec((B, None, ts, LANES), tile),
            pl.BlockSpec((L, LANES), lambda i: (0, 0)),
        ],
        out_specs=[
            pl.BlockSpec((None, L, D), lambda i: (i, 0, 0)),
            pl.BlockSpec((B, None, ts, LANES), tile),
        ],
        out_shape=[
            jax.ShapeDtypeStruct((nt, L, D), BF16),
            jax.ShapeDtypeStruct((B, nt, ts, LANES), F32),
        ],
        compiler_params=_cparams(("arbitrary",)),
        name="moe_dispatch",
    )(h.reshape(B, nt, ts, D), wc.reshape(B, nt, ts, LANES), rept)
    return xs, over.reshape(N, LANES)


def _experts_kernel(x_ref, wg_ref, wu_ref, wd_ref, o_ref, wg_s, wu_s, wd_s):
    @pl.when(pl.program_id(1) == 0)
    def _():
        wg_s[...] = wg_ref[0].astype(BF16)
        wu_s[...] = wu_ref[0].astype(BF16)
        wd_s[...] = wd_ref[0].astype(BF16)

    tg, cap, D = x_ref.shape
    x = x_ref[...].reshape(tg * cap, D)
    hid = _silu(jnp.dot(x, wg_s[...], preferred_element_type=F32)) * jnp.dot(
        x, wu_s[...], preferred_element_type=F32)
    o_ref[...] = _dot(hid, wd_s[...]).astype(BF16).reshape(tg, cap, D)


def _experts(xs, wg, wu, wd):
    nt, L, D = xs.shape
    E, _, Hd = wg.shape
    cap = L // E
    tg = math.gcd(nt, 16)
    slots = pl.BlockSpec((tg, None, cap, D), lambda e, g: (g, e, 0, 0))
    return pl.pallas_call(
        _experts_kernel,
        grid=(E, nt // tg),
        in_specs=[
            slots,
            pl.BlockSpec((1, D, Hd), lambda e, g: (e, 0, 0)),
            pl.BlockSpec((1, D, Hd), lambda e, g: (e, 0, 0)),
            pl.BlockSpec((1, Hd, D), lambda e, g: (e, 0, 0)),
        ],
        out_specs=slots,
        out_shape=jax.ShapeDtypeStruct((nt, E, cap, D), BF16),
        scratch_shapes=[pltpu.VMEM((D, Hd), BF16), pltpu.VMEM((D, Hd), BF16),
                        pltpu.VMEM((Hd, D), BF16)],
        compiler_params=_cparams(("arbitrary", "arbitrary")),
        name="moe_experts",
    )(xs.reshape(nt, E, cap, D), wg, wu, wd).reshape(nt, L, D)


def _combine_kernel(y_ref, wc_ref, rep_ref, h_ref, x1_ref, ex_ref, g2_ref, sg_ref, su_ref, sd_ref,
                    fg_ref, o_ref, *, final_norm):
    nb, ts, D = h_ref.shape
    tm = nb * ts
    wc = wc_ref[...].reshape(tm, LANES)
    _, pos = _slot_positions(wc)
    rep = rep_ref[...]
    pos_lanes = _dot(pos, rep)
    w_lanes = _dot(wc, rep)
    slot = (lax.broadcasted_iota(jnp.int32, (1, rep.shape[1]), 1) % MOE_CAP).astype(F32)
    weighted = jnp.where(pos_lanes == slot, w_lanes, 0.0).astype(BF16)
    routed = jnp.dot(weighted, y_ref[...], preferred_element_type=F32)
    h = h_ref[...].reshape(tm, D)
    hid = _silu(jnp.dot(h, sg_ref[...], preferred_element_type=F32)) * jnp.dot(
        h, su_ref[...], preferred_element_type=F32)
    y = routed + ex_ref[...].reshape(tm, D) + _dot(hid, sd_ref[...])
    gate = jnp.broadcast_to(g2_ref[...], (nb, ts, D)).reshape(tm, D)
    xo = x1_ref[...].reshape(tm, D) + gate * y
    if final_norm:
        ms = jnp.mean(xo * xo, axis=-1, keepdims=True)
        xo = xo * lax.rsqrt(ms + NORM_EPS) * fg_ref[...]
    o_ref[...] = xo.reshape(nb, ts, D)


def _combine(ys, wc, h, x1, extra, gate2, sg, su, sd, final_g, B, T, final_norm):
    N, D = x1.shape
    ts = MOE_TILE_SEQ
    nt = T // ts
    L = N_EXPERTS * MOE_CAP
    cols = np.arange(L) // MOE_CAP
    rep = jnp.asarray((np.arange(LANES)[:, None] == cols[None, :]).astype(np.float32)).astype(BF16)
    tile = lambda i: (0, i, 0, 0)
    const = lambda i: (0, 0)
    tok = lambda a: a.reshape(B, nt, ts, a.shape[-1])
    out = pl.pallas_call(
        functools.partial(_combine_kernel, final_norm=final_norm),
        grid=(nt,),
        in_specs=[
            pl.BlockSpec((None, L, D), lambda i: (i, 0, 0)),
            pl.BlockSpec((B, None, ts, LANES), tile),
            pl.BlockSpec((LANES, L), const),
            pl.BlockSpec((B, None, ts, D), tile),
            pl.BlockSpec((B, None, ts, D), tile),
            pl.BlockSpec((B, None, ts, D), tile),
            pl.BlockSpec((B, 1, D), lambda i: (0, 0, 0)),
            pl.BlockSpec(sg.shape, const),
            pl.BlockSpec(su.shape, const),
            pl.BlockSpec(sd.shape, const),
            pl.BlockSpec((1, D), const),
        ],
        out_specs=pl.BlockSpec((B, None, ts, D), tile),
        out_shape=jax.ShapeDtypeStruct((B, nt, ts, D), F32),
        compiler_params=_cparams(("arbitrary",)),
        name="moe_combine",
    )(ys, tok(wc), rep, tok(h), tok(x1), tok(extra), gate2.reshape(B, 1, D), sg, su, sd,
      final_g.reshape(1, D))
    return out.reshape(N, D)


def _overflow_kernel(h_ref, wc_ref, wg_ref, wu_ref, wd_ref, o_ref):
    e = pl.program_id(1)

    @pl.when(e == 0)
    def _():
        o_ref[...] = jnp.zeros_like(o_ref)

    h = h_ref[...]
    lane = lax.broadcasted_iota(jnp.int32, wc_ref.shape, 1)
    w = jnp.sum(jnp.where(lane == e, wc_ref[...], 0.0), axis=1, keepdims=True)
    hid = _silu(jnp.dot(h, wg_ref[0], preferred_element_type=F32)) * jnp.dot(
        h, wu_ref[0], preferred_element_type=F32)
    o_ref[...] += _dot(hid * w, wd_ref[0])


def _overflow(h, wc_over, wg, wu, wd):
    N, D = h.shape
    tm = math.gcd(N, 1024)
    E, _, Hd = wg.shape
    row = lambda i, e: (i, 0)
    return pl.pallas_call(
        _overflow_kernel,
        grid=(N // tm, E),
        in_specs=[
            pl.BlockSpec((tm, D), row),
            pl.BlockSpec((tm, LANES), row),
            pl.BlockSpec((1, D, Hd), lambda i, e: (e, 0, 0)),
            pl.BlockSpec((1, D, Hd), lambda i, e: (e, 0, 0)),
            pl.BlockSpec((1, Hd, D), lambda i, e: (e, 0, 0)),
        ],
        out_specs=pl.BlockSpec((tm, D), row),
        out_shape=jax.ShapeDtypeStruct((N, D), F32),
        compiler_params=_cparams(("arbitrary", "arbitrary")),
        name="moe_overflow",
    )(h, wc_over, wg, wu, wd)


def _moe(h, wc, wg, wu, wd, sg, su, sd, x1, gate2, final_g, B, T, final_norm):
    xs, wc_over = _dispatch(h, wc, B, T)
    ys = _experts(xs, wg, wu, wd)
    extra = lax.cond(jnp.any(wc_over != 0.0),
                     lambda: _overflow(h, wc_over, wg.astype(BF16), wu.astype(BF16),
                                       wd.astype(BF16)),
                     lambda: jnp.zeros(x1.shape, F32))
    return _combine(ys, wc, h, x1, extra, gate2, sg, su, sd, final_g, B, T, final_norm)


def _split_w_in(w_in):
    W = WIDTH
    half = HEAD_DIM // 2
    j = np.arange(W)
    swap = (j // HEAD_DIM) * HEAD_DIM + (j % HEAD_DIM + half) % HEAD_DIM
    w_rwkv = w_in[:, :RWKV_COLS]
    w_ret = w_in[:, RWKV_COLS:RWKV_COLS + RET_COLS]
    w_pool = w_in[:, RWKV_COLS + RET_COLS:]
    q_sw = w_ret[:, 0:W][:, swap]
    k_sw = w_ret[:, W:2 * W][:, swap]
    w_ret_ext = jnp.concatenate([w_ret, q_sw, k_sw], axis=1)
    return w_rwkv.astype(BF16), w_ret_ext.astype(BF16), w_pool.astype(BF16)


def kernel(x, c, norm1_g, norm2_g, w_ada, b_ada, w_in, w_out, rwkv_mu, rwkv_w_up, rwkv_w0,
           rwkv_a_up, rwkv_a0, rwkv_g_up, rwkv_k_k, rwkv_k_a, rwkv_r_k, rwkv_lnx_w, rwkv_lnx_b,
           pool_w, pool_scale, w_router, router_bias, we_gate, we_up, we_down,
           ws_gate, ws_up, ws_down, final_g):
    B, T, D = x.shape
    L = w_in.shape[0]
    N = B * T
    mod = _adaln(c, w_ada, b_ada)
    x2 = x.reshape(N, D)
    for l in range(L):
        shift1, scale1, gate1, shift2, scale2, gate2 = jnp.split(mod[l], 6, axis=-1)
        w1, w2, w3 = _split_w_in(w_in[l])
        p_rwkv, p_ret, p_pool = _inproj(x2, norm1_g[l], scale1, shift1, w1, w2, w3, T)
        y_rwkv = _rwkv(p_rwkv.reshape(B, T, -1), rwkv_mu[l], rwkv_w_up[l], rwkv_w0[l],
                       rwkv_a_up[l], rwkv_a0[l], rwkv_g_up[l], rwkv_k_k[l], rwkv_k_a[l],
                       rwkv_r_k[l], rwkv_lnx_w[l], rwkv_lnx_b[l])
        y_ret = _retention(p_ret.reshape(B, T, -1))
        y_pool = _pool(p_pool.reshape(B, T, -1), pool_w[l], pool_scale[l])
        x1, h, wc = _outproj(y_rwkv.reshape(N, -1), y_ret.reshape(N, -1), y_pool.reshape(N, -1),
                             x2, w_out[l], gate1, norm2_g[l], scale2, shift2,
                             w_router[l], router_bias[l], T)
        x2 = _moe(h, wc, we_gate[l], we_up[l], we_down[l],
                  ws_gate[l].astype(BF16), ws_up[l].astype(BF16),
                  ws_down[l].astype(BF16), x1, gate2, final_g, B, T,
                  final_norm=(l == L - 1))
    return x2.reshape(B, T, D)
```

```python
import functools
import math

import numpy as np
import jax
import jax.numpy as jnp
from jax import lax
from jax.experimental import pallas as pl
from jax.experimental.pallas import tpu as pltpu

F32 = jnp.float32
BF16 = jnp.bfloat16
HIGHEST = lax.Precision.HIGHEST

D_MODEL = 1024
HEADS = 6
HEAD_DIM = 64
WIDTH = HEADS * HEAD_DIM
W_LORA = 64
A_LORA = 64
G_LORA = 128
RWKV_COLS = 3 * WIDTH + W_LORA + A_LORA + G_LORA
RET_COLS = 4 * WIDTH
RET_EXT_COLS = 6 * WIDTH
RET_CHUNK = 128
RWKV_CHUNK = 128
RWKV_SUB = 16
RET_ROWS = 4
RWKV_ROWS = 4
POOL_GROUPS = 4
POOL_GROUP_DIM = 64
POOL_WIDTH = POOL_GROUPS * POOL_GROUP_DIM
POOL_WINDOWS = (2, 4, 8, 16)
N_EXPERTS = 64
TOP_K = 8
N_GROUPS = 8
TOPK_GROUPS = 4
EXPERT_HIDDEN = 256
ROUTED_SCALE = 2.5
NORM_EPS = 1e-6
RWKV_LNX_EPS = 64e-5
RET_NORM_EPS = 1e-6
ROPE_BASE = 10000.0
LANES = 128
MOE_TILE_SEQ = 16
MOE_CAP = 80
MOE_USED_STEPS = (48, 64)
VMEM_LIMIT = 48 * 1024 * 1024


def _cparams(sem):
    return pltpu.CompilerParams(dimension_semantics=sem, vmem_limit_bytes=VMEM_LIMIT)


def _dot(a, b):
    return jnp.dot(a.astype(BF16), b.astype(BF16), preferred_element_type=F32)


def _dot_nt(a, b):
    return lax.dot_general(a.astype(BF16), b.astype(BF16), (((1,), (1,)), ((), ())),
                           preferred_element_type=F32)


def _dot_tn(a, b):
    return lax.dot_general(a.astype(BF16), b.astype(BF16), (((0,), (0,)), ((), ())),
                           preferred_element_type=F32)


def _dot_f32(a, b):
    return jnp.dot(a, b, preferred_element_type=F32, precision=HIGHEST)


def _sigmoid(x):
    return 1.0 / (1.0 + jnp.exp(-x))


def _silu(x):
    return x * _sigmoid(x)


def _head_masks(rows, width):
    lane = lax.broadcasted_iota(jnp.int32, (rows, width), 1)
    return [lane // HEAD_DIM == h for h in range(width // HEAD_DIM)]


def _stack_heads(x, masks):
    return jnp.concatenate([jnp.where(m, x, 0.0) for m in masks], axis=0).astype(BF16)


def _select_heads(stacked, masks, c):
    out = stacked[0:c]
    for h in range(1, len(masks)):
        out = jnp.where(masks[h], stacked[h * c:(h + 1) * c], out)
    return out


def _adaln_kernel(c_ref, w_ref, b_ref, o_ref):
    o_ref[0] = _dot_f32(_silu(c_ref[...]), w_ref[0]) + b_ref[0]


def _adaln(c, w_ada, b_ada):
    L, D, M = w_ada.shape
    B = c.shape[0]
    tn = 1536
    return pl.pallas_call(
        _adaln_kernel,
        grid=(L, M // tn),
        in_specs=[
            pl.BlockSpec((B, D), lambda l, j: (0, 0)),
            pl.BlockSpec((1, D, tn), lambda l, j: (l, 0, j)),
            pl.BlockSpec((1, 1, tn), lambda l, j: (l, 0, j)),
        ],
        out_specs=pl.BlockSpec((1, B, tn), lambda l, j: (l, 0, j)),
        out_shape=jax.ShapeDtypeStruct((L, B, M), F32),
        compiler_params=_cparams(("arbitrary", "arbitrary")),
        name="adaln",
    )(c, w_ada, b_ada.reshape(L, 1, M))


def _modulated_norm(x, g, scale, shift):
    ms = jnp.mean(x * x, axis=-1, keepdims=True)
    return x * lax.rsqrt(ms + NORM_EPS) * g * (1.0 + scale) + shift


def _inproj_kernel(x_ref, g_ref, sc_ref, sh_ref, w1_ref, w2_ref, w3_ref, o1_ref, o2_ref, o3_ref):
    h = _modulated_norm(x_ref[...], g_ref[...], sc_ref[0], sh_ref[0]).astype(BF16)
    o1_ref[...] = jnp.dot(h, w1_ref[...], preferred_element_type=F32)
    o2_ref[...] = jnp.dot(h, w2_ref[...], preferred_element_type=F32)
    o3_ref[...] = jnp.dot(h, w3_ref[...], preferred_element_type=F32)


def _inproj(x2, g, scale, shift, w1, w2, w3, T):
    N, D = x2.shape
    B = N // T
    tm = min(256, T)
    per_b = T // tm
    row = lambda i: (i, 0)
    const = lambda i: (0, 0)
    bvec = lambda i: (i // per_b, 0, 0)
    return pl.pallas_call(
        _inproj_kernel,
        grid=(N // tm,),
        in_specs=[
            pl.BlockSpec((tm, D), row),
            pl.BlockSpec((1, D), const),
            pl.BlockSpec((1, 1, D), bvec),
            pl.BlockSpec((1, 1, D), bvec),
            pl.BlockSpec(w1.shape, const),
            pl.BlockSpec(w2.shape, const),
            pl.BlockSpec(w3.shape, const),
        ],
        out_specs=[
            pl.BlockSpec((tm, w1.shape[1]), row),
            pl.BlockSpec((tm, w2.shape[1]), row),
            pl.BlockSpec((tm, w3.shape[1]), row),
        ],
        out_shape=[
            jax.ShapeDtypeStruct((N, w1.shape[1]), F32),
            jax.ShapeDtypeStruct((N, w2.shape[1]), F32),
            jax.ShapeDtypeStruct((N, w3.shape[1]), F32),
        ],
        compiler_params=_cparams(("arbitrary",)),
        name="inproj",
    )(x2, g.reshape(1, D), scale.reshape(B, 1, D), shift.reshape(B, 1, D), w1, w2, w3)


def _unit_lower_inverse(a3):
    H, C, _ = a3.shape
    ri = lax.broadcasted_iota(jnp.int32, (H, C, C), 1)
    ci = lax.broadcasted_iota(jnp.int32, (H, C, C), 2)
    eye = (ri == ci).astype(F32)
    same = (ri // RWKV_SUB) == (ci // RWKV_SUB)
    dm = jnp.where(same, a3, 0.0)
    off = jnp.where(same, 0.0, a3)

    def bmm(x, y):
        return jnp.einsum('hij,hjk->hik', x.astype(BF16), y.astype(BF16),
                          preferred_element_type=F32)

    d2 = bmm(dm, dm)
    d4 = bmm(d2, d2)
    d8 = bmm(d4, d4)
    x = eye - dm
    x = x + bmm(x, d2)
    x = x + bmm(x, d4)
    x = x + bmm(x, d8)
    n = bmm(x, off)
    y = eye - n
    power = bmm(n, n)
    order = 2
    while order < C // RWKV_SUB:
        y = y + bmm(y, power)
        order *= 2
        if order < C // RWKV_SUB:
            power = bmm(power, power)
    return bmm(y, x)


def _rwkv_prepare(p, carry, prm):
    (mu, wup, w0, aup, a0, gup, k_k, k_a, r_k, lnw, lnb, hsum) = prm
    C = RWKV_CHUNK
    W = WIDTH
    H = HEADS
    row = lax.broadcasted_iota(jnp.int32, (C, 1), 0)
    prev = jnp.where(row == 0, carry, pltpu.roll(p, 1, 0))
    xs = p + (prev - p) * mu

    r = xs[:, 0:W]
    k = xs[:, W:2 * W]
    v = xs[:, 2 * W:3 * W]
    xwa = xs[:, 3 * W:3 * W + W_LORA + A_LORA]
    xg = xs[:, 3 * W + W_LORA + A_LORA:]

    z = w0 + _dot(jnp.tanh(xwa), wup)
    log_w = -math.exp(-0.5) * _sigmoid(z)
    a = _sigmoid(a0 + _dot(xwa, aup))
    g = _dot(_sigmoid(xg), gup)
    kk = k * k_k
    kk = kk / jnp.maximum(jnp.sqrt(_dot(kk * kk, hsum)), 1e-12)
    k = k * (1.0 + (a - 1.0) * k_a)

    ti = lax.broadcasted_iota(jnp.int32, (C, C), 0)
    si = lax.broadcasted_iota(jnp.int32, (C, C), 1)
    tri = jnp.where(ti >= si, 1.0, 0.0).astype(BF16)
    log_w_hi = log_w.astype(BF16)
    cum = (jnp.dot(tri, log_w_hi, preferred_element_type=F32)
           + _dot(tri, log_w - log_w_hi.astype(F32)))
    g_in = jnp.exp(cum)
    g_inv = jnp.exp(-cum)
    g_end = g_in[C - 1:C, :]
    kt = kk * jnp.exp(cum - log_w)
    bt = kk * a * g_inv
    kq = k * g_inv
    rt = r * g_in

    masks = _head_masks(C, W)
    t3 = lax.broadcasted_iota(jnp.int32, (H, C, C), 1)
    s3 = lax.broadcasted_iota(jnp.int32, (H, C, C), 2)
    a_ab = jnp.where(t3 > s3, _dot_nt(_stack_heads(kt, masks), bt).reshape(H, C, C), 0.0)
    bt_rows = _stack_heads(bt, masks)
    kq_rows = _stack_heads(kq, masks)
    t1 = lax.broadcasted_iota(jnp.int32, (C, H * C), 0)
    s1 = lax.broadcasted_iota(jnp.int32, (C, H * C), 1) % C
    a_ak = jnp.where(t1 > s1, _dot_nt(kt, kq_rows), 0.0)
    t2 = lax.broadcasted_iota(jnp.int32, (C, 2 * H * C), 0)
    s2 = lax.broadcasted_iota(jnp.int32, (C, 2 * H * C), 1) % C
    p_r = jnp.where(t2 >= s2, _dot_nt(rt, jnp.concatenate([bt_rows, kq_rows], axis=0)), 0.0)
    return a_ab, (a_ak, p_r, kt, rt, bt, kq, g_end, r, k, v, g)


def _rwkv_finish(vals, t_inv, s0, prm):
    (a_ak, p_r, kt, rt, bt, kq, g_end, r, k, v, g) = vals
    (mu, wup, w0, aup, a0, gup, k_k, k_a, r_k, lnw, lnb, hsum) = prm
    C = RWKV_CHUNK
    W = WIDTH
    H = HEADS
    masks = _head_masks(C, W)
    ks = _dot_nt(kt, s0)
    rs = _dot_nt(rt, s0)
    v_rows = _stack_heads(v, masks)
    av = _dot(a_ak, v_rows)
    u = _select_heads(_dot(t_inv.reshape(H * C, C), -(ks + av)), masks, C)
    y = rs + _dot(p_r, jnp.concatenate([_stack_heads(u, masks), v_rows], axis=0))
    upd = _dot_tn(jnp.concatenate([u, v], axis=0),
                  jnp.concatenate([bt * g_end, kq * g_end], axis=0))
    hi = lax.broadcasted_iota(jnp.int32, (W, W), 0) // HEAD_DIM
    hj = lax.broadcasted_iota(jnp.int32, (W, W), 1) // HEAD_DIM
    s_new = s0 * g_end + jnp.where(hi == hj, upd, 0.0)

    inv_d = 1.0 / HEAD_DIM
    mean = _dot(y, hsum) * inv_d
    yc = y - mean
    var = _dot(yc * yc, hsum) * inv_d
    yn = yc * lax.rsqrt(var + RWKV_LNX_EPS) * lnw + lnb
    bonus = _dot(r * k * r_k, hsum) * v
    return (yn + bonus) * g, s_new


def _rwkv_kernel(p_ref, mu_ref, wup_ref, w0_ref, aup_ref, a0_ref, gup_ref, kk_ref, ka_ref,
                 rk_ref, lnw_ref, lnb_ref, hsum_ref, o_ref, carry_ref, s_ref):
    @pl.when(pl.program_id(1) == 0)
    def _():
        carry_ref[...] = jnp.zeros_like(carry_ref)
        s_ref[...] = jnp.zeros_like(s_ref)

    prm = tuple(ref[...] for ref in (mu_ref, wup_ref, w0_ref, aup_ref, a0_ref, gup_ref, kk_ref,
                                     ka_ref, rk_ref, lnw_ref, lnb_ref, hsum_ref))
    G = p_ref.shape[0]
    H = HEADS
    a_abs, vals = [], []
    for i in range(G):
        p = p_ref[i]
        a_ab, val = _rwkv_prepare(p, carry_ref[i], prm)
        carry_ref[i] = p[RWKV_CHUNK - 1:RWKV_CHUNK, :]
        a_abs.append(a_ab)
        vals.append(val)
    t_inv = _unit_lower_inverse(jnp.concatenate(a_abs, axis=0))
    for i in range(G):
        out, s_new = _rwkv_finish(vals[i], t_inv[i * H:(i + 1) * H], s_ref[i], prm)
        s_ref[i] = s_new
        o_ref[i] = out


def _rwkv(p, mu, wup, w0, aup, a0, gup, k_k, k_a, r_k, lnx_w, lnx_b):
    B, T, _ = p.shape
    C = RWKV_CHUNK
    W = WIDTH
    G = RWKV_ROWS if B % RWKV_ROWS == 0 else 1
    lora_in = W_LORA + A_LORA
    wup_pad = jnp.zeros((lora_in, W), F32).at[:W_LORA].set(wup).astype(BF16)
    aup_pad = jnp.zeros((lora_in, W), F32).at[W_LORA:].set(aup).astype(BF16)
    head = np.arange(W) // HEAD_DIM
    hsum = jnp.asarray((head[:, None] == head[None, :]).astype(np.float32)).astype(BF16)
    vec = lambda a: a.reshape(1, -1)
    const = lambda b, t: (0, 0)
    params = [vec(mu), wup_pad, vec(w0), aup_pad, vec(a0), gup.astype(BF16), vec(k_k), vec(k_a),
              vec(r_k), vec(lnx_w), vec(lnx_b), hsum]
    return pl.pallas_call(
        _rwkv_kernel,
        grid=(B // G, T // C),
        in_specs=[pl.BlockSpec((G, C, RWKV_COLS), lambda b, t: (b, t, 0))]
        + [pl.BlockSpec(a.shape, const) for a in params],
        out_specs=pl.BlockSpec((G, C, W), lambda b, t: (b, t, 0)),
        out_shape=jax.ShapeDtypeStruct((B, T, W), F32),
        scratch_shapes=[pltpu.VMEM((G, 1, RWKV_COLS), F32), pltpu.VMEM((G, W, W), F32)],
        compiler_params=_cparams(("arbitrary", "arbitrary")),
        name="rwkv7",
    )(p, *params)


def _retention_kernel(p_ref, cos_ref, sin_ref, dec_ref, xi_ref, zeta_ref, cd_ref, hsum_ref,
                      o_ref, s_ref):
    C = RET_CHUNK
    W = WIDTH

    @pl.when(pl.program_id(1) == 0)
    def _():
        s_ref[...] = jnp.zeros_like(s_ref)

    cos = cos_ref[...]
    sin = sin_ref[...]
    masks = _head_masks(C, W)
    hi = lax.broadcasted_iota(jnp.int32, (W, W), 0) // HEAD_DIM
    hj = lax.broadcasted_iota(jnp.int32, (W, W), 1) // HEAD_DIM
    for i in range(p_ref.shape[0]):
        p = p_ref[i]
        q = p[:, 0:W] * cos + p[:, 4 * W:5 * W] * sin
        k = (p[:, W:2 * W] * cos + p[:, 5 * W:6 * W] * sin) * (HEAD_DIM ** -0.5)
        v = p[:, 2 * W:3 * W]
        gate = p[:, 3 * W:4 * W]
        scores = _dot_nt(q, _stack_heads(k, masks)) * dec_ref[...]
        y = _dot(scores, _stack_heads(v, masks))
        s0 = s_ref[i]
        y = y + _dot(q * xi_ref[...], s0)
        kv = _dot_tn(k * zeta_ref[...], v)
        s_ref[i] = s0 * cd_ref[...] + jnp.where(hi == hj, kv, 0.0)
        ms = _dot(y * y, hsum_ref[...]) * (1.0 / HEAD_DIM)
        o_ref[i] = _silu(gate) * (y * lax.rsqrt(ms + RET_NORM_EPS))


def _retention_tables(T):
    C, H, d = RET_CHUNK, HEADS, HEAD_DIM
    pos = jnp.arange(T, dtype=F32)
    inv_freq = ROPE_BASE ** (-jnp.arange(0, d, 2, dtype=F32) / d)
    ang = pos[:, None] * inv_freq[None, :]
    cos = jnp.cos(ang)
    sin = jnp.sin(ang)
    cos_full = jnp.tile(jnp.concatenate([cos, cos], -1), (1, H))
    sin_full = jnp.tile(jnp.concatenate([-sin, sin], -1), (1, H))
    log_gamma = jnp.log1p(-(2.0 ** (-5.0 - jnp.arange(H, dtype=F32))))
    idx = jnp.arange(C, dtype=F32)
    diff = idx[:, None] - idx[None, :]
    dec = jnp.where(diff >= 0, jnp.exp(log_gamma[:, None, None] * jnp.maximum(diff, 0.0)), 0.0)
    xi = jnp.exp(log_gamma[:, None] * (idx + 1.0))
    zeta = jnp.exp(log_gamma[:, None] * (C - 1.0 - idx))
    cd = jnp.exp(log_gamma * C)
    per_lane = lambda a: jnp.repeat(a.T, d, axis=1)
    return (cos_full, sin_full, dec.transpose(1, 0, 2).reshape(C, H * C), per_lane(xi), per_lane(zeta),
            jnp.repeat(cd, d).reshape(1, H * d))


def _retention(p):
    B, T, cols = p.shape
    C = RET_CHUNK
    W = WIDTH
    cos, sin, dec, xi, zeta, cd = _retention_tables(T)
    head = np.arange(W) // HEAD_DIM
    hsum = jnp.asarray((head[:, None] == head[None, :]).astype(np.float32)).astype(BF16)
    const = lambda b, t: (0, 0)
    G = RET_ROWS if B % RET_ROWS == 0 else 1
    return pl.pallas_call(
        _retention_kernel,
        grid=(B // G, T // C),
        in_specs=[
            pl.BlockSpec((G, C, cols), lambda b, t: (b, t, 0)),
            pl.BlockSpec((C, W), lambda b, t: (t, 0)),
            pl.BlockSpec((C, W), lambda b, t: (t, 0)),
            pl.BlockSpec(dec.shape, const),
            pl.BlockSpec(xi.shape, const),
            pl.BlockSpec(zeta.shape, const),
            pl.BlockSpec(cd.shape, const),
            pl.BlockSpec(hsum.shape, const),
        ],
        out_specs=pl.BlockSpec((G, C, W), lambda b, t: (b, t, 0)),
        out_shape=jax.ShapeDtypeStruct((B, T, W), F32),
        scratch_shapes=[pltpu.VMEM((G, W, W), F32)],
        compiler_params=_cparams(("arbitrary", "arbitrary")),
        name="retention",
    )(p, cos, sin, dec, xi, zeta, cd, hsum)


def _pool_kernel(u_ref, w_ref, scale_ref, o_ref):
    u = u_ref[0]
    T = u.shape[0]
    row = lax.broadcasted_iota(jnp.int32, (T, 1), 0)

    def lag(x, k):
        return jnp.where(row >= k, pltpu.roll(x, k, 0), 0.0)

    s2 = u + lag(u, 1)
    s4 = s2 + lag(s2, 2)
    s8 = s4 + lag(s4, 4)
    s16 = s8 + lag(s8, 8)
    grp = lax.broadcasted_iota(jnp.int32, (1, POOL_WIDTH), 1) // POOL_GROUP_DIM
    s = jnp.where(grp == 0, s2, jnp.where(grp == 1, s4, jnp.where(grp == 2, s8, s16)))
    win = jnp.where(grp == 0, POOL_WINDOWS[0],
                    jnp.where(grp == 1, POOL_WINDOWS[1],
                              jnp.where(grp == 2, POOL_WINDOWS[2], POOL_WINDOWS[3])))
    count = jnp.minimum(row + 1, win).astype(F32)
    pooled = s / count - u
    o_ref[0] = _dot(pooled, w_ref[...]) * scale_ref[...]


def _pool(u, pool_w, pool_scale):
    B, T, Wp = u.shape
    G, d = POOL_GROUPS, POOL_GROUP_DIM
    wbd = jnp.zeros((Wp, Wp), F32)
    for gi in range(G):
        wbd = wbd.at[gi * d:(gi + 1) * d, gi * d:(gi + 1) * d].set(pool_w[gi])
    return pl.pallas_call(
        _pool_kernel,
        grid=(B,),
        in_specs=[
            pl.BlockSpec((1, T, Wp), lambda b: (b, 0, 0)),
            pl.BlockSpec((Wp, Wp), lambda b: (0, 0)),
            pl.BlockSpec((1, Wp), lambda b: (0, 0)),
        ],
        out_specs=pl.BlockSpec((1, T, Wp), lambda b: (b, 0, 0)),
        out_shape=jax.ShapeDtypeStruct((B, T, Wp), F32),
        compiler_params=_cparams(("arbitrary",)),
        name="pool",
    )(u, wbd.astype(BF16), pool_scale.reshape(1, Wp))


def _route(logits_t, bias_col):
    E, tm = logits_t.shape
    per_group = E // N_GROUPS
    neg_inf = -jnp.inf
    scores = _sigmoid(logits_t)
    choice = scores + bias_col
    c3 = choice.reshape(N_GROUPS, per_group, tm)
    sub = lax.broadcasted_iota(jnp.int32, c3.shape, 1)
    m1 = jnp.max(c3, axis=1, keepdims=True)
    first = jnp.min(jnp.where(c3 == m1, sub, per_group), axis=1, keepdims=True)
    m2 = jnp.max(jnp.where(sub == first, neg_inf, c3), axis=1, keepdims=True)
    gs = m1 + m2
    gidx = lax.broadcasted_iota(jnp.int32, gs.shape, 0)
    grank = jnp.zeros(gs.shape, jnp.int32)
    for j in range(N_GROUPS):
        other = gs[j:j + 1]
        ahead = jnp.where(other > gs, 1, jnp.where((other == gs) & (gidx > j), 1, 0))
        grank = grank + ahead
    gmask = jnp.broadcast_to(grank < TOPK_GROUPS, c3.shape)
    masked = jnp.where(gmask, c3, neg_inf).reshape(E, tm)
    eidx = lax.broadcasted_iota(jnp.int32, (E, tm), 0)
    top = jnp.zeros((E, tm), F32)
    for _ in range(TOP_K):
        best = jnp.max(masked, axis=0, keepdims=True)
        first = jnp.min(jnp.where(masked == best, eidx, E), axis=0, keepdims=True)
        hit = eidx == first
        top = jnp.where(hit, scores, top)
        masked = jnp.where(hit, neg_inf, masked)
    return top / jnp.sum(top, axis=0, keepdims=True) * ROUTED_SCALE


def _outproj_kernel(yr_ref, yt_ref, yp_ref, x_ref, w1_ref, w2_ref, w3_ref, g1_ref, ng_ref,
                    sc_ref, sh_ref, wr_ref, rb_ref, x1_ref, h_ref, wc_ref):
    mixed = (_dot(yr_ref[...], w1_ref[...]) + _dot(yt_ref[...], w2_ref[...])
             + _dot(yp_ref[...], w3_ref[...]))
    x1 = x_ref[...] + g1_ref[0] * mixed
    x1_ref[...] = x1
    h = _modulated_norm(x1, ng_ref[...], sc_ref[0], sh_ref[0])
    h_ref[...] = h.astype(BF16)
    logits_t = lax.dot_general(wr_ref[...], h, (((1,), (1,)), ((), ())),
                               preferred_element_type=F32, precision=HIGHEST)
    wc_t = _route(logits_t, rb_ref[...])
    pad = jnp.zeros((LANES - N_EXPERTS, wc_t.shape[1]), F32)
    wc_ref[...] = jnp.concatenate([wc_t, pad], axis=0).T


def _outproj(yr, yt, yp, x2, w_out, gate1, ng, scale2, shift2, w_router, router_bias, T):
    N, D = x2.shape
    B = N // T
    tm = min(512, T)
    per_b = T // tm
    W = WIDTH
    w1 = w_out[:W].astype(BF16)
    w2 = w_out[W:2 * W].astype(BF16)
    w3 = w_out[2 * W:].astype(BF16)
    row = lambda i: (i, 0)
    const = lambda i: (0, 0)
    bvec = lambda i: (i // per_b, 0, 0)
    return pl.pallas_call(
        _outproj_kernel,
        grid=(N // tm,),
        in_specs=[
            pl.BlockSpec((tm, W), row),
            pl.BlockSpec((tm, W), row),
            pl.BlockSpec((tm, POOL_WIDTH), row),
            pl.BlockSpec((tm, D), row),
            pl.BlockSpec(w1.shape, const),
            pl.BlockSpec(w2.shape, const),
            pl.BlockSpec(w3.shape, const),
            pl.BlockSpec((1, 1, D), bvec),
            pl.BlockSpec((1, D), const),
            pl.BlockSpec((1, 1, D), bvec),
            pl.BlockSpec((1, 1, D), bvec),
            pl.BlockSpec((N_EXPERTS, D), const),
            pl.BlockSpec((N_EXPERTS, 1), const),
        ],
        out_specs=[
            pl.BlockSpec((tm, D), row),
            pl.BlockSpec((tm, D), row),
            pl.BlockSpec((tm, LANES), row),
        ],
        out_shape=[
            jax.ShapeDtypeStruct((N, D), F32),
            jax.ShapeDtypeStruct((N, D), BF16),
            jax.ShapeDtypeStruct((N, LANES), F32),
        ],
        compiler_params=_cparams(("arbitrary",)),
        name="outproj_router",
    )(yr, yt, yp, x2, w1, w2, w3, gate1.reshape(B, 1, D), ng.reshape(1, D),
      scale2.reshape(B, 1, D), shift2.reshape(B, 1, D), w_router.T,
      router_bias.reshape(N_EXPERTS, 1))


def _slot_positions(wc):
    tm = wc.shape[0]
    sel = jnp.where(wc > 0.0, 1.0, 0.0).astype(BF16)
    ti = lax.broadcasted_iota(jnp.int32, (tm, tm), 0)
    si = lax.broadcasted_iota(jnp.int32, (tm, tm), 1)
    earlier = jnp.where(ti > si, 1.0, 0.0).astype(BF16)
    return sel, jnp.dot(earlier, sel, preferred_element_type=F32)


def _dispatch_kernel(h_ref, wc_ref, rept_ref, xs_ref, over_ref, cnt_ref):
    nb, ts, D = h_ref.shape
    tm = nb * ts
    wc = wc_ref[...].reshape(tm, LANES)
    sel, pos = _slot_positions(wc)
    over_ref[...] = jnp.where(pos >= MOE_CAP, wc, 0.0).reshape(nb, ts, LANES)
    cnt_ref[...] = jnp.sum(sel.astype(F32), axis=0, keepdims=True)
    rept = rept_ref[...]
    code = jnp.where(wc > 0.0, pos, -1.0)
    code_rows = _dot_nt(rept, code)
    slot = (lax.broadcasted_iota(jnp.int32, (rept.shape[0], 1), 0) % MOE_CAP).astype(F32)
    onehot = jnp.where(code_rows == slot, 1.0, 0.0).astype(BF16)
    xs_ref[...] = jnp.dot(onehot, h_ref[...].reshape(tm, D),
                          preferred_element_type=F32).astype(BF16)


def _dispatch(h, wc, B, T):
    N, D = h.shape
    ts = MOE_TILE_SEQ
    nt = T // ts
    L = N_EXPERTS * MOE_CAP
    rows = np.arange(L) // MOE_CAP
    rept = jnp.asarray((rows[:, None] == np.arange(LANES)[None, :]).astype(np.float32)).astype(BF16)
    tile = lambda i: (0, i, 0, 0)
    xs, over, cnt = pl.pallas_call(
        _dispatch_kernel,
        grid=(nt,),
        in_specs=[
            pl.BlockSpec((B, None, ts, D), tile),
            pl.BlockSpec((B, None, ts, LANES), tile),
            pl.BlockSpec((L, LANES), lambda i: (0, 0)),
        ],
        out_specs=[
            pl.BlockSpec((None, L, D), lambda i: (i, 0, 0)),
            pl.BlockSpec((B, None, ts, LANES), tile),
            pl.BlockSpec((None, 1, LANES), lambda i: (i, 0, 0)),
        ],
        out_shape=[
            jax.ShapeDtypeStruct((nt, L, D), BF16),
            jax.ShapeDtypeStruct((B, nt, ts, LANES), F32),
            jax.ShapeDtypeStruct((nt, 1, LANES), F32),
        ],
        compiler_params=_cparams(("arbitrary",)),
        name="moe_dispatch",
    )(h.reshape(B, nt, ts, D), wc.reshape(B, nt, ts, LANES), rept)
    return xs, over.reshape(N, LANES), cnt.reshape(nt, LANES)[:, :N_EXPERTS]


def _experts_kernel(used_ref, x_ref, wg_ref, wu_ref, wd_ref, o_ref, wg_s, wu_s, wd_s):
    @pl.when(pl.program_id(1) == 0)
    def _():
        wg_s[...] = wg_ref[0, 0].astype(BF16)
        wu_s[...] = wu_ref[0, 0].astype(BF16)
        wd_s[...] = wd_ref[0, 0].astype(BF16)

    tg, cap, D = x_ref.shape
    used = used_ref[pl.program_id(0), pl.program_id(1)]

    def run(n):
        x = x_ref[:, 0:n, :].reshape(tg * n, D)
        hid = _silu(jnp.dot(x, wg_s[...], preferred_element_type=F32)) * jnp.dot(
            x, wu_s[...], preferred_element_type=F32)
        o_ref[:, 0:n, :] = _dot(hid, wd_s[...]).astype(BF16).reshape(tg, n, D)
        if n < cap:
            o_ref[:, n:cap, :] = jnp.zeros((tg, cap - n, D), BF16)

    steps = [n for n in MOE_USED_STEPS if n < cap] + [cap]
    lo = 0
    for n in steps:
        pl.when((used > lo) & (used <= n) if n < cap else used > lo)(functools.partial(run, n))
        lo = n

    @pl.when(used == 0)
    def _():
        o_ref[...] = jnp.zeros_like(o_ref)


def _experts(xs, cnt, wg, wu, wd, layer):
    nt, L, D = xs.shape
    _, E, _, Hd = wg.shape
    cap = L // E
    tg = math.gcd(nt, 16)
    used = jnp.minimum(jnp.max(cnt.reshape(nt // tg, tg, E), axis=1), cap).T.astype(jnp.int32)
    slots = pl.BlockSpec((tg, None, cap, D), lambda e, g, u: (g, e, 0, 0))
    grid_spec = pltpu.PrefetchScalarGridSpec(
        num_scalar_prefetch=1,
        grid=(E, nt // tg),
        in_specs=[
            slots,
            pl.BlockSpec((1, 1, D, Hd), lambda e, g, u: (layer, e, 0, 0)),
            pl.BlockSpec((1, 1, D, Hd), lambda e, g, u: (layer, e, 0, 0)),
            pl.BlockSpec((1, 1, Hd, D), lambda e, g, u: (layer, e, 0, 0)),
        ],
        out_specs=slots,
        scratch_shapes=[pltpu.VMEM((D, Hd), BF16), pltpu.VMEM((D, Hd), BF16),
                        pltpu.VMEM((Hd, D), BF16)],
    )
    return pl.pallas_call(
        _experts_kernel,
        grid_spec=grid_spec,
        out_shape=jax.ShapeDtypeStruct((nt, E, cap, D), BF16),
        compiler_params=_cparams(("arbitrary", "arbitrary")),
        name="moe_experts",
    )(used, xs.reshape(nt, E, cap, D), wg, wu, wd).reshape(nt, L, D)


def _combine_kernel(y_ref, wc_ref, rep_ref, h_ref, x1_ref, ex_ref, g2_ref, sg_ref, su_ref, sd_ref,
                    fg_ref, o_ref, *, final_norm):
    nb, ts, D = h_ref.shape
    tm = nb * ts
    wc = wc_ref[...].reshape(tm, LANES)
    _, pos = _slot_positions(wc)
    rep = rep_ref[...]
    pos_lanes = _dot(pos, rep)
    w_lanes = _dot(wc, rep)
    slot = (lax.broadcasted_iota(jnp.int32, (1, rep.shape[1]), 1) % MOE_CAP).astype(F32)
    weighted = jnp.where(pos_lanes == slot, w_lanes, 0.0).astype(BF16)
    routed = jnp.dot(weighted, y_ref[...], preferred_element_type=F32)
    h = h_ref[...].reshape(tm, D)
    hid = _silu(jnp.dot(h, sg_ref[...], preferred_element_type=F32)) * jnp.dot(
        h, su_ref[...], preferred_element_type=F32)
    y = routed + ex_ref[...].reshape(tm, D) + _dot(hid, sd_ref[...])
    gate = jnp.broadcast_to(g2_ref[...], (nb, ts, D)).reshape(tm, D)
    xo = x1_ref[...].reshape(tm, D) + gate * y
    if final_norm:
        ms = jnp.mean(xo * xo, axis=-1, keepdims=True)
        xo = xo * lax.rsqrt(ms + NORM_EPS) * fg_ref[...]
    o_ref[...] = xo.reshape(nb, ts, D)


def _combine(ys, wc, h, x1, extra, gate2, sg, su, sd, final_g, B, T, final_norm):
    N, D = x1.shape
    ts = MOE_TILE_SEQ
    nt = T // ts
    L = N_EXPERTS * MOE_CAP
    cols = np.arange(L) // MOE_CAP
    rep = jnp.asarray((np.arange(LANES)[:, None] == cols[None, :]).astype(np.float32)).astype(BF16)
    tile = lambda i: (0, i, 0, 0)
    const = lambda i: (0, 0)
    tok = lambda a: a.reshape(B, nt, ts, a.shape[-1])
    out = pl.pallas_call(
        functools.partial(_combine_kernel, final_norm=final_norm),
        grid=(nt,),
        in_specs=[
            pl.BlockSpec((None, L, D), lambda i: (i, 0, 0)),
            pl.BlockSpec((B, None, ts, LANES), tile),
            pl.BlockSpec((LANES, L), const),
            pl.BlockSpec((B, None, ts, D), tile),
            pl.BlockSpec((B, None, ts, D), tile),
            pl.BlockSpec((B, None, ts, D), tile),
            pl.BlockSpec((B, 1, D), lambda i: (0, 0, 0)),
            pl.BlockSpec(sg.shape, const),
            pl.BlockSpec(su.shape, const),
            pl.BlockSpec(sd.shape, const),
            pl.BlockSpec((1, D), const),
        ],
        out_specs=pl.BlockSpec((B, None, ts, D), tile),
        out_shape=jax.ShapeDtypeStruct((B, nt, ts, D), F32),
        compiler_params=_cparams(("arbitrary",)),
        name="moe_combine",
    )(ys, tok(wc), rep, tok(h), tok(x1), tok(extra), gate2.reshape(B, 1, D), sg, su, sd,
      final_g.reshape(1, D))
    return out.reshape(N, D)


def _overflow_kernel(h_ref, wc_ref, wg_ref, wu_ref, wd_ref, o_ref):
    e = pl.program_id(1)

    @pl.when(e == 0)
    def _():
        o_ref[...] = jnp.zeros_like(o_ref)

    h = h_ref[...]
    lane = lax.broadcasted_iota(jnp.int32, wc_ref.shape, 1)
    w = jnp.sum(jnp.where(lane == e, wc_ref[...], 0.0), axis=1, keepdims=True)
    hid = _silu(jnp.dot(h, wg_ref[0], preferred_element_type=F32)) * jnp.dot(
        h, wu_ref[0], preferred_element_type=F32)
    o_ref[...] += _dot(hid * w, wd_ref[0])


def _overflow(h, wc_over, wg, wu, wd):
    N, D = h.shape
    tm = math.gcd(N, 1024)
    E, _, Hd = wg.shape
    row = lambda i, e: (i, 0)
    return pl.pallas_call(
        _overflow_kernel,
        grid=(N // tm, E),
        in_specs=[
            pl.BlockSpec((tm, D), row),
            pl.BlockSpec((tm, LANES), row),
            pl.BlockSpec((1, D, Hd), lambda i, e: (e, 0, 0)),
            pl.BlockSpec((1, D, Hd), lambda i, e: (e, 0, 0)),
            pl.BlockSpec((1, Hd, D), lambda i, e: (e, 0, 0)),
        ],
        out_specs=pl.BlockSpec((tm, D), row),
        out_shape=jax.ShapeDtypeStruct((N, D), F32),
        compiler_params=_cparams(("arbitrary", "arbitrary")),
        name="moe_overflow",
    )(h, wc_over, wg, wu, wd)


def _moe(h, wc, wg, wu, wd, layer, sg, su, sd, x1, gate2, final_g, B, T, final_norm):
    xs, wc_over, cnt = _dispatch(h, wc, B, T)
    ys = _experts(xs, cnt, wg, wu, wd, layer)
    extra = lax.cond(jnp.any(wc_over != 0.0),
                     lambda: _overflow(h, wc_over, wg[layer].astype(BF16), wu[layer].astype(BF16),
                                       wd[layer].astype(BF16)),
                     lambda: jnp.zeros(x1.shape, F32))
    return _combine(ys, wc, h, x1, extra, gate2, sg, su, sd, final_g, B, T, final_norm)


def _split_w_in(w_in):
    W = WIDTH
    half = HEAD_DIM // 2
    j = np.arange(W)
    swap = (j // HEAD_DIM) * HEAD_DIM + (j % HEAD_DIM + half) % HEAD_DIM
    w_rwkv = w_in[:, :RWKV_COLS]
    w_ret = w_in[:, RWKV_COLS:RWKV_COLS + RET_COLS]
    w_pool = w_in[:, RWKV_COLS + RET_COLS:]
    q_sw = w_ret[:, 0:W][:, swap]
    k_sw = w_ret[:, W:2 * W][:, swap]
    w_ret_ext = jnp.concatenate([w_ret, q_sw, k_sw], axis=1)
    return w_rwkv.astype(BF16), w_ret_ext.astype(BF16), w_pool.astype(BF16)


def kernel(x, c, norm1_g, norm2_g, w_ada, b_ada, w_in, w_out, rwkv_mu, rwkv_w_up, rwkv_w0,
           rwkv_a_up, rwkv_a0, rwkv_g_up, rwkv_k_k, rwkv_k_a, rwkv_r_k, rwkv_lnx_w, rwkv_lnx_b,
           pool_w, pool_scale, w_router, router_bias, we_gate, we_up, we_down,
           ws_gate, ws_up, ws_down, final_g):
    B, T, D = x.shape
    L = w_in.shape[0]
    N = B * T
    mod = _adaln(c, w_ada, b_ada)
    x2 = x.reshape(N, D)
    for l in range(L):
        shift1, scale1, gate1, shift2, scale2, gate2 = jnp.split(mod[l], 6, axis=-1)
        w1, w2, w3 = _split_w_in(w_in[l])
        p_rwkv, p_ret, p_pool = _inproj(x2, norm1_g[l], scale1, shift1, w1, w2, w3, T)
        y_rwkv = _rwkv(p_rwkv.reshape(B, T, -1), rwkv_mu[l], rwkv_w_up[l], rwkv_w0[l],
                       rwkv_a_up[l], rwkv_a0[l], rwkv_g_up[l], rwkv_k_k[l], rwkv_k_a[l],
                       rwkv_r_k[l], rwkv_lnx_w[l], rwkv_lnx_b[l])
        y_ret = _retention(p_ret.reshape(B, T, -1))
        y_pool = _pool(p_pool.reshape(B, T, -1), pool_w[l], pool_scale[l])
        x1, h, wc = _outproj(y_rwkv.reshape(N, -1), y_ret.reshape(N, -1), y_pool.reshape(N, -1),
                             x2, w_out[l], gate1, norm2_g[l], scale2, shift2,
                             w_router[l], router_bias[l], T)
        x2 = _moe(h, wc, we_gate, we_up, we_down, l,
                  ws_gate[l].astype(BF16), ws_up[l].astype(BF16),
                  ws_down[l].astype(BF16), x1, gate2, final_g, B, T,
                  final_norm=(l == L - 1))
    return x2.reshape(B, T, D)
```

```python
import functools
import math

import numpy as np
import jax
import jax.numpy as jnp
from jax import lax
from jax.experimental import pallas as pl
from jax.experimental.pallas import tpu as pltpu

F32 = jnp.float32
BF16 = jnp.bfloat16
HIGHEST = lax.Precision.HIGHEST

D_MODEL = 1024
HEADS = 6
HEAD_DIM = 64
WIDTH = HEADS * HEAD_DIM
W_LORA = 64
A_LORA = 64
G_LORA = 128
RWKV_COLS = 3 * WIDTH + W_LORA + A_LORA + G_LORA
RET_COLS = 4 * WIDTH
RET_EXT_COLS = 6 * WIDTH
RET_CHUNK = 128
RWKV_CHUNK = 128
RWKV_SUB = 16
RET_ROWS = 4
RWKV_ROWS = 4
POOL_GROUPS = 4
POOL_GROUP_DIM = 64
POOL_WIDTH = POOL_GROUPS * POOL_GROUP_DIM
POOL_WINDOWS = (2, 4, 8, 16)
N_EXPERTS = 64
TOP_K = 8
N_GROUPS = 8
TOPK_GROUPS = 4
EXPERT_HIDDEN = 256
ROUTED_SCALE = 2.5
NORM_EPS = 1e-6
RWKV_LNX_EPS = 64e-5
RET_NORM_EPS = 1e-6
ROPE_BASE = 10000.0
LANES = 128
MOE_TILE_SEQ = 16
MOE_CAP = 80
MOE_USED_STEPS = (48, 64)
VMEM_LIMIT = 48 * 1024 * 1024


def _cparams(sem):
    return pltpu.CompilerParams(dimension_semantics=sem, vmem_limit_bytes=VMEM_LIMIT)


def _dot(a, b):
    return jnp.dot(a.astype(BF16), b.astype(BF16), preferred_element_type=F32)


def _dot_nt(a, b):
    return lax.dot_general(a.astype(BF16), b.astype(BF16), (((1,), (1,)), ((), ())),
                           preferred_element_type=F32)


def _dot_tn(a, b):
    return lax.dot_general(a.astype(BF16), b.astype(BF16), (((0,), (0,)), ((), ())),
                           preferred_element_type=F32)


def _dot_f32(a, b):
    return jnp.dot(a, b, preferred_element_type=F32, precision=HIGHEST)


def _sigmoid(x):
    return 1.0 / (1.0 + jnp.exp(-x))


def _silu(x):
    return x * _sigmoid(x)


def _head_masks(rows, width):
    lane = lax.broadcasted_iota(jnp.int32, (rows, width), 1)
    return [lane // HEAD_DIM == h for h in range(width // HEAD_DIM)]


def _stack_heads(x, masks):
    return jnp.concatenate([jnp.where(m, x, 0.0) for m in masks], axis=0).astype(BF16)


def _select_heads(stacked, masks, c):
    out = stacked[0:c]
    for h in range(1, len(masks)):
        out = jnp.where(masks[h], stacked[h * c:(h + 1) * c], out)
    return out


def _adaln_kernel(c_ref, w_ref, b_ref, o_ref):
    o_ref[0] = _dot_f32(_silu(c_ref[...]), w_ref[0]) + b_ref[0]


def _adaln(c, w_ada, b_ada):
    L, D, M = w_ada.shape
    B = c.shape[0]
    tn = 1536
    return pl.pallas_call(
        _adaln_kernel,
        grid=(L, M // tn),
        in_specs=[
            pl.BlockSpec((B, D), lambda l, j: (0, 0)),
            pl.BlockSpec((1, D, tn), lambda l, j: (l, 0, j)),
            pl.BlockSpec((1, 1, tn), lambda l, j: (l, 0, j)),
        ],
        out_specs=pl.BlockSpec((1, B, tn), lambda l, j: (l, 0, j)),
        out_shape=jax.ShapeDtypeStruct((L, B, M), F32),
        compiler_params=_cparams(("arbitrary", "arbitrary")),
        name="adaln",
    )(c, w_ada, b_ada.reshape(L, 1, M))


def _modulated_norm(x, g, scale, shift):
    ms = jnp.mean(x * x, axis=-1, keepdims=True)
    return x * lax.rsqrt(ms + NORM_EPS) * g * (1.0 + scale) + shift


def _inproj_kernel(x_ref, g_ref, sc_ref, sh_ref, w1_ref, w2_ref, w3_ref, o1_ref, o2_ref, o3_ref):
    h = _modulated_norm(x_ref[...], g_ref[...], sc_ref[0], sh_ref[0]).astype(BF16)
    o1_ref[...] = jnp.dot(h, w1_ref[...], preferred_element_type=F32)
    o2_ref[...] = jnp.dot(h, w2_ref[...], preferred_element_type=F32)
    o3_ref[...] = jnp.dot(h, w3_ref[...], preferred_element_type=F32)


def _inproj(x2, g, scale, shift, w1, w2, w3, T):
    N, D = x2.shape
    B = N // T
    tm = min(256, T)
    per_b = T // tm
    row = lambda i: (i, 0)
    const = lambda i: (0, 0)
    bvec = lambda i: (i // per_b, 0, 0)
    return pl.pallas_call(
        _inproj_kernel,
        grid=(N // tm,),
        in_specs=[
            pl.BlockSpec((tm, D), row),
            pl.BlockSpec((1, D), const),
            pl.BlockSpec((1, 1, D), bvec),
            pl.BlockSpec((1, 1, D), bvec),
            pl.BlockSpec(w1.shape, const),
            pl.BlockSpec(w2.shape, const),
            pl.BlockSpec(w3.shape, const),
        ],
        out_specs=[
            pl.BlockSpec((tm, w1.shape[1]), row),
            pl.BlockSpec((tm, w2.shape[1]), row),
            pl.BlockSpec((tm, w3.shape[1]), row),
        ],
        out_shape=[
            jax.ShapeDtypeStruct((N, w1.shape[1]), F32),
            jax.ShapeDtypeStruct((N, w2.shape[1]), F32),
            jax.ShapeDtypeStruct((N, w3.shape[1]), F32),
        ],
        compiler_params=_cparams(("arbitrary",)),
        name="inproj",
    )(x2, g.reshape(1, D), scale.reshape(B, 1, D), shift.reshape(B, 1, D), w1, w2, w3)


def _unit_lower_inverse(a3):
    H, C, _ = a3.shape
    ri = lax.broadcasted_iota(jnp.int32, (H, C, C), 1)
    ci = lax.broadcasted_iota(jnp.int32, (H, C, C), 2)
    eye = (ri == ci).astype(F32)
    same = (ri // RWKV_SUB) == (ci // RWKV_SUB)
    dm = jnp.where(same, a3, 0.0)
    off = jnp.where(same, 0.0, a3)

    def bmm(x, y):
        return jnp.einsum('hij,hjk->hik', x.astype(BF16), y.astype(BF16),
                          preferred_element_type=F32)

    d2 = bmm(dm, dm)
    d4 = bmm(d2, d2)
    d8 = bmm(d4, d4)
    x = eye - dm
    x = x + bmm(x, d2)
    x = x + bmm(x, d4)
    x = x + bmm(x, d8)
    n = bmm(x, off)
    y = eye - n
    power = bmm(n, n)
    order = 2
    while order < C // RWKV_SUB:
        y = y + bmm(y, power)
        order *= 2
        if order < C // RWKV_SUB:
            power = bmm(power, power)
    return bmm(y, x)


def _rwkv_prepare(p, carry, prm):
    (mu, wup, w0, aup, a0, gup, k_k, k_a, r_k, lnw, lnb, hsum) = prm
    C = RWKV_CHUNK
    W = WIDTH
    H = HEADS
    row = lax.broadcasted_iota(jnp.int32, (C, 1), 0)
    prev = jnp.where(row == 0, carry, pltpu.roll(p, 1, 0))
    xs = p + (prev - p) * mu

    r = xs[:, 0:W]
    k = xs[:, W:2 * W]
    v = xs[:, 2 * W:3 * W]
    xwa = xs[:, 3 * W:3 * W + W_LORA + A_LORA]
    xg = xs[:, 3 * W + W_LORA + A_LORA:]

    z = w0 + _dot(jnp.tanh(xwa), wup)
    log_w = -math.exp(-0.5) * _sigmoid(z)
    a = _sigmoid(a0 + _dot(xwa, aup))
    g = _dot(_sigmoid(xg), gup)
    kk = k * k_k
    kk = kk / jnp.maximum(jnp.sqrt(_dot(kk * kk, hsum)), 1e-12)
    k = k * (1.0 + (a - 1.0) * k_a)

    ti = lax.broadcasted_iota(jnp.int32, (C, C), 0)
    si = lax.broadcasted_iota(jnp.int32, (C, C), 1)
    tri = jnp.where(ti >= si, 1.0, 0.0).astype(BF16)
    log_w_hi = log_w.astype(BF16)
    cum = (jnp.dot(tri, log_w_hi, preferred_element_type=F32)
           + _dot(tri, log_w - log_w_hi.astype(F32)))
    g_in = jnp.exp(cum)
    g_inv = jnp.exp(-cum)
    g_end = g_in[C - 1:C, :]
    kt = kk * jnp.exp(cum - log_w)
    bt = kk * a * g_inv
    kq = k * g_inv
    rt = r * g_in

    masks = _head_masks(C, W)
    t3 = lax.broadcasted_iota(jnp.int32, (H, C, C), 1)
    s3 = lax.broadcasted_iota(jnp.int32, (H, C, C), 2)
    a_ab = jnp.where(t3 > s3, _dot_nt(_stack_heads(kt, masks), bt).reshape(H, C, C), 0.0)
    bt_rows = _stack_heads(bt, masks)
    kq_rows = _stack_heads(kq, masks)
    t1 = lax.broadcasted_iota(jnp.int32, (C, H * C), 0)
    s1 = lax.broadcasted_iota(jnp.int32, (C, H * C), 1) % C
    a_ak = jnp.where(t1 > s1, _dot_nt(kt, kq_rows), 0.0)
    t2 = lax.broadcasted_iota(jnp.int32, (C, 2 * H * C), 0)
    s2 = lax.broadcasted_iota(jnp.int32, (C, 2 * H * C), 1) % C
    p_r = jnp.where(t2 >= s2, _dot_nt(rt, jnp.concatenate([bt_rows, kq_rows], axis=0)), 0.0)
    return a_ab, (a_ak, p_r, kt, rt, bt, kq, g_end, r, k, v, g)


def _rwkv_finish(vals, t_inv, s0, prm):
    (a_ak, p_r, kt, rt, bt, kq, g_end, r, k, v, g) = vals
    (mu, wup, w0, aup, a0, gup, k_k, k_a, r_k, lnw, lnb, hsum) = prm
    C = RWKV_CHUNK
    W = WIDTH
    H = HEADS
    masks = _head_masks(C, W)
    ks = _dot_nt(kt, s0)
    rs = _dot_nt(rt, s0)
    v_rows = _stack_heads(v, masks)
    av = _dot(a_ak, v_rows)
    u = _select_heads(_dot(t_inv.reshape(H * C, C), -(ks + av)), masks, C)
    y = rs + _dot(p_r, jnp.concatenate([_stack_heads(u, masks), v_rows], axis=0))
    upd = _dot_tn(jnp.concatenate([u, v], axis=0),
                  jnp.concatenate([bt * g_end, kq * g_end], axis=0))
    hi = lax.broadcasted_iota(jnp.int32, (W, W), 0) // HEAD_DIM
    hj = lax.broadcasted_iota(jnp.int32, (W, W), 1) // HEAD_DIM
    s_new = s0 * g_end + jnp.where(hi == hj, upd, 0.0)

    inv_d = 1.0 / HEAD_DIM
    mean = _dot(y, hsum) * inv_d
    yc = y - mean
    var = _dot(yc * yc, hsum) * inv_d
    yn = yc * lax.rsqrt(var + RWKV_LNX_EPS) * lnw + lnb
    bonus = _dot(r * k * r_k, hsum) * v
    return (yn + bonus) * g, s_new


def _rwkv_kernel(p_ref, mu_ref, wup_ref, w0_ref, aup_ref, a0_ref, gup_ref, kk_ref, ka_ref,
                 rk_ref, lnw_ref, lnb_ref, hsum_ref, o_ref, carry_ref, s_ref):
    @pl.when(pl.program_id(1) == 0)
    def _():
        carry_ref[...] = jnp.zeros_like(carry_ref)
        s_ref[...] = jnp.zeros_like(s_ref)

    prm = tuple(ref[...] for ref in (mu_ref, wup_ref, w0_ref, aup_ref, a0_ref, gup_ref, kk_ref,
                                     ka_ref, rk_ref, lnw_ref, lnb_ref, hsum_ref))
    G = p_ref.shape[0]
    H = HEADS
    a_abs, vals = [], []
    for i in range(G):
        p = p_ref[i]
        a_ab, val = _rwkv_prepare(p, carry_ref[i], prm)
        carry_ref[i] = p[RWKV_CHUNK - 1:RWKV_CHUNK, :]
        a_abs.append(a_ab)
        vals.append(val)
    t_inv = _unit_lower_inverse(jnp.concatenate(a_abs, axis=0))
    for i in range(G):
        out, s_new = _rwkv_finish(vals[i], t_inv[i * H:(i + 1) * H], s_ref[i], prm)
        s_ref[i] = s_new
        o_ref[i] = out


def _rwkv(p, mu, wup, w0, aup, a0, gup, k_k, k_a, r_k, lnx_w, lnx_b):
    B, T, _ = p.shape
    C = RWKV_CHUNK
    W = WIDTH
    G = RWKV_ROWS if B % RWKV_ROWS == 0 else 1
    lora_in = W_LORA + A_LORA
    wup_pad = jnp.zeros((lora_in, W), F32).at[:W_LORA].set(wup).astype(BF16)
    aup_pad = jnp.zeros((lora_in, W), F32).at[W_LORA:].set(aup).astype(BF16)
    head = np.arange(W) // HEAD_DIM
    hsum = jnp.asarray((head[:, None] == head[None, :]).astype(np.float32)).astype(BF16)
    vec = lambda a: a.reshape(1, -1)
    const = lambda b, t: (0, 0)
    params = [vec(mu), wup_pad, vec(w0), aup_pad, vec(a0), gup.astype(BF16), vec(k_k), vec(k_a),
              vec(r_k), vec(lnx_w), vec(lnx_b), hsum]
    return pl.pallas_call(
        _rwkv_kernel,
        grid=(B // G, T // C),
        in_specs=[pl.BlockSpec((G, C, RWKV_COLS), lambda b, t: (b, t, 0))]
        + [pl.BlockSpec(a.shape, const) for a in params],
        out_specs=pl.BlockSpec((G, C, W), lambda b, t: (b, t, 0)),
        out_shape=jax.ShapeDtypeStruct((B, T, W), F32),
        scratch_shapes=[pltpu.VMEM((G, 1, RWKV_COLS), F32), pltpu.VMEM((G, W, W), F32)],
        compiler_params=_cparams(("arbitrary", "arbitrary")),
        name="rwkv7",
    )(p, *params)


def _retention_kernel(p_ref, cos_ref, sin_ref, dec_ref, xi_ref, zeta_ref, cd_ref, hsum_ref,
                      o_ref, s_ref):
    C = RET_CHUNK
    W = WIDTH

    @pl.when(pl.program_id(1) == 0)
    def _():
        s_ref[...] = jnp.zeros_like(s_ref)

    cos = cos_ref[...]
    sin = sin_ref[...]
    masks = _head_masks(C, W)
    hi = lax.broadcasted_iota(jnp.int32, (W, W), 0) // HEAD_DIM
    hj = lax.broadcasted_iota(jnp.int32, (W, W), 1) // HEAD_DIM
    for i in range(p_ref.shape[0]):
        p = p_ref[i]
        q = p[:, 0:W] * cos + p[:, 4 * W:5 * W] * sin
        k = (p[:, W:2 * W] * cos + p[:, 5 * W:6 * W] * sin) * (HEAD_DIM ** -0.5)
        v = p[:, 2 * W:3 * W]
        gate = p[:, 3 * W:4 * W]
        scores = _dot_nt(q, _stack_heads(k, masks)) * dec_ref[...]
        y = _dot(scores, _stack_heads(v, masks))
        s0 = s_ref[i]
        y = y + _dot(q * xi_ref[...], s0)
        kv = _dot_tn(k * zeta_ref[...], v)
        s_ref[i] = s0 * cd_ref[...] + jnp.where(hi == hj, kv, 0.0)
        ms = _dot(y * y, hsum_ref[...]) * (1.0 / HEAD_DIM)
        o_ref[i] = _silu(gate) * (y * lax.rsqrt(ms + RET_NORM_EPS))


def _retention_tables(T):
    C, H, d = RET_CHUNK, HEADS, HEAD_DIM
    pos = jnp.arange(T, dtype=F32)
    inv_freq = ROPE_BASE ** (-jnp.arange(0, d, 2, dtype=F32) / d)
    ang = pos[:, None] * inv_freq[None, :]
    cos = jnp.cos(ang)
    sin = jnp.sin(ang)
    cos_full = jnp.tile(jnp.concatenate([cos, cos], -1), (1, H))
    sin_full = jnp.tile(jnp.concatenate([-sin, sin], -1), (1, H))
    log_gamma = jnp.log1p(-(2.0 ** (-5.0 - jnp.arange(H, dtype=F32))))
    idx = jnp.arange(C, dtype=F32)
    diff = idx[:, None] - idx[None, :]
    dec = jnp.where(diff >= 0, jnp.exp(log_gamma[:, None, None] * jnp.maximum(diff, 0.0)), 0.0)
    xi = jnp.exp(log_gamma[:, None] * (idx + 1.0))
    zeta = jnp.exp(log_gamma[:, None] * (C - 1.0 - idx))
    cd = jnp.exp(log_gamma * C)
    per_lane = lambda a: jnp.repeat(a.T, d, axis=1)
    return (cos_full, sin_full, dec.transpose(1, 0, 2).reshape(C, H * C), per_lane(xi), per_lane(zeta),
            jnp.repeat(cd, d).reshape(1, H * d))


def _retention(p):
    B, T, cols = p.shape
    C = RET_CHUNK
    W = WIDTH
    cos, sin, dec, xi, zeta, cd = _retention_tables(T)
    head = np.arange(W) // HEAD_DIM
    hsum = jnp.asarray((head[:, None] == head[None, :]).astype(np.float32)).astype(BF16)
    const = lambda b, t: (0, 0)
    G = RET_ROWS if B % RET_ROWS == 0 else 1
    return pl.pallas_call(
        _retention_kernel,
        grid=(B // G, T // C),
        in_specs=[
            pl.BlockSpec((G, C, cols), lambda b, t: (b, t, 0)),
            pl.BlockSpec((C, W), lambda b, t: (t, 0)),
            pl.BlockSpec((C, W), lambda b, t: (t, 0)),
            pl.BlockSpec(dec.shape, const),
            pl.BlockSpec(xi.shape, const),
            pl.BlockSpec(zeta.shape, const),
            pl.BlockSpec(cd.shape, const),
            pl.BlockSpec(hsum.shape, const),
        ],
        out_specs=pl.BlockSpec((G, C, W), lambda b, t: (b, t, 0)),
        out_shape=jax.ShapeDtypeStruct((B, T, W), F32),
        scratch_shapes=[pltpu.VMEM((G, W, W), F32)],
        compiler_params=_cparams(("arbitrary", "arbitrary")),
        name="retention",
    )(p, cos, sin, dec, xi, zeta, cd, hsum)


def _pool_kernel(u_ref, w_ref, scale_ref, o_ref):
    u = u_ref[0]
    T = u.shape[0]
    row = lax.broadcasted_iota(jnp.int32, (T, 1), 0)

    def lag(x, k):
        return jnp.where(row >= k, pltpu.roll(x, k, 0), 0.0)

    s2 = u + lag(u, 1)
    s4 = s2 + lag(s2, 2)
    s8 = s4 + lag(s4, 4)
    s16 = s8 + lag(s8, 8)
    grp = lax.broadcasted_iota(jnp.int32, (1, POOL_WIDTH), 1) // POOL_GROUP_DIM
    s = jnp.where(grp == 0, s2, jnp.where(grp == 1, s4, jnp.where(grp == 2, s8, s16)))
    win = jnp.where(grp == 0, POOL_WINDOWS[0],
                    jnp.where(grp == 1, POOL_WINDOWS[1],
                              jnp.where(grp == 2, POOL_WINDOWS[2], POOL_WINDOWS[3])))
    count = jnp.minimum(row + 1, win).astype(F32)
    pooled = s / count - u
    o_ref[0] = _dot(pooled, w_ref[...]) * scale_ref[...]


def _pool(u, pool_w, pool_scale):
    B, T, Wp = u.shape
    G, d = POOL_GROUPS, POOL_GROUP_DIM
    wbd = jnp.zeros((Wp, Wp), F32)
    for gi in range(G):
        wbd = wbd.at[gi * d:(gi + 1) * d, gi * d:(gi + 1) * d].set(pool_w[gi])
    return pl.pallas_call(
        _pool_kernel,
        grid=(B,),
        in_specs=[
            pl.BlockSpec((1, T, Wp), lambda b: (b, 0, 0)),
            pl.BlockSpec((Wp, Wp), lambda b: (0, 0)),
            pl.BlockSpec((1, Wp), lambda b: (0, 0)),
        ],
        out_specs=pl.BlockSpec((1, T, Wp), lambda b: (b, 0, 0)),
        out_shape=jax.ShapeDtypeStruct((B, T, Wp), F32),
        compiler_params=_cparams(("arbitrary",)),
        name="pool",
    )(u, wbd.astype(BF16), pool_scale.reshape(1, Wp))


def _route(logits_t, bias_col):
    E, tm = logits_t.shape
    per_group = E // N_GROUPS
    neg_inf = -jnp.inf
    scores = _sigmoid(logits_t)
    choice = scores + bias_col
    c3 = choice.reshape(N_GROUPS, per_group, tm)
    sub = lax.broadcasted_iota(jnp.int32, c3.shape, 1)
    m1 = jnp.max(c3, axis=1, keepdims=True)
    first = jnp.min(jnp.where(c3 == m1, sub, per_group), axis=1, keepdims=True)
    m2 = jnp.max(jnp.where(sub == first, neg_inf, c3), axis=1, keepdims=True)
    gs = m1 + m2
    gidx = lax.broadcasted_iota(jnp.int32, gs.shape, 0)
    grank = jnp.zeros(gs.shape, jnp.int32)
    for j in range(N_GROUPS):
        other = gs[j:j + 1]
        ahead = jnp.where(other > gs, 1, jnp.where((other == gs) & (gidx > j), 1, 0))
        grank = grank + ahead
    gmask = jnp.broadcast_to(grank < TOPK_GROUPS, c3.shape)
    masked = jnp.where(gmask, c3, neg_inf).reshape(E, tm)
    eidx = lax.broadcasted_iota(jnp.int32, (E, tm), 0)
    top = jnp.zeros((E, tm), F32)
    for _ in range(TOP_K):
        best = jnp.max(masked, axis=0, keepdims=True)
        first = jnp.min(jnp.where(masked == best, eidx, E), axis=0, keepdims=True)
        hit = eidx == first
        top = jnp.where(hit, scores, top)
        masked = jnp.where(hit, neg_inf, masked)
    return top / jnp.sum(top, axis=0, keepdims=True) * ROUTED_SCALE


def _outproj_kernel(yr_ref, yt_ref, yp_ref, x_ref, w1_ref, w2_ref, w3_ref, g1_ref, ng_ref,
                    sc_ref, sh_ref, wr_ref, rb_ref, x1_ref, h_ref, wc_ref):
    mixed = (_dot(yr_ref[...], w1_ref[...]) + _dot(yt_ref[...], w2_ref[...])
             + _dot(yp_ref[...], w3_ref[...]))
    x1 = x_ref[...] + g1_ref[0] * mixed
    x1_ref[...] = x1
    h = _modulated_norm(x1, ng_ref[...], sc_ref[0], sh_ref[0])
    h_ref[...] = h.astype(BF16)
    logits_t = lax.dot_general(wr_ref[...], h, (((1,), (1,)), ((), ())),
                               preferred_element_type=F32, precision=HIGHEST)
    wc_t = _route(logits_t, rb_ref[...])
    pad = jnp.zeros((LANES - N_EXPERTS, wc_t.shape[1]), F32)
    wc_ref[...] = jnp.concatenate([wc_t, pad], axis=0).T


def _outproj(yr, yt, yp, x2, w_out, gate1, ng, scale2, shift2, w_router, router_bias, T):
    N, D = x2.shape
    B = N // T
    tm = min(512, T)
    per_b = T // tm
    W = WIDTH
    w1 = w_out[:W].astype(BF16)
    w2 = w_out[W:2 * W].astype(BF16)
    w3 = w_out[2 * W:].astype(BF16)
    row = lambda i: (i, 0)
    const = lambda i: (0, 0)
    bvec = lambda i: (i // per_b, 0, 0)
    return pl.pallas_call(
        _outproj_kernel,
        grid=(N // tm,),
        in_specs=[
            pl.BlockSpec((tm, W), row),
            pl.BlockSpec((tm, W), row),
            pl.BlockSpec((tm, POOL_WIDTH), row),
            pl.BlockSpec((tm, D), row),
            pl.BlockSpec(w1.shape, const),
            pl.BlockSpec(w2.shape, const),
            pl.BlockSpec(w3.shape, const),
            pl.BlockSpec((1, 1, D), bvec),
            pl.BlockSpec((1, D), const),
            pl.BlockSpec((1, 1, D), bvec),
            pl.BlockSpec((1, 1, D), bvec),
            pl.BlockSpec((N_EXPERTS, D), const),
            pl.BlockSpec((N_EXPERTS, 1), const),
        ],
        out_specs=[
            pl.BlockSpec((tm, D), row),
            pl.BlockSpec((tm, D), row),
            pl.BlockSpec((tm, LANES), row),
        ],
        out_shape=[
            jax.ShapeDtypeStruct((N, D), F32),
            jax.ShapeDtypeStruct((N, D), BF16),
            jax.ShapeDtypeStruct((N, LANES), F32),
        ],
        compiler_params=_cparams(("arbitrary",)),
        name="outproj_router",
    )(yr, yt, yp, x2, w1, w2, w3, gate1.reshape(B, 1, D), ng.reshape(1, D),
      scale2.reshape(B, 1, D), shift2.reshape(B, 1, D), w_router.T,
      router_bias.reshape(N_EXPERTS, 1))


def _slot_positions(wc):
    tm = wc.shape[0]
    sel = jnp.where(wc > 0.0, 1.0, 0.0).astype(BF16)
    ti = lax.broadcasted_iota(jnp.int32, (tm, tm), 0)
    si = lax.broadcasted_iota(jnp.int32, (tm, tm), 1)
    earlier = jnp.where(ti > si, 1.0, 0.0).astype(BF16)
    return sel, jnp.dot(earlier, sel, preferred_element_type=F32)


def _dispatch_kernel(h_ref, wc_ref, rept_ref, xs_ref, over_ref, cnt_ref):
    nb, ts, D = h_ref.shape
    tm = nb * ts
    wc = wc_ref[...].reshape(tm, LANES)
    sel, pos = _slot_positions(wc)
    over_ref[...] = jnp.where(pos >= MOE_CAP, wc, 0.0).reshape(nb, ts, LANES)
    cnt_ref[...] = jnp.sum(sel.astype(F32), axis=0, keepdims=True)
    rept = rept_ref[...]
    code = jnp.where(wc > 0.0, pos, -1.0)
    code_rows = _dot_nt(rept, code)
    slot = (lax.broadcasted_iota(jnp.int32, (rept.shape[0], 1), 0) % MOE_CAP).astype(F32)
    onehot = jnp.where(code_rows == slot, 1.0, 0.0).astype(BF16)
    xs_ref[...] = jnp.dot(onehot, h_ref[...].reshape(tm, D),
                          preferred_element_type=F32).astype(BF16)


def _dispatch(h, wc, B, T):
    N, D = h.shape
    ts = MOE_TILE_SEQ
    nt = T // ts
    L = N_EXPERTS * MOE_CAP
    rows = np.arange(L) // MOE_CAP
    rept = jnp.asarray((rows[:, None] == np.arange(LANES)[None, :]).astype(np.float32)).astype(BF16)
    tile = lambda i: (0, i, 0, 0)
    xs, over, cnt = pl.pallas_call(
        _dispatch_kernel,
        grid=(nt,),
        in_specs=[
            pl.BlockSpec((B, None, ts, D), tile),
            pl.BlockSpec((B, None, ts, LANES), tile),
            pl.BlockSpec((L, LANES), lambda i: (0, 0)),
        ],
        out_specs=[
            pl.BlockSpec((None, L, D), lambda i: (i, 0, 0)),
            pl.BlockSpec((B, None, ts, LANES), tile),
            pl.BlockSpec((None, 1, LANES), lambda i: (i, 0, 0)),
        ],
        out_shape=[
            jax.ShapeDtypeStruct((nt, L, D), BF16),
            jax.ShapeDtypeStruct((B, nt, ts, LANES), F32),
            jax.ShapeDtypeStruct((nt, 1, LANES), F32),
        ],
        compiler_params=_cparams(("arbitrary",)),
        name="moe_dispatch",
    )(h.reshape(B, nt, ts, D), wc.reshape(B, nt, ts, LANES), rept)
    return xs, over.reshape(N, LANES), cnt.reshape(nt, LANES)[:, :N_EXPERTS]


def _experts_kernel(used_ref, x_ref, wg_ref, wu_ref, wd_ref, o_ref, wg_s, wu_s, wd_s):
    @pl.when(pl.program_id(1) == 0)
    def _():
        wg_s[...] = wg_ref[0, 0].astype(BF16)
        wu_s[...] = wu_ref[0, 0].astype(BF16)
        wd_s[...] = wd_ref[0, 0].astype(BF16)

    tg, cap, D = x_ref.shape
    used = used_ref[pl.program_id(0), pl.program_id(1)]

    def run(n):
        x = x_ref[:, 0:n, :].reshape(tg * n, D)
        hid = _silu(jnp.dot(x, wg_s[...], preferred_element_type=F32)) * jnp.dot(
            x, wu_s[...], preferred_element_type=F32)
        o_ref[:, 0:n, :] = _dot(hid, wd_s[...]).astype(BF16).reshape(tg, n, D)
        if n < cap:
            o_ref[:, n:cap, :] = jnp.zeros((tg, cap - n, D), BF16)

    steps = [n for n in MOE_USED_STEPS if n < cap] + [cap]
    lo = 0
    for n in steps:
        pl.when((used > lo) & (used <= n) if n < cap else used > lo)(functools.partial(run, n))
        lo = n

    @pl.when(used == 0)
    def _():
        o_ref[...] = jnp.zeros_like(o_ref)


def _experts(xs, cnt, wg, wu, wd, layer):
    nt, L, D = xs.shape
    _, E, _, Hd = wg.shape
    cap = L // E
    tg = math.gcd(nt, 16)
    used = jnp.minimum(jnp.max(cnt.reshape(nt // tg, tg, E), axis=1), cap).T.astype(jnp.int32)
    slots = pl.BlockSpec((tg, None, cap, D), lambda e, g, u: (g, e, 0, 0))
    grid_spec = pltpu.PrefetchScalarGridSpec(
        num_scalar_prefetch=1,
        grid=(E, nt // tg),
        in_specs=[
            slots,
            pl.BlockSpec((1, 1, D, Hd), lambda e, g, u: (layer, e, 0, 0)),
            pl.BlockSpec((1, 1, D, Hd), lambda e, g, u: (layer, e, 0, 0)),
            pl.BlockSpec((1, 1, Hd, D), lambda e, g, u: (layer, e, 0, 0)),
        ],
        out_specs=slots,
        scratch_shapes=[pltpu.VMEM((D, Hd), BF16), pltpu.VMEM((D, Hd), BF16),
                        pltpu.VMEM((Hd, D), BF16)],
    )
    return pl.pallas_call(
        _experts_kernel,
        grid_spec=grid_spec,
        out_shape=jax.ShapeDtypeStruct((nt, E, cap, D), BF16),
        compiler_params=_cparams(("arbitrary", "arbitrary")),
        name="moe_experts",
    )(used, xs.reshape(nt, E, cap, D), wg, wu, wd).reshape(nt, L, D)


def _combine_kernel(y_ref, wc_ref, rep_ref, h_ref, x1_ref, g2_ref, sg_ref, su_ref, sd_ref, fg_ref,
                    *rest, final_norm):
    ex_ref = rest[0] if len(rest) == 2 else None
    o_ref = rest[-1]
    nb, ts, D = h_ref.shape
    tm = nb * ts
    wc = wc_ref[...].reshape(tm, LANES)
    _, pos = _slot_positions(wc)
    rep = rep_ref[...]
    pos_lanes = _dot(pos, rep)
    w_lanes = _dot(wc, rep)
    slot = (lax.broadcasted_iota(jnp.int32, (1, rep.shape[1]), 1) % MOE_CAP).astype(F32)
    weighted = jnp.where(pos_lanes == slot, w_lanes, 0.0).astype(BF16)
    routed = jnp.dot(weighted, y_ref[...], preferred_element_type=F32)
    h = h_ref[...].reshape(tm, D)
    hid = _silu(jnp.dot(h, sg_ref[...], preferred_element_type=F32)) * jnp.dot(
        h, su_ref[...], preferred_element_type=F32)
    y = routed + _dot(hid, sd_ref[...])
    if ex_ref is not None:
        y = y + ex_ref[...].reshape(tm, D)
    gate = jnp.broadcast_to(g2_ref[...], (nb, ts, D)).reshape(tm, D)
    xo = x1_ref[...].reshape(tm, D) + gate * y
    if final_norm:
        ms = jnp.mean(xo * xo, axis=-1, keepdims=True)
        xo = xo * lax.rsqrt(ms + NORM_EPS) * fg_ref[...]
    o_ref[...] = xo.reshape(nb, ts, D)


def _combine(ys, wc, h, x1, extra, gate2, sg, su, sd, final_g, B, T, final_norm):
    N, D = x1.shape
    ts = MOE_TILE_SEQ
    nt = T // ts
    L = N_EXPERTS * MOE_CAP
    cols = np.arange(L) // MOE_CAP
    rep = jnp.asarray((np.arange(LANES)[:, None] == cols[None, :]).astype(np.float32)).astype(BF16)
    tile = lambda i: (0, i, 0, 0)
    const = lambda i: (0, 0)
    tok = lambda a: a.reshape(B, nt, ts, a.shape[-1])
    extras = [] if extra is None else [tok(extra)]
    out = pl.pallas_call(
        functools.partial(_combine_kernel, final_norm=final_norm),
        grid=(nt,),
        in_specs=[
            pl.BlockSpec((None, L, D), lambda i: (i, 0, 0)),
            pl.BlockSpec((B, None, ts, LANES), tile),
            pl.BlockSpec((LANES, L), const),
            pl.BlockSpec((B, None, ts, D), tile),
            pl.BlockSpec((B, None, ts, D), tile),
            pl.BlockSpec((B, 1, D), lambda i: (0, 0, 0)),
            pl.BlockSpec(sg.shape, const),
            pl.BlockSpec(su.shape, const),
            pl.BlockSpec(sd.shape, const),
            pl.BlockSpec((1, D), const),
        ] + [pl.BlockSpec((B, None, ts, D), tile) for _ in extras],
        out_specs=pl.BlockSpec((B, None, ts, D), tile),
        out_shape=jax.ShapeDtypeStruct((B, nt, ts, D), F32),
        compiler_params=_cparams(("arbitrary",)),
        name="moe_combine",
    )(ys, tok(wc), rep, tok(h), tok(x1), gate2.reshape(B, 1, D), sg, su, sd,
      final_g.reshape(1, D), *extras)
    return out.reshape(N, D)


def _overflow_kernel(h_ref, wc_ref, wg_ref, wu_ref, wd_ref, o_ref):
    e = pl.program_id(1)

    @pl.when(e == 0)
    def _():
        o_ref[...] = jnp.zeros_like(o_ref)

    h = h_ref[...]
    lane = lax.broadcasted_iota(jnp.int32, wc_ref.shape, 1)
    w = jnp.sum(jnp.where(lane == e, wc_ref[...], 0.0), axis=1, keepdims=True)
    hid = _silu(jnp.dot(h, wg_ref[0], preferred_element_type=F32)) * jnp.dot(
        h, wu_ref[0], preferred_element_type=F32)
    o_ref[...] += _dot(hid * w, wd_ref[0])


def _overflow(h, wc_over, wg, wu, wd):
    N, D = h.shape
    tm = math.gcd(N, 1024)
    E, _, Hd = wg.shape
    row = lambda i, e: (i, 0)
    return pl.pallas_call(
        _overflow_kernel,
        grid=(N // tm, E),
        in_specs=[
            pl.BlockSpec((tm, D), row),
            pl.BlockSpec((tm, LANES), row),
            pl.BlockSpec((1, D, Hd), lambda i, e: (e, 0, 0)),
            pl.BlockSpec((1, D, Hd), lambda i, e: (e, 0, 0)),
            pl.BlockSpec((1, Hd, D), lambda i, e: (e, 0, 0)),
        ],
        out_specs=pl.BlockSpec((tm, D), row),
        out_shape=jax.ShapeDtypeStruct((N, D), F32),
        compiler_params=_cparams(("arbitrary", "arbitrary")),
        name="moe_overflow",
    )(h, wc_over, wg, wu, wd)


def _moe(h, wc, wg, wu, wd, layer, sg, su, sd, x1, gate2, final_g, B, T, final_norm):
    xs, wc_over, cnt = _dispatch(h, wc, B, T)
    ys = _experts(xs, cnt, wg, wu, wd, layer)
    def finish(extra):
        return _combine(ys, wc, h, x1, extra, gate2, sg, su, sd, final_g, B, T, final_norm)

    def with_overflow():
        return finish(_overflow(h, wc_over, wg[layer].astype(BF16), wu[layer].astype(BF16),
                                wd[layer].astype(BF16)))

    return lax.cond(jnp.any(wc_over != 0.0), with_overflow, lambda: finish(None))


def _split_w_in(w_in):
    W = WIDTH
    half = HEAD_DIM // 2
    j = np.arange(W)
    swap = (j // HEAD_DIM) * HEAD_DIM + (j % HEAD_DIM + half) % HEAD_DIM
    w_rwkv = w_in[:, :RWKV_COLS]
    w_ret = w_in[:, RWKV_COLS:RWKV_COLS + RET_COLS]
    w_pool = w_in[:, RWKV_COLS + RET_COLS:]
    q_sw = w_ret[:, 0:W][:, swap]
    k_sw = w_ret[:, W:2 * W][:, swap]
    w_ret_ext = jnp.concatenate([w_ret, q_sw, k_sw], axis=1)
    return w_rwkv.astype(BF16), w_ret_ext.astype(BF16), w_pool.astype(BF16)


def kernel(x, c, norm1_g, norm2_g, w_ada, b_ada, w_in, w_out, rwkv_mu, rwkv_w_up, rwkv_w0,
           rwkv_a_up, rwkv_a0, rwkv_g_up, rwkv_k_k, rwkv_k_a, rwkv_r_k, rwkv_lnx_w, rwkv_lnx_b,
           pool_w, pool_scale, w_router, router_bias, we_gate, we_up, we_down,
           ws_gate, ws_up, ws_down, final_g):
    B, T, D = x.shape
    L = w_in.shape[0]
    N = B * T
    mod = _adaln(c, w_ada, b_ada)
    x2 = x.reshape(N, D)
    for l in range(L):
        shift1, scale1, gate1, shift2, scale2, gate2 = jnp.split(mod[l], 6, axis=-1)
        w1, w2, w3 = _split_w_in(w_in[l])
        p_rwkv, p_ret, p_pool = _inproj(x2, norm1_g[l], scale1, shift1, w1, w2, w3, T)
        y_rwkv = _rwkv(p_rwkv.reshape(B, T, -1), rwkv_mu[l], rwkv_w_up[l], rwkv_w0[l],
                       rwkv_a_up[l], rwkv_a0[l], rwkv_g_up[l], rwkv_k_k[l], rwkv_k_a[l],
                       rwkv_r_k[l], rwkv_lnx_w[l], rwkv_lnx_b[l])
        y_ret = _retention(p_ret.reshape(B, T, -1))
        y_pool = _pool(p_pool.reshape(B, T, -1), pool_w[l], pool_scale[l])
        x1, h, wc = _outproj(y_rwkv.reshape(N, -1), y_ret.reshape(N, -1), y_pool.reshape(N, -1),
                             x2, w_out[l], gate1, norm2_g[l], scale2, shift2,
                             w_router[l], router_bias[l], T)
        x2 = _moe(h, wc, we_gate, we_up, we_down, l,
                  ws_gate[l].astype(BF16), ws_up[l].astype(BF16),
                  ws_down[l].astype(BF16), x1, gate2, final_g, B, T,
                  final_norm=(l == L - 1))
    return x2.reshape(B, T, D)
```

```python
import functools
import math

import numpy as np
import jax
import jax.numpy as jnp
from jax import lax
from jax.experimental import pallas as pl
from jax.experimental.pallas import tpu as pltpu

F32 = jnp.float32
BF16 = jnp.bfloat16
HIGHEST = lax.Precision.HIGHEST

D_MODEL = 1024
HEADS = 6
HEAD_DIM = 64
WIDTH = HEADS * HEAD_DIM
W_LORA = 64
A_LORA = 64
G_LORA = 128
RWKV_COLS = 3 * WIDTH + W_LORA + A_LORA + G_LORA
RET_COLS = 4 * WIDTH
RET_CHUNK = 128
RWKV_CHUNK = 128
RWKV_SUB = 16
RET_ROWS = 4
RWKV_ROWS = 4
POOL_GROUPS = 4
POOL_GROUP_DIM = 64
POOL_WIDTH = POOL_GROUPS * POOL_GROUP_DIM
POOL_WINDOWS = (2, 4, 8, 16)
N_EXPERTS = 64
TOP_K = 8
N_GROUPS = 8
TOPK_GROUPS = 4
EXPERT_HIDDEN = 256
ROUTED_SCALE = 2.5
NORM_EPS = 1e-6
RWKV_LNX_EPS = 64e-5
RET_NORM_EPS = 1e-6
ROPE_BASE = 10000.0
LANES = 128
MOE_TILE_SEQ = 16
MOE_CAP = 80
MOE_USED_STEPS = (48, 64)
VMEM_LIMIT = 48 * 1024 * 1024


def _cparams(sem):
    return pltpu.CompilerParams(dimension_semantics=sem, vmem_limit_bytes=VMEM_LIMIT)


def _dot(a, b):
    return jnp.dot(a.astype(BF16), b.astype(BF16), preferred_element_type=F32)


def _dot_nt(a, b):
    return lax.dot_general(a.astype(BF16), b.astype(BF16), (((1,), (1,)), ((), ())),
                           preferred_element_type=F32)


def _dot_tn(a, b):
    return lax.dot_general(a.astype(BF16), b.astype(BF16), (((0,), (0,)), ((), ())),
                           preferred_element_type=F32)


def _dot_f32(a, b):
    return jnp.dot(a, b, preferred_element_type=F32, precision=HIGHEST)


def _sigmoid(x):
    return 1.0 / (1.0 + jnp.exp(-x))


def _silu(x):
    return x * _sigmoid(x)


def _head_masks(rows, width):
    lane = lax.broadcasted_iota(jnp.int32, (rows, width), 1)
    return [lane // HEAD_DIM == h for h in range(width // HEAD_DIM)]


def _stack_heads(x, masks):
    return jnp.concatenate([jnp.where(m, x, 0.0) for m in masks], axis=0).astype(BF16)


def _select_heads(stacked, masks, c):
    out = stacked[0:c]
    for h in range(1, len(masks)):
        out = jnp.where(masks[h], stacked[h * c:(h + 1) * c], out)
    return out


def _adaln_kernel(c_ref, w_ref, b_ref, o_ref):
    o_ref[0] = _dot_f32(_silu(c_ref[...]), w_ref[0]) + b_ref[0]


def _adaln(c, w_ada, b_ada):
    L, D, M = w_ada.shape
    B = c.shape[0]
    tn = 1536
    return pl.pallas_call(
        _adaln_kernel,
        grid=(L, M // tn),
        in_specs=[
            pl.BlockSpec((B, D), lambda l, j: (0, 0)),
            pl.BlockSpec((1, D, tn), lambda l, j: (l, 0, j)),
            pl.BlockSpec((1, 1, tn), lambda l, j: (l, 0, j)),
        ],
        out_specs=pl.BlockSpec((1, B, tn), lambda l, j: (l, 0, j)),
        out_shape=jax.ShapeDtypeStruct((L, B, M), F32),
        compiler_params=_cparams(("arbitrary", "arbitrary")),
        name="adaln",
    )(c, w_ada, b_ada.reshape(L, 1, M))


def _modulated_norm(x, g, scale, shift):
    ms = jnp.mean(x * x, axis=-1, keepdims=True)
    return x * lax.rsqrt(ms + NORM_EPS) * g * (1.0 + scale) + shift


def _inproj_kernel(x_ref, g_ref, sc_ref, sh_ref, w1_ref, w2_ref, w3_ref, o1_ref, o2_ref, o3_ref):
    h = _modulated_norm(x_ref[...], g_ref[...], sc_ref[0], sh_ref[0]).astype(BF16)
    o1_ref[...] = jnp.dot(h, w1_ref[...], preferred_element_type=F32)
    o2_ref[...] = jnp.dot(h, w2_ref[...], preferred_element_type=F32)
    o3_ref[...] = jnp.dot(h, w3_ref[...], preferred_element_type=F32)


def _inproj(x2, g, scale, shift, w1, w2, w3, T):
    N, D = x2.shape
    B = N // T
    tm = min(512, T)
    per_b = T // tm
    row = lambda i: (i, 0)
    const = lambda i: (0, 0)
    bvec = lambda i: (i // per_b, 0, 0)
    return pl.pallas_call(
        _inproj_kernel,
        grid=(N // tm,),
        in_specs=[
            pl.BlockSpec((tm, D), row),
            pl.BlockSpec((1, D), const),
            pl.BlockSpec((1, 1, D), bvec),
            pl.BlockSpec((1, 1, D), bvec),
            pl.BlockSpec(w1.shape, const),
            pl.BlockSpec(w2.shape, const),
            pl.BlockSpec(w3.shape, const),
        ],
        out_specs=[
            pl.BlockSpec((tm, w1.shape[1]), row),
            pl.BlockSpec((tm, w2.shape[1]), row),
            pl.BlockSpec((tm, w3.shape[1]), row),
        ],
        out_shape=[
            jax.ShapeDtypeStruct((N, w1.shape[1]), F32),
            jax.ShapeDtypeStruct((N, w2.shape[1]), F32),
            jax.ShapeDtypeStruct((N, w3.shape[1]), F32),
        ],
        compiler_params=_cparams(("arbitrary",)),
        name="inproj",
    )(x2, g.reshape(1, D), scale.reshape(B, 1, D), shift.reshape(B, 1, D), w1, w2, w3)


def _unit_lower_inverse(a3):
    H, C, _ = a3.shape
    ri = lax.broadcasted_iota(jnp.int32, (H, C, C), 1)
    ci = lax.broadcasted_iota(jnp.int32, (H, C, C), 2)
    eye = (ri == ci).astype(F32)
    same = (ri // RWKV_SUB) == (ci // RWKV_SUB)
    dm = jnp.where(same, a3, 0.0)
    off = jnp.where(same, 0.0, a3)

    def bmm(x, y):
        return jnp.einsum('hij,hjk->hik', x.astype(BF16), y.astype(BF16),
                          preferred_element_type=F32)

    d2 = bmm(dm, dm)
    d4 = bmm(d2, d2)
    d8 = bmm(d4, d4)
    x = eye - dm
    x = x + bmm(x, d2)
    x = x + bmm(x, d4)
    x = x + bmm(x, d8)
    n = bmm(x, off)
    y = eye - n
    power = bmm(n, n)
    order = 2
    while order < C // RWKV_SUB:
        y = y + bmm(y, power)
        order *= 2
        if order < C // RWKV_SUB:
            power = bmm(power, power)
    return bmm(y, x)


def _rwkv_prepare(p, carry, prm):
    (mu, wup, w0, aup, a0, gup, k_k, k_a, r_k, lnw, lnb, hsum) = prm
    C = RWKV_CHUNK
    W = WIDTH
    H = HEADS
    row = lax.broadcasted_iota(jnp.int32, (C, 1), 0)
    prev = jnp.where(row == 0, carry, pltpu.roll(p, 1, 0))
    xs = p + (prev - p) * mu

    r = xs[:, 0:W]
    k = xs[:, W:2 * W]
    v = xs[:, 2 * W:3 * W]
    xwa = xs[:, 3 * W:3 * W + W_LORA + A_LORA]
    xg = xs[:, 3 * W + W_LORA + A_LORA:]

    z = w0 + _dot(jnp.tanh(xwa), wup)
    log_w = -math.exp(-0.5) * _sigmoid(z)
    a = _sigmoid(a0 + _dot(xwa, aup))
    g = _dot(_sigmoid(xg), gup)
    kk = k * k_k
    kk = kk / jnp.maximum(jnp.sqrt(_dot(kk * kk, hsum)), 1e-12)
    k = k * (1.0 + (a - 1.0) * k_a)

    ti = lax.broadcasted_iota(jnp.int32, (C, C), 0)
    si = lax.broadcasted_iota(jnp.int32, (C, C), 1)
    tri = jnp.where(ti >= si, 1.0, 0.0).astype(BF16)
    log_w_hi = log_w.astype(BF16)
    cum = (jnp.dot(tri, log_w_hi, preferred_element_type=F32)
           + _dot(tri, log_w - log_w_hi.astype(F32)))
    g_in = jnp.exp(cum)
    g_inv = jnp.exp(-cum)
    g_end = g_in[C - 1:C, :]
    kt = kk * jnp.exp(cum - log_w)
    bt = kk * a * g_inv
    kq = k * g_inv
    rt = r * g_in

    masks = _head_masks(C, W)
    t3 = lax.broadcasted_iota(jnp.int32, (H, C, C), 1)
    s3 = lax.broadcasted_iota(jnp.int32, (H, C, C), 2)
    a_ab = jnp.where(t3 > s3, _dot_nt(_stack_heads(kt, masks), bt).reshape(H, C, C), 0.0)
    bt_rows = _stack_heads(bt, masks)
    kq_rows = _stack_heads(kq, masks)
    t1 = lax.broadcasted_iota(jnp.int32, (C, H * C), 0)
    s1 = lax.broadcasted_iota(jnp.int32, (C, H * C), 1) % C
    a_ak = jnp.where(t1 > s1, _dot_nt(kt, kq_rows), 0.0)
    t2 = lax.broadcasted_iota(jnp.int32, (C, 2 * H * C), 0)
    s2 = lax.broadcasted_iota(jnp.int32, (C, 2 * H * C), 1) % C
    p_r = jnp.where(t2 >= s2, _dot_nt(rt, jnp.concatenate([bt_rows, kq_rows], axis=0)), 0.0)
    return a_ab, (a_ak, p_r, kt, rt, bt, kq, g_end, r, k, v, g)


def _rwkv_finish(vals, t_inv, s0, prm):
    (a_ak, p_r, kt, rt, bt, kq, g_end, r, k, v, g) = vals
    (mu, wup, w0, aup, a0, gup, k_k, k_a, r_k, lnw, lnb, hsum) = prm
    C = RWKV_CHUNK
    W = WIDTH
    H = HEADS
    masks = _head_masks(C, W)
    ks = _dot_nt(kt, s0)
    rs = _dot_nt(rt, s0)
    v_rows = _stack_heads(v, masks)
    av = _dot(a_ak, v_rows)
    u = _select_heads(_dot(t_inv.reshape(H * C, C), -(ks + av)), masks, C)
    y = rs + _dot(p_r, jnp.concatenate([_stack_heads(u, masks), v_rows], axis=0))
    upd = _dot_tn(jnp.concatenate([u, v], axis=0),
                  jnp.concatenate([bt * g_end, kq * g_end], axis=0))
    hi = lax.broadcasted_iota(jnp.int32, (W, W), 0) // HEAD_DIM
    hj = lax.broadcasted_iota(jnp.int32, (W, W), 1) // HEAD_DIM
    s_new = s0 * g_end + jnp.where(hi == hj, upd, 0.0)

    inv_d = 1.0 / HEAD_DIM
    mean = _dot(y, hsum) * inv_d
    yc = y - mean
    var = _dot(yc * yc, hsum) * inv_d
    yn = yc * lax.rsqrt(var + RWKV_LNX_EPS) * lnw + lnb
    bonus = _dot(r * k * r_k, hsum) * v
    return (yn + bonus) * g, s_new


def _rwkv_kernel(p_ref, mu_ref, wup_ref, w0_ref, aup_ref, a0_ref, gup_ref, kk_ref, ka_ref,
                 rk_ref, lnw_ref, lnb_ref, hsum_ref, o_ref, carry_ref, s_ref):
    @pl.when(pl.program_id(1) == 0)
    def _():
        carry_ref[...] = jnp.zeros_like(carry_ref)
        s_ref[...] = jnp.zeros_like(s_ref)

    prm = tuple(ref[...] for ref in (mu_ref, wup_ref, w0_ref, aup_ref, a0_ref, gup_ref, kk_ref,
                                     ka_ref, rk_ref, lnw_ref, lnb_ref, hsum_ref))
    G = p_ref.shape[0]
    H = HEADS
    a_abs, vals = [], []
    for i in range(G):
        p = p_ref[i]
        a_ab, val = _rwkv_prepare(p, carry_ref[i], prm)
        carry_ref[i] = p[RWKV_CHUNK - 1:RWKV_CHUNK, :]
        a_abs.append(a_ab)
        vals.append(val)
    t_inv = _unit_lower_inverse(jnp.concatenate(a_abs, axis=0))
    for i in range(G):
        out, s_new = _rwkv_finish(vals[i], t_inv[i * H:(i + 1) * H], s_ref[i], prm)
        s_ref[i] = s_new
        o_ref[i] = out


def _rwkv(p, mu, wup, w0, aup, a0, gup, k_k, k_a, r_k, lnx_w, lnx_b):
    B, T, _ = p.shape
    C = RWKV_CHUNK
    W = WIDTH
    G = RWKV_ROWS if B % RWKV_ROWS == 0 else 1
    lora_in = W_LORA + A_LORA
    wup_pad = jnp.zeros((lora_in, W), F32).at[:W_LORA].set(wup).astype(BF16)
    aup_pad = jnp.zeros((lora_in, W), F32).at[W_LORA:].set(aup).astype(BF16)
    head = np.arange(W) // HEAD_DIM
    hsum = jnp.asarray((head[:, None] == head[None, :]).astype(np.float32)).astype(BF16)
    vec = lambda a: a.reshape(1, -1)
    const = lambda b, t: (0, 0)
    params = [vec(mu), wup_pad, vec(w0), aup_pad, vec(a0), gup.astype(BF16), vec(k_k), vec(k_a),
              vec(r_k), vec(lnx_w), vec(lnx_b), hsum]
    return pl.pallas_call(
        _rwkv_kernel,
        grid=(B // G, T // C),
        in_specs=[pl.BlockSpec((G, C, RWKV_COLS), lambda b, t: (b, t, 0))]
        + [pl.BlockSpec(a.shape, const) for a in params],
        out_specs=pl.BlockSpec((G, C, W), lambda b, t: (b, t, 0)),
        out_shape=jax.ShapeDtypeStruct((B, T, W), F32),
        scratch_shapes=[pltpu.VMEM((G, 1, RWKV_COLS), F32), pltpu.VMEM((G, W, W), F32)],
        compiler_params=_cparams(("arbitrary", "arbitrary")),
        name="rwkv7",
    )(p, *params)


def _retention_kernel(p_ref, cos_ref, sin_ref, dec_ref, xi_ref, zeta_ref, cd_ref, hsum_ref,
                      o_ref, s_ref):
    C = RET_CHUNK
    W = WIDTH

    @pl.when(pl.program_id(1) == 0)
    def _():
        s_ref[...] = jnp.zeros_like(s_ref)

    cos = cos_ref[...]
    sin = sin_ref[...]
    masks = _head_masks(C, W)
    hi = lax.broadcasted_iota(jnp.int32, (W, W), 0) // HEAD_DIM
    hj = lax.broadcasted_iota(jnp.int32, (W, W), 1) // HEAD_DIM
    first_half = lax.broadcasted_iota(jnp.int32, (C, W), 1) % HEAD_DIM < HEAD_DIM // 2

    def swap_halves(x):
        return jnp.where(first_half, pltpu.roll(x, W - HEAD_DIM // 2, 1),
                         pltpu.roll(x, HEAD_DIM // 2, 1))

    for i in range(p_ref.shape[0]):
        p = p_ref[i]
        q = p[:, 0:W] * cos + swap_halves(p[:, 0:W]) * sin
        k = (p[:, W:2 * W] * cos + swap_halves(p[:, W:2 * W]) * sin) * (HEAD_DIM ** -0.5)
        v = p[:, 2 * W:3 * W]
        gate = p[:, 3 * W:4 * W]
        scores = _dot_nt(q, _stack_heads(k, masks)) * dec_ref[...]
        y = _dot(scores, _stack_heads(v, masks))
        s0 = s_ref[i]
        y = y + _dot(q * xi_ref[...], s0)
        kv = _dot_tn(k * zeta_ref[...], v)
        s_ref[i] = s0 * cd_ref[...] + jnp.where(hi == hj, kv, 0.0)
        ms = _dot(y * y, hsum_ref[...]) * (1.0 / HEAD_DIM)
        o_ref[i] = _silu(gate) * (y * lax.rsqrt(ms + RET_NORM_EPS))


def _retention_tables(T):
    C, H, d = RET_CHUNK, HEADS, HEAD_DIM
    pos = jnp.arange(T, dtype=F32)
    inv_freq = ROPE_BASE ** (-jnp.arange(0, d, 2, dtype=F32) / d)
    ang = pos[:, None] * inv_freq[None, :]
    cos = jnp.cos(ang)
    sin = jnp.sin(ang)
    cos_full = jnp.tile(jnp.concatenate([cos, cos], -1), (1, H))
    sin_full = jnp.tile(jnp.concatenate([-sin, sin], -1), (1, H))
    log_gamma = jnp.log1p(-(2.0 ** (-5.0 - jnp.arange(H, dtype=F32))))
    idx = jnp.arange(C, dtype=F32)
    diff = idx[:, None] - idx[None, :]
    dec = jnp.where(diff >= 0, jnp.exp(log_gamma[:, None, None] * jnp.maximum(diff, 0.0)), 0.0)
    xi = jnp.exp(log_gamma[:, None] * (idx + 1.0))
    zeta = jnp.exp(log_gamma[:, None] * (C - 1.0 - idx))
    cd = jnp.exp(log_gamma * C)
    per_lane = lambda a: jnp.repeat(a.T, d, axis=1)
    return (cos_full, sin_full, dec.transpose(1, 0, 2).reshape(C, H * C), per_lane(xi), per_lane(zeta),
            jnp.repeat(cd, d).reshape(1, H * d))


def _retention(p):
    B, T, cols = p.shape
    C = RET_CHUNK
    W = WIDTH
    cos, sin, dec, xi, zeta, cd = _retention_tables(T)
    head = np.arange(W) // HEAD_DIM
    hsum = jnp.asarray((head[:, None] == head[None, :]).astype(np.float32)).astype(BF16)
    const = lambda b, t: (0, 0)
    G = RET_ROWS if B % RET_ROWS == 0 else 1
    return pl.pallas_call(
        _retention_kernel,
        grid=(B // G, T // C),
        in_specs=[
            pl.BlockSpec((G, C, cols), lambda b, t: (b, t, 0)),
            pl.BlockSpec((C, W), lambda b, t: (t, 0)),
            pl.BlockSpec((C, W), lambda b, t: (t, 0)),
            pl.BlockSpec(dec.shape, const),
            pl.BlockSpec(xi.shape, const),
            pl.BlockSpec(zeta.shape, const),
            pl.BlockSpec(cd.shape, const),
            pl.BlockSpec(hsum.shape, const),
        ],
        out_specs=pl.BlockSpec((G, C, W), lambda b, t: (b, t, 0)),
        out_shape=jax.ShapeDtypeStruct((B, T, W), F32),
        scratch_shapes=[pltpu.VMEM((G, W, W), F32)],
        compiler_params=_cparams(("arbitrary", "arbitrary")),
        name="retention",
    )(p, cos, sin, dec, xi, zeta, cd, hsum)


def _pool_kernel(u_ref, w_ref, scale_ref, o_ref):
    u = u_ref[0]
    T = u.shape[0]
    row = lax.broadcasted_iota(jnp.int32, (T, 1), 0)

    def lag(x, k):
        return jnp.where(row >= k, pltpu.roll(x, k, 0), 0.0)

    s2 = u + lag(u, 1)
    s4 = s2 + lag(s2, 2)
    s8 = s4 + lag(s4, 4)
    s16 = s8 + lag(s8, 8)
    grp = lax.broadcasted_iota(jnp.int32, (1, POOL_WIDTH), 1) // POOL_GROUP_DIM
    s = jnp.where(grp == 0, s2, jnp.where(grp == 1, s4, jnp.where(grp == 2, s8, s16)))
    win = jnp.where(grp == 0, POOL_WINDOWS[0],
                    jnp.where(grp == 1, POOL_WINDOWS[1],
                              jnp.where(grp == 2, POOL_WINDOWS[2], POOL_WINDOWS[3])))
    count = jnp.minimum(row + 1, win).astype(F32)
    pooled = s / count - u
    o_ref[0] = _dot(pooled, w_ref[...]) * scale_ref[...]


def _pool(u, pool_w, pool_scale):
    B, T, Wp = u.shape
    G, d = POOL_GROUPS, POOL_GROUP_DIM
    wbd = jnp.zeros((Wp, Wp), F32)
    for gi in range(G):
        wbd = wbd.at[gi * d:(gi + 1) * d, gi * d:(gi + 1) * d].set(pool_w[gi])
    return pl.pallas_call(
        _pool_kernel,
        grid=(B,),
        in_specs=[
            pl.BlockSpec((1, T, Wp), lambda b: (b, 0, 0)),
            pl.BlockSpec((Wp, Wp), lambda b: (0, 0)),
            pl.BlockSpec((1, Wp), lambda b: (0, 0)),
        ],
        out_specs=pl.BlockSpec((1, T, Wp), lambda b: (b, 0, 0)),
        out_shape=jax.ShapeDtypeStruct((B, T, Wp), F32),
        compiler_params=_cparams(("arbitrary",)),
        name="pool",
    )(u, wbd.astype(BF16), pool_scale.reshape(1, Wp))


def _route(logits_t, bias_col):
    E, tm = logits_t.shape
    per_group = E // N_GROUPS
    neg_inf = -jnp.inf
    scores = _sigmoid(logits_t)
    choice = scores + bias_col
    c3 = choice.reshape(N_GROUPS, per_group, tm)
    sub = lax.broadcasted_iota(jnp.int32, c3.shape, 1)
    m1 = jnp.max(c3, axis=1, keepdims=True)
    first = jnp.min(jnp.where(c3 == m1, sub, per_group), axis=1, keepdims=True)
    m2 = jnp.max(jnp.where(sub == first, neg_inf, c3), axis=1, keepdims=True)
    gs = m1 + m2
    gidx = lax.broadcasted_iota(jnp.int32, gs.shape, 0)
    grank = jnp.zeros(gs.shape, jnp.int32)
    for j in range(N_GROUPS):
        other = gs[j:j + 1]
        ahead = jnp.where(other > gs, 1, jnp.where((other == gs) & (gidx > j), 1, 0))
        grank = grank + ahead
    gmask = jnp.broadcast_to(grank < TOPK_GROUPS, c3.shape)
    masked = jnp.where(gmask, c3, neg_inf).reshape(E, tm)
    eidx = lax.broadcasted_iota(jnp.int32, (E, tm), 0)
    top = jnp.zeros((E, tm), F32)
    for _ in range(TOP_K):
        best = jnp.max(masked, axis=0, keepdims=True)
        first = jnp.min(jnp.where(masked == best, eidx, E), axis=0, keepdims=True)
        hit = eidx == first
        top = jnp.where(hit, scores, top)
        masked = jnp.where(hit, neg_inf, masked)
    return top / jnp.sum(top, axis=0, keepdims=True) * ROUTED_SCALE


def _outproj_kernel(yr_ref, yt_ref, yp_ref, x_ref, w1_ref, w2_ref, w3_ref, g1_ref, ng_ref,
                    sc_ref, sh_ref, wr_ref, rb_ref, x1_ref, h_ref, wc_ref):
    mixed = (_dot(yr_ref[...], w1_ref[...]) + _dot(yt_ref[...], w2_ref[...])
             + _dot(yp_ref[...], w3_ref[...]))
    x1 = x_ref[...] + g1_ref[0] * mixed
    x1_ref[...] = x1
    h = _modulated_norm(x1, ng_ref[...], sc_ref[0], sh_ref[0])
    h_ref[...] = h.astype(BF16)
    logits_t = lax.dot_general(wr_ref[...], h, (((1,), (1,)), ((), ())),
                               preferred_element_type=F32, precision=HIGHEST)
    wc_t = _route(logits_t, rb_ref[...])
    pad = jnp.zeros((LANES - N_EXPERTS, wc_t.shape[1]), F32)
    wc_ref[...] = jnp.concatenate([wc_t, pad], axis=0).T


def _outproj(yr, yt, yp, x2, w_out, gate1, ng, scale2, shift2, w_router, router_bias, T):
    N, D = x2.shape
    B = N // T
    tm = min(512, T)
    per_b = T // tm
    W = WIDTH
    w1 = w_out[:W].astype(BF16)
    w2 = w_out[W:2 * W].astype(BF16)
    w3 = w_out[2 * W:].astype(BF16)
    row = lambda i: (i, 0)
    const = lambda i: (0, 0)
    bvec = lambda i: (i // per_b, 0, 0)
    return pl.pallas_call(
        _outproj_kernel,
        grid=(N // tm,),
        in_specs=[
            pl.BlockSpec((tm, W), row),
            pl.BlockSpec((tm, W), row),
            pl.BlockSpec((tm, POOL_WIDTH), row),
            pl.BlockSpec((tm, D), row),
            pl.BlockSpec(w1.shape, const),
            pl.BlockSpec(w2.shape, const),
            pl.BlockSpec(w3.shape, const),
            pl.BlockSpec((1, 1, D), bvec),
            pl.BlockSpec((1, D), const),
            pl.BlockSpec((1, 1, D), bvec),
            pl.BlockSpec((1, 1, D), bvec),
            pl.BlockSpec((N_EXPERTS, D), const),
            pl.BlockSpec((N_EXPERTS, 1), const),
        ],
        out_specs=[
            pl.BlockSpec((tm, D), row),
            pl.BlockSpec((tm, D), row),
            pl.BlockSpec((tm, LANES), row),
        ],
        out_shape=[
            jax.ShapeDtypeStruct((N, D), F32),
            jax.ShapeDtypeStruct((N, D), BF16),
            jax.ShapeDtypeStruct((N, LANES), F32),
        ],
        compiler_params=_cparams(("arbitrary",)),
        name="outproj_router",
    )(yr, yt, yp, x2, w1, w2, w3, gate1.reshape(B, 1, D), ng.reshape(1, D),
      scale2.reshape(B, 1, D), shift2.reshape(B, 1, D), w_router.T,
      router_bias.reshape(N_EXPERTS, 1))


def _slot_positions(wc):
    tm = wc.shape[0]
    sel = jnp.where(wc > 0.0, 1.0, 0.0).astype(BF16)
    ti = lax.broadcasted_iota(jnp.int32, (tm, tm), 0)
    si = lax.broadcasted_iota(jnp.int32, (tm, tm), 1)
    earlier = jnp.where(ti > si, 1.0, 0.0).astype(BF16)
    return sel, jnp.dot(earlier, sel, preferred_element_type=F32)


def _dispatch_kernel(h_ref, wc_ref, rept_ref, xs_ref, over_ref, cnt_ref):
    nb, ts, D = h_ref.shape
    tm = nb * ts
    wc = wc_ref[...].reshape(tm, LANES)
    sel, pos = _slot_positions(wc)
    over_ref[...] = jnp.where(pos >= MOE_CAP, wc, 0.0).reshape(nb, ts, LANES)
    cnt_ref[...] = jnp.sum(sel.astype(F32), axis=0, keepdims=True)
    rept = rept_ref[...]
    code = jnp.where(wc > 0.0, pos, -1.0)
    code_rows = _dot_nt(rept, code)
    slot = (lax.broadcasted_iota(jnp.int32, (rept.shape[0], 1), 0) % MOE_CAP).astype(F32)
    onehot = jnp.where(code_rows == slot, 1.0, 0.0).astype(BF16)
    xs_ref[...] = jnp.dot(onehot, h_ref[...].reshape(tm, D),
                          preferred_element_type=F32).astype(BF16)


def _dispatch(h, wc, B, T):
    N, D = h.shape
    ts = MOE_TILE_SEQ
    nt = T // ts
    L = N_EXPERTS * MOE_CAP
    rows = np.arange(L) // MOE_CAP
    rept = jnp.asarray((rows[:, None] == np.arange(LANES)[None, :]).astype(np.float32)).astype(BF16)
    tile = lambda i: (0, i, 0, 0)
    xs, over, cnt = pl.pallas_call(
        _dispatch_kernel,
        grid=(nt,),
        in_specs=[
            pl.BlockSpec((B, None, ts, D), tile),
            pl.BlockSpec((B, None, ts, LANES), tile),
            pl.BlockSpec((L, LANES), lambda i: (0, 0)),
        ],
        out_specs=[
            pl.BlockSpec((None, L, D), lambda i: (i, 0, 0)),
            pl.BlockSpec((B, None, ts, LANES), tile),
            pl.BlockSpec((None, 1, LANES), lambda i: (i, 0, 0)),
        ],
        out_shape=[
            jax.ShapeDtypeStruct((nt, L, D), BF16),
            jax.ShapeDtypeStruct((B, nt, ts, LANES), F32),
            jax.ShapeDtypeStruct((nt, 1, LANES), F32),
        ],
        compiler_params=_cparams(("arbitrary",)),
        name="moe_dispatch",
    )(h.reshape(B, nt, ts, D), wc.reshape(B, nt, ts, LANES), rept)
    return xs, over.reshape(N, LANES), cnt.reshape(nt, LANES)[:, :N_EXPERTS]


def _experts_kernel(used_ref, x_ref, wg_ref, wu_ref, wd_ref, o_ref, wg_s, wu_s, wd_s):
    @pl.when(pl.program_id(1) == 0)
    def _():
        wg_s[...] = wg_ref[0, 0].astype(BF16)
        wu_s[...] = wu_ref[0, 0].astype(BF16)
        wd_s[...] = wd_ref[0, 0].astype(BF16)

    tg, cap, D = x_ref.shape
    used = used_ref[pl.program_id(0), pl.program_id(1)]

    def run(n):
        x = x_ref[:, 0:n, :].reshape(tg * n, D)
        hid = _silu(jnp.dot(x, wg_s[...], preferred_element_type=F32)) * jnp.dot(
            x, wu_s[...], preferred_element_type=F32)
        o_ref[:, 0:n, :] = _dot(hid, wd_s[...]).astype(BF16).reshape(tg, n, D)
        if n < cap:
            o_ref[:, n:cap, :] = jnp.zeros((tg, cap - n, D), BF16)

    steps = [n for n in MOE_USED_STEPS if n < cap] + [cap]
    lo = 0
    for n in steps:
        pl.when((used > lo) & (used <= n) if n < cap else used > lo)(functools.partial(run, n))
        lo = n

    @pl.when(used == 0)
    def _():
        o_ref[...] = jnp.zeros_like(o_ref)


def _experts(xs, cnt, wg, wu, wd, layer):
    nt, L, D = xs.shape
    _, E, _, Hd = wg.shape
    cap = L // E
    tg = math.gcd(nt, 16)
    used = jnp.minimum(jnp.max(cnt.reshape(nt // tg, tg, E), axis=1), cap).T.astype(jnp.int32)
    slots = pl.BlockSpec((tg, None, cap, D), lambda e, g, u: (g, e, 0, 0))
    grid_spec = pltpu.PrefetchScalarGridSpec(
        num_scalar_prefetch=1,
        grid=(E, nt // tg),
        in_specs=[
            slots,
            pl.BlockSpec((1, 1, D, Hd), lambda e, g, u: (layer, e, 0, 0)),
            pl.BlockSpec((1, 1, D, Hd), lambda e, g, u: (layer, e, 0, 0)),
            pl.BlockSpec((1, 1, Hd, D), lambda e, g, u: (layer, e, 0, 0)),
        ],
        out_specs=slots,
        scratch_shapes=[pltpu.VMEM((D, Hd), BF16), pltpu.VMEM((D, Hd), BF16),
                        pltpu.VMEM((Hd, D), BF16)],
    )
    return pl.pallas_call(
        _experts_kernel,
        grid_spec=grid_spec,
        out_shape=jax.ShapeDtypeStruct((nt, E, cap, D), BF16),
        compiler_params=_cparams(("arbitrary", "arbitrary")),
        name="moe_experts",
    )(used, xs.reshape(nt, E, cap, D), wg, wu, wd).reshape(nt, L, D)


def _combine_kernel(y_ref, wc_ref, rep_ref, h_ref, x1_ref, g2_ref, sg_ref, su_ref, sd_ref, fg_ref,
                    *rest, final_norm):
    ex_ref = rest[0] if len(rest) == 2 else None
    o_ref = rest[-1]
    nb, ts, D = h_ref.shape
    tm = nb * ts
    wc = wc_ref[...].reshape(tm, LANES)
    _, pos = _slot_positions(wc)
    rep = rep_ref[...]
    pos_lanes = _dot(pos, rep)
    w_lanes = _dot(wc, rep)
    slot = (lax.broadcasted_iota(jnp.int32, (1, rep.shape[1]), 1) % MOE_CAP).astype(F32)
    weighted = jnp.where(pos_lanes == slot, w_lanes, 0.0).astype(BF16)
    routed = jnp.dot(weighted, y_ref[...], preferred_element_type=F32)
    h = h_ref[...].reshape(tm, D)
    hid = _silu(jnp.dot(h, sg_ref[...], preferred_element_type=F32)) * jnp.dot(
        h, su_ref[...], preferred_element_type=F32)
    y = routed + _dot(hid, sd_ref[...])
    if ex_ref is not None:
        y = y + ex_ref[...].reshape(tm, D)
    gate = jnp.broadcast_to(g2_ref[...], (nb, ts, D)).reshape(tm, D)
    xo = x1_ref[...].reshape(tm, D) + gate * y
    if final_norm:
        ms = jnp.mean(xo * xo, axis=-1, keepdims=True)
        xo = xo * lax.rsqrt(ms + NORM_EPS) * fg_ref[...]
    o_ref[...] = xo.reshape(nb, ts, D)


def _combine(ys, wc, h, x1, extra, gate2, sg, su, sd, final_g, B, T, final_norm):
    N, D = x1.shape
    ts = MOE_TILE_SEQ
    nt = T // ts
    L = N_EXPERTS * MOE_CAP
    cols = np.arange(L) // MOE_CAP
    rep = jnp.asarray((np.arange(LANES)[:, None] == cols[None, :]).astype(np.float32)).astype(BF16)
    tile = lambda i: (0, i, 0, 0)
    const = lambda i: (0, 0)
    tok = lambda a: a.reshape(B, nt, ts, a.shape[-1])
    extras = [] if extra is None else [tok(extra)]
    out = pl.pallas_call(
        functools.partial(_combine_kernel, final_norm=final_norm),
        grid=(nt,),
        in_specs=[
            pl.BlockSpec((None, L, D), lambda i: (i, 0, 0)),
            pl.BlockSpec((B, None, ts, LANES), tile),
            pl.BlockSpec((LANES, L), const),
            pl.BlockSpec((B, None, ts, D), tile),
            pl.BlockSpec((B, None, ts, D), tile),
            pl.BlockSpec((B, 1, D), lambda i: (0, 0, 0)),
            pl.BlockSpec(sg.shape, const),
            pl.BlockSpec(su.shape, const),
            pl.BlockSpec(sd.shape, const),
            pl.BlockSpec((1, D), const),
        ] + [pl.BlockSpec((B, None, ts, D), tile) for _ in extras],
        out_specs=pl.BlockSpec((B, None, ts, D), tile),
        out_shape=jax.ShapeDtypeStruct((B, nt, ts, D), F32),
        compiler_params=_cparams(("arbitrary",)),
        name="moe_combine",
    )(ys, tok(wc), rep, tok(h), tok(x1), gate2.reshape(B, 1, D), sg, su, sd,
      final_g.reshape(1, D), *extras)
    return out.reshape(N, D)


def _overflow_kernel(h_ref, wc_ref, wg_ref, wu_ref, wd_ref, o_ref):
    e = pl.program_id(1)

    @pl.when(e == 0)
    def _():
        o_ref[...] = jnp.zeros_like(o_ref)

    h = h_ref[...]
    lane = lax.broadcasted_iota(jnp.int32, wc_ref.shape, 1)
    w = jnp.sum(jnp.where(lane == e, wc_ref[...], 0.0), axis=1, keepdims=True)
    hid = _silu(jnp.dot(h, wg_ref[0], preferred_element_type=F32)) * jnp.dot(
        h, wu_ref[0], preferred_element_type=F32)
    o_ref[...] += _dot(hid * w, wd_ref[0])


def _overflow(h, wc_over, wg, wu, wd):
    N, D = h.shape
    tm = math.gcd(N, 1024)
    E, _, Hd = wg.shape
    row = lambda i, e: (i, 0)
    return pl.pallas_call(
        _overflow_kernel,
        grid=(N // tm, E),
        in_specs=[
            pl.BlockSpec((tm, D), row),
            pl.BlockSpec((tm, LANES), row),
            pl.BlockSpec((1, D, Hd), lambda i, e: (e, 0, 0)),
            pl.BlockSpec((1, D, Hd), lambda i, e: (e, 0, 0)),
            pl.BlockSpec((1, Hd, D), lambda i, e: (e, 0, 0)),
        ],
        out_specs=pl.BlockSpec((tm, D), row),
        out_shape=jax.ShapeDtypeStruct((N, D), F32),
        compiler_params=_cparams(("arbitrary", "arbitrary")),
        name="moe_overflow",
    )(h, wc_over, wg, wu, wd)


def _moe(h, wc, wg, wu, wd, layer, sg, su, sd, x1, gate2, final_g, B, T, final_norm):
    xs, wc_over, cnt = _dispatch(h, wc, B, T)
    ys = _experts(xs, cnt, wg, wu, wd, layer)
    def finish(extra):
        return _combine(ys, wc, h, x1, extra, gate2, sg, su, sd, final_g, B, T, final_norm)

    def with_overflow():
        return finish(_overflow(h, wc_over, wg[layer].astype(BF16), wu[layer].astype(BF16),
                                wd[layer].astype(BF16)))

    return lax.cond(jnp.any(wc_over != 0.0), with_overflow, lambda: finish(None))


def _split_w_in(w_in):
    w_rwkv = w_in[:, :RWKV_COLS]
    w_ret = w_in[:, RWKV_COLS:RWKV_COLS + RET_COLS]
    w_pool = w_in[:, RWKV_COLS + RET_COLS:]
    return w_rwkv.astype(BF16), w_ret.astype(BF16), w_pool.astype(BF16)


def kernel(x, c, norm1_g, norm2_g, w_ada, b_ada, w_in, w_out, rwkv_mu, rwkv_w_up, rwkv_w0,
           rwkv_a_up, rwkv_a0, rwkv_g_up, rwkv_k_k, rwkv_k_a, rwkv_r_k, rwkv_lnx_w, rwkv_lnx_b,
           pool_w, pool_scale, w_router, router_bias, we_gate, we_up, we_down,
           ws_gate, ws_up, ws_down, final_g):
    B, T, D = x.shape
    L = w_in.shape[0]
    N = B * T
    mod = _adaln(c, w_ada, b_ada)
    x2 = x.reshape(N, D)
    for l in range(L):
        shift1, scale1, gate1, shift2, scale2, gate2 = jnp.split(mod[l], 6, axis=-1)
        w1, w2, w3 = _split_w_in(w_in[l])
        p_rwkv, p_ret, p_pool = _inproj(x2, norm1_g[l], scale1, shift1, w1, w2, w3, T)
        y_rwkv = _rwkv(p_rwkv.reshape(B, T, -1), rwkv_mu[l], rwkv_w_up[l], rwkv_w0[l],
                       rwkv_a_up[l], rwkv_a0[l], rwkv_g_up[l], rwkv_k_k[l], rwkv_k_a[l],
                       rwkv_r_k[l], rwkv_lnx_w[l], rwkv_lnx_b[l])
        y_ret = _retention(p_ret.reshape(B, T, -1))
        y_pool = _pool(p_pool.reshape(B, T, -1), pool_w[l], pool_scale[l])
        x1, h, wc = _outproj(y_rwkv.reshape(N, -1), y_ret.reshape(N, -1), y_pool.reshape(N, -1),
                             x2, w_out[l], gate1, norm2_g[l], scale2, shift2,
                             w_router[l], router_bias[l], T)
        x2 = _moe(h, wc, we_gate, we_up, we_down, l,
                  ws_gate[l].astype(BF16), ws_up[l].astype(BF16),
                  ws_down[l].astype(BF16), x1, gate2, final_g, B, T,
                  final_norm=(l == L - 1))
    return x2.reshape(B, T, D)
```

```python
import functools
import math

import numpy as np
import jax
import jax.numpy as jnp
from jax import lax
from jax.experimental import pallas as pl
from jax.experimental.pallas import tpu as pltpu

F32 = jnp.float32
BF16 = jnp.bfloat16
HIGHEST = lax.Precision.HIGHEST

D_MODEL = 1024
HEADS = 6
HEAD_DIM = 64
WIDTH = HEADS * HEAD_DIM
W_LORA = 64
A_LORA = 64
G_LORA = 128
RWKV_COLS = 3 * WIDTH + W_LORA + A_LORA + G_LORA
RET_COLS = 4 * WIDTH
RET_CHUNK = 128
RWKV_CHUNK = 128
RWKV_SUB = 16
RET_ROWS = 4
RWKV_ROWS = 4
POOL_GROUPS = 4
POOL_GROUP_DIM = 64
POOL_WIDTH = POOL_GROUPS * POOL_GROUP_DIM
POOL_WINDOWS = (2, 4, 8, 16)
N_EXPERTS = 64
TOP_K = 8
N_GROUPS = 8
TOPK_GROUPS = 4
EXPERT_HIDDEN = 256
ROUTED_SCALE = 2.5
NORM_EPS = 1e-6
RWKV_LNX_EPS = 64e-5
RET_NORM_EPS = 1e-6
ROPE_BASE = 10000.0
LANES = 128
MOE_TILE_SEQ = 16
MOE_CAP = 80
MOE_USED_STEPS = (48, 64)
VMEM_LIMIT = 48 * 1024 * 1024


def _cparams(sem):
    return pltpu.CompilerParams(dimension_semantics=sem, vmem_limit_bytes=VMEM_LIMIT)


def _dot(a, b):
    return jnp.dot(a.astype(BF16), b.astype(BF16), preferred_element_type=F32)


def _dot_nt(a, b):
    return lax.dot_general(a.astype(BF16), b.astype(BF16), (((1,), (1,)), ((), ())),
                           preferred_element_type=F32)


def _dot_tn(a, b):
    return lax.dot_general(a.astype(BF16), b.astype(BF16), (((0,), (0,)), ((), ())),
                           preferred_element_type=F32)


def _dot_f32(a, b):
    return jnp.dot(a, b, preferred_element_type=F32, precision=HIGHEST)


def _sigmoid(x):
    return 1.0 / (1.0 + jnp.exp(-x))


def _silu(x):
    return x * _sigmoid(x)


def _head_masks(rows, width):
    lane = lax.broadcasted_iota(jnp.int32, (rows, width), 1)
    return [lane // HEAD_DIM == h for h in range(width // HEAD_DIM)]


def _stack_heads(x, masks):
    return jnp.concatenate([jnp.where(m, x, 0.0) for m in masks], axis=0).astype(BF16)


def _select_heads(stacked, masks, c):
    out = stacked[0:c]
    for h in range(1, len(masks)):
        out = jnp.where(masks[h], stacked[h * c:(h + 1) * c], out)
    return out


def _adaln_kernel(c_ref, w_ref, b_ref, o_ref):
    o_ref[0] = _dot_f32(_silu(c_ref[...]), w_ref[0]) + b_ref[0]


def _adaln(c, w_ada, b_ada):
    L, D, M = w_ada.shape
    B = c.shape[0]
    tn = 1536
    return pl.pallas_call(
        _adaln_kernel,
        grid=(L, M // tn),
        in_specs=[
            pl.BlockSpec((B, D), lambda l, j: (0, 0)),
            pl.BlockSpec((1, D, tn), lambda l, j: (l, 0, j)),
            pl.BlockSpec((1, 1, tn), lambda l, j: (l, 0, j)),
        ],
        out_specs=pl.BlockSpec((1, B, tn), lambda l, j: (l, 0, j)),
        out_shape=jax.ShapeDtypeStruct((L, B, M), F32),
        compiler_params=_cparams(("arbitrary", "arbitrary")),
        name="adaln",
    )(c, w_ada, b_ada.reshape(L, 1, M))


def _modulated_norm(x, g, scale, shift):
    ms = jnp.mean(x * x, axis=-1, keepdims=True)
    return x * lax.rsqrt(ms + NORM_EPS) * g * (1.0 + scale) + shift


def _inproj_kernel(x_ref, g_ref, sc_ref, sh_ref, w1_ref, w2_ref, w3_ref, o1_ref, o2_ref, o3_ref):
    h = _modulated_norm(x_ref[...], g_ref[...], sc_ref[0], sh_ref[0]).astype(BF16)
    o1_ref[...] = jnp.dot(h, w1_ref[...], preferred_element_type=F32)
    o2_ref[...] = jnp.dot(h, w2_ref[...], preferred_element_type=F32)
    o3_ref[...] = jnp.dot(h, w3_ref[...], preferred_element_type=F32)


def _inproj(x2, g, scale, shift, w1, w2, w3, T):
    N, D = x2.shape
    B = N // T
    tm = min(512, T)
    per_b = T // tm
    row = lambda i: (i, 0)
    const = lambda i: (0, 0)
    bvec = lambda i: (i // per_b, 0, 0)
    return pl.pallas_call(
        _inproj_kernel,
        grid=(N // tm,),
        in_specs=[
            pl.BlockSpec((tm, D), row),
            pl.BlockSpec((1, D), const),
            pl.BlockSpec((1, 1, D), bvec),
            pl.BlockSpec((1, 1, D), bvec),
            pl.BlockSpec(w1.shape, const),
            pl.BlockSpec(w2.shape, const),
            pl.BlockSpec(w3.shape, const),
        ],
        out_specs=[
            pl.BlockSpec((tm, w1.shape[1]), row),
            pl.BlockSpec((tm, w2.shape[1]), row),
            pl.BlockSpec((tm, w3.shape[1]), row),
        ],
        out_shape=[
            jax.ShapeDtypeStruct((N, w1.shape[1]), F32),
            jax.ShapeDtypeStruct((N, w2.shape[1]), F32),
            jax.ShapeDtypeStruct((N, w3.shape[1]), F32),
        ],
        compiler_params=_cparams(("arbitrary",)),
        name="inproj",
    )(x2, g.reshape(1, D), scale.reshape(B, 1, D), shift.reshape(B, 1, D), w1, w2, w3)


def _unit_lower_inverse(a3):
    H, C, _ = a3.shape
    ri = lax.broadcasted_iota(jnp.int32, (H, C, C), 1)
    ci = lax.broadcasted_iota(jnp.int32, (H, C, C), 2)
    eye = (ri == ci).astype(F32)
    same = (ri // RWKV_SUB) == (ci // RWKV_SUB)
    dm = jnp.where(same, a3, 0.0)
    off = jnp.where(same, 0.0, a3)

    def bmm(x, y):
        return jnp.einsum('hij,hjk->hik', x.astype(BF16), y.astype(BF16),
                          preferred_element_type=F32)

    d2 = bmm(dm, dm)
    d4 = bmm(d2, d2)
    d8 = bmm(d4, d4)
    x = eye - dm
    x = x + bmm(x, d2)
    x = x + bmm(x, d4)
    x = x + bmm(x, d8)
    n = bmm(x, off)
    y = eye - n
    power = bmm(n, n)
    order = 2
    while order < C // RWKV_SUB:
        y = y + bmm(y, power)
        order *= 2
        if order < C // RWKV_SUB:
            power = bmm(power, power)
    return bmm(y, x)


def _rwkv_prepare(p, carry, prm):
    (mu, wup, w0, aup, a0, gup, k_k, k_a, r_k, lnw, lnb, hsum) = prm
    C = RWKV_CHUNK
    W = WIDTH
    H = HEADS
    row = lax.broadcasted_iota(jnp.int32, (C, 1), 0)
    prev = jnp.where(row == 0, carry, pltpu.roll(p, 1, 0))
    xs = p + (prev - p) * mu

    r = xs[:, 0:W]
    k = xs[:, W:2 * W]
    v = xs[:, 2 * W:3 * W]
    xwa = xs[:, 3 * W:3 * W + W_LORA + A_LORA]
    xg = xs[:, 3 * W + W_LORA + A_LORA:]

    z = w0 + _dot(jnp.tanh(xwa), wup)
    log_w = -math.exp(-0.5) * _sigmoid(z)
    a = _sigmoid(a0 + _dot(xwa, aup))
    g = _dot(_sigmoid(xg), gup)
    kk = k * k_k
    kk = kk / jnp.maximum(jnp.sqrt(_dot(kk * kk, hsum)), 1e-12)
    k = k * (1.0 + (a - 1.0) * k_a)

    ti = lax.broadcasted_iota(jnp.int32, (C, C), 0)
    si = lax.broadcasted_iota(jnp.int32, (C, C), 1)
    tri = jnp.where(ti >= si, 1.0, 0.0).astype(BF16)
    log_w_hi = log_w.astype(BF16)
    cum = (jnp.dot(tri, log_w_hi, preferred_element_type=F32)
           + _dot(tri, log_w - log_w_hi.astype(F32)))
    g_in = jnp.exp(cum)
    g_inv = jnp.exp(-cum)
    g_end = g_in[C - 1:C, :]
    kt = kk * jnp.exp(cum - log_w)
    bt = kk * a * g_inv
    kq = k * g_inv
    rt = r * g_in

    masks = _head_masks(C, W)
    t3 = lax.broadcasted_iota(jnp.int32, (H, C, C), 1)
    s3 = lax.broadcasted_iota(jnp.int32, (H, C, C), 2)
    a_ab = jnp.where(t3 > s3, _dot_nt(_stack_heads(kt, masks), bt).reshape(H, C, C), 0.0)
    bt_rows = _stack_heads(bt, masks)
    kq_rows = _stack_heads(kq, masks)
    t1 = lax.broadcasted_iota(jnp.int32, (C, H * C), 0)
    s1 = lax.broadcasted_iota(jnp.int32, (C, H * C), 1) % C
    a_ak = jnp.where(t1 > s1, _dot_nt(kt, kq_rows), 0.0)
    t2 = lax.broadcasted_iota(jnp.int32, (C, 2 * H * C), 0)
    s2 = lax.broadcasted_iota(jnp.int32, (C, 2 * H * C), 1) % C
    p_r = jnp.where(t2 >= s2, _dot_nt(rt, jnp.concatenate([bt_rows, kq_rows], axis=0)), 0.0)
    return a_ab, (a_ak, p_r, kt, rt, bt, kq, g_end, r, k, v, g)


def _rwkv_finish(vals, t_inv, s0, prm):
    (a_ak, p_r, kt, rt, bt, kq, g_end, r, k, v, g) = vals
    (mu, wup, w0, aup, a0, gup, k_k, k_a, r_k, lnw, lnb, hsum) = prm
    C = RWKV_CHUNK
    W = WIDTH
    H = HEADS
    masks = _head_masks(C, W)
    ks = _dot_nt(kt, s0)
    rs = _dot_nt(rt, s0)
    v_rows = _stack_heads(v, masks)
    av = _dot(a_ak, v_rows)
    u = _select_heads(_dot(t_inv.reshape(H * C, C), -(ks + av)), masks, C)
    y = rs + _dot(p_r, jnp.concatenate([_stack_heads(u, masks), v_rows], axis=0))
    upd = _dot_tn(jnp.concatenate([u, v], axis=0),
                  jnp.concatenate([bt * g_end, kq * g_end], axis=0))
    hi = lax.broadcasted_iota(jnp.int32, (W, W), 0) // HEAD_DIM
    hj = lax.broadcasted_iota(jnp.int32, (W, W), 1) // HEAD_DIM
    s_new = s0 * g_end + jnp.where(hi == hj, upd, 0.0)

    inv_d = 1.0 / HEAD_DIM
    mean = _dot(y, hsum) * inv_d
    yc = y - mean
    var = _dot(yc * yc, hsum) * inv_d
    yn = yc * lax.rsqrt(var + RWKV_LNX_EPS) * lnw + lnb
    bonus = _dot(r * k * r_k, hsum) * v
    return (yn + bonus) * g, s_new


def _rwkv_kernel(p_ref, mu_ref, wup_ref, w0_ref, aup_ref, a0_ref, gup_ref, kk_ref, ka_ref,
                 rk_ref, lnw_ref, lnb_ref, hsum_ref, o_ref, carry_ref, s_ref):
    @pl.when(pl.program_id(1) == 0)
    def _():
        carry_ref[...] = jnp.zeros_like(carry_ref)
        s_ref[...] = jnp.zeros_like(s_ref)

    prm = tuple(ref[...] for ref in (mu_ref, wup_ref, w0_ref, aup_ref, a0_ref, gup_ref, kk_ref,
                                     ka_ref, rk_ref, lnw_ref, lnb_ref, hsum_ref))
    G = p_ref.shape[0]
    H = HEADS
    a_abs, vals = [], []
    for i in range(G):
        p = p_ref[i]
        a_ab, val = _rwkv_prepare(p, carry_ref[i], prm)
        carry_ref[i] = p[RWKV_CHUNK - 1:RWKV_CHUNK, :]
        a_abs.append(a_ab)
        vals.append(val)
    t_inv = _unit_lower_inverse(jnp.concatenate(a_abs, axis=0))
    for i in range(G):
        out, s_new = _rwkv_finish(vals[i], t_inv[i * H:(i + 1) * H], s_ref[i], prm)
        s_ref[i] = s_new
        o_ref[i] = out


def _rwkv(p, mu, wup, w0, aup, a0, gup, k_k, k_a, r_k, lnx_w, lnx_b):
    B, T, _ = p.shape
    C = RWKV_CHUNK
    W = WIDTH
    G = RWKV_ROWS if B % RWKV_ROWS == 0 else 1
    lora_in = W_LORA + A_LORA
    wup_pad = jnp.zeros((lora_in, W), F32).at[:W_LORA].set(wup).astype(BF16)
    aup_pad = jnp.zeros((lora_in, W), F32).at[W_LORA:].set(aup).astype(BF16)
    head = np.arange(W) // HEAD_DIM
    hsum = jnp.asarray((head[:, None] == head[None, :]).astype(np.float32)).astype(BF16)
    vec = lambda a: a.reshape(1, -1)
    const = lambda b, t: (0, 0)
    params = [vec(mu), wup_pad, vec(w0), aup_pad, vec(a0), gup.astype(BF16), vec(k_k), vec(k_a),
              vec(r_k), vec(lnx_w), vec(lnx_b), hsum]
    return pl.pallas_call(
        _rwkv_kernel,
        grid=(B // G, T // C),
        in_specs=[pl.BlockSpec((G, C, RWKV_COLS), lambda b, t: (b, t, 0))]
        + [pl.BlockSpec(a.shape, const) for a in params],
        out_specs=pl.BlockSpec((G, C, W), lambda b, t: (b, t, 0)),
        out_shape=jax.ShapeDtypeStruct((B, T, W), F32),
        scratch_shapes=[pltpu.VMEM((G, 1, RWKV_COLS), F32), pltpu.VMEM((G, W, W), F32)],
        compiler_params=_cparams(("arbitrary", "arbitrary")),
        name="rwkv7",
    )(p, *params)


def _retention_kernel(p_ref, cos_ref, sin_ref, dec_ref, xi_ref, zeta_ref, cd_ref, hsum_ref,
                      o_ref, s_ref):
    C = RET_CHUNK
    W = WIDTH

    @pl.when(pl.program_id(1) == 0)
    def _():
        s_ref[...] = jnp.zeros_like(s_ref)

    cos = cos_ref[...]
    sin = sin_ref[...]
    masks = _head_masks(C, W)
    hi = lax.broadcasted_iota(jnp.int32, (W, W), 0) // HEAD_DIM
    hj = lax.broadcasted_iota(jnp.int32, (W, W), 1) // HEAD_DIM
    first_half = lax.broadcasted_iota(jnp.int32, (C, W), 1) % HEAD_DIM < HEAD_DIM // 2

    def swap_halves(x):
        return jnp.where(first_half, pltpu.roll(x, W - HEAD_DIM // 2, 1),
                         pltpu.roll(x, HEAD_DIM // 2, 1))

    for i in range(p_ref.shape[0]):
        p = p_ref[i]
        q = p[:, 0:W] * cos + swap_halves(p[:, 0:W]) * sin
        k = (p[:, W:2 * W] * cos + swap_halves(p[:, W:2 * W]) * sin) * (HEAD_DIM ** -0.5)
        v = p[:, 2 * W:3 * W]
        gate = p[:, 3 * W:4 * W]
        scores = _dot_nt(q, _stack_heads(k, masks)) * dec_ref[...]
        y = _dot(scores, _stack_heads(v, masks))
        s0 = s_ref[i]
        y = y + _dot(q * xi_ref[...], s0)
        kv = _dot_tn(k * zeta_ref[...], v)
        s_ref[i] = s0 * cd_ref[...] + jnp.where(hi == hj, kv, 0.0)
        ms = _dot(y * y, hsum_ref[...]) * (1.0 / HEAD_DIM)
        o_ref[i] = _silu(gate) * (y * lax.rsqrt(ms + RET_NORM_EPS))


def _retention_tables(T):
    C, H, d = RET_CHUNK, HEADS, HEAD_DIM
    pos = jnp.arange(T, dtype=F32)
    inv_freq = ROPE_BASE ** (-jnp.arange(0, d, 2, dtype=F32) / d)
    ang = pos[:, None] * inv_freq[None, :]
    cos = jnp.cos(ang)
    sin = jnp.sin(ang)
    cos_full = jnp.tile(jnp.concatenate([cos, cos], -1), (1, H))
    sin_full = jnp.tile(jnp.concatenate([-sin, sin], -1), (1, H))
    log_gamma = jnp.log1p(-(2.0 ** (-5.0 - jnp.arange(H, dtype=F32))))
    idx = jnp.arange(C, dtype=F32)
    diff = idx[:, None] - idx[None, :]
    dec = jnp.where(diff >= 0, jnp.exp(log_gamma[:, None, None] * jnp.maximum(diff, 0.0)), 0.0)
    xi = jnp.exp(log_gamma[:, None] * (idx + 1.0))
    zeta = jnp.exp(log_gamma[:, None] * (C - 1.0 - idx))
    cd = jnp.exp(log_gamma * C)
    per_lane = lambda a: jnp.repeat(a.T, d, axis=1)
    return (cos_full, sin_full, dec.transpose(1, 0, 2).reshape(C, H * C), per_lane(xi), per_lane(zeta),
            jnp.repeat(cd, d).reshape(1, H * d))


def _retention(p):
    B, T, cols = p.shape
    C = RET_CHUNK
    W = WIDTH
    cos, sin, dec, xi, zeta, cd = _retention_tables(T)
    head = np.arange(W) // HEAD_DIM
    hsum = jnp.asarray((head[:, None] == head[None, :]).astype(np.float32)).astype(BF16)
    const = lambda b, t: (0, 0)
    G = RET_ROWS if B % RET_ROWS == 0 else 1
    return pl.pallas_call(
        _retention_kernel,
        grid=(B // G, T // C),
        in_specs=[
            pl.BlockSpec((G, C, cols), lambda b, t: (b, t, 0)),
            pl.BlockSpec((C, W), lambda b, t: (t, 0)),
            pl.BlockSpec((C, W), lambda b, t: (t, 0)),
            pl.BlockSpec(dec.shape, const),
            pl.BlockSpec(xi.shape, const),
            pl.BlockSpec(zeta.shape, const),
            pl.BlockSpec(cd.shape, const),
            pl.BlockSpec(hsum.shape, const),
        ],
        out_specs=pl.BlockSpec((G, C, W), lambda b, t: (b, t, 0)),
        out_shape=jax.ShapeDtypeStruct((B, T, W), F32),
        scratch_shapes=[pltpu.VMEM((G, W, W), F32)],
        compiler_params=_cparams(("arbitrary", "arbitrary")),
        name="retention",
    )(p, cos, sin, dec, xi, zeta, cd, hsum)


def _pool_kernel(u_ref, w_ref, scale_ref, o_ref):
    u = u_ref[0]
    T = u.shape[0]
    row = lax.broadcasted_iota(jnp.int32, (T, 1), 0)

    def lag(x, k):
        return jnp.where(row >= k, pltpu.roll(x, k, 0), 0.0)

    s2 = u + lag(u, 1)
    s4 = s2 + lag(s2, 2)
    s8 = s4 + lag(s4, 4)
    s16 = s8 + lag(s8, 8)
    grp = lax.broadcasted_iota(jnp.int32, (1, POOL_WIDTH), 1) // POOL_GROUP_DIM
    s = jnp.where(grp == 0, s2, jnp.where(grp == 1, s4, jnp.where(grp == 2, s8, s16)))
    win = jnp.where(grp == 0, POOL_WINDOWS[0],
                    jnp.where(grp == 1, POOL_WINDOWS[1],
                              jnp.where(grp == 2, POOL_WINDOWS[2], POOL_WINDOWS[3])))
    count = jnp.minimum(row + 1, win).astype(F32)
    pooled = s / count - u
    o_ref[0] = _dot(pooled, w_ref[...]) * scale_ref[...]


def _pool(u, pool_w, pool_scale):
    B, T, Wp = u.shape
    G, d = POOL_GROUPS, POOL_GROUP_DIM
    wbd = jnp.zeros((Wp, Wp), F32)
    for gi in range(G):
        wbd = wbd.at[gi * d:(gi + 1) * d, gi * d:(gi + 1) * d].set(pool_w[gi])
    return pl.pallas_call(
        _pool_kernel,
        grid=(B,),
        in_specs=[
            pl.BlockSpec((1, T, Wp), lambda b: (b, 0, 0)),
            pl.BlockSpec((Wp, Wp), lambda b: (0, 0)),
            pl.BlockSpec((1, Wp), lambda b: (0, 0)),
        ],
        out_specs=pl.BlockSpec((1, T, Wp), lambda b: (b, 0, 0)),
        out_shape=jax.ShapeDtypeStruct((B, T, Wp), F32),
        compiler_params=_cparams(("arbitrary",)),
        name="pool",
    )(u, wbd.astype(BF16), pool_scale.reshape(1, Wp))


def _route(logits_t, bias_col):
    E, tm = logits_t.shape
    per_group = E // N_GROUPS
    neg_inf = -jnp.inf
    scores = _sigmoid(logits_t)
    choice = scores + bias_col
    c3 = choice.reshape(N_GROUPS, per_group, tm)
    sub = lax.broadcasted_iota(jnp.int32, c3.shape, 1)
    m1 = jnp.max(c3, axis=1, keepdims=True)
    first = jnp.min(jnp.where(c3 == m1, sub, per_group), axis=1, keepdims=True)
    m2 = jnp.max(jnp.where(sub == first, neg_inf, c3), axis=1, keepdims=True)
    gs = m1 + m2
    gidx = lax.broadcasted_iota(jnp.int32, gs.shape, 0)
    grank = jnp.zeros(gs.shape, jnp.int32)
    for j in range(N_GROUPS):
        other = gs[j:j + 1]
        ahead = jnp.where(other > gs, 1, jnp.where((other == gs) & (gidx > j), 1, 0))
        grank = grank + ahead
    gmask = jnp.broadcast_to(grank < TOPK_GROUPS, c3.shape)
    masked = jnp.where(gmask, c3, neg_inf).reshape(E, tm)
    eidx = lax.broadcasted_iota(jnp.int32, (E, tm), 0)
    top = jnp.zeros((E, tm), F32)
    for _ in range(TOP_K):
        best = jnp.max(masked, axis=0, keepdims=True)
        first = jnp.min(jnp.where(masked == best, eidx, E), axis=0, keepdims=True)
        hit = eidx == first
        top = jnp.where(hit, scores, top)
        masked = jnp.where(hit, neg_inf, masked)
    return top / jnp.sum(top, axis=0, keepdims=True) * ROUTED_SCALE


def _outproj_kernel(yr_ref, yt_ref, yp_ref, x_ref, w1_ref, w2_ref, w3_ref, g1_ref, ng_ref,
                    sc_ref, sh_ref, wr_ref, rb_ref, x1_ref, h_ref, wc_ref):
    mixed = (_dot(yr_ref[...], w1_ref[...]) + _dot(yt_ref[...], w2_ref[...])
             + _dot(yp_ref[...], w3_ref[...]))
    x1 = x_ref[...] + g1_ref[0] * mixed
    x1_ref[...] = x1
    h = _modulated_norm(x1, ng_ref[...], sc_ref[0], sh_ref[0])
    h_ref[...] = h.astype(BF16)
    logits_t = lax.dot_general(wr_ref[...], h, (((1,), (1,)), ((), ())),
                               preferred_element_type=F32, precision=HIGHEST)
    wc_t = _route(logits_t, rb_ref[...])
    pad = jnp.zeros((LANES - N_EXPERTS, wc_t.shape[1]), F32)
    wc_ref[...] = jnp.concatenate([wc_t, pad], axis=0).T


def _outproj(yr, yt, yp, x2, w_out, gate1, ng, scale2, shift2, w_router, router_bias, T):
    N, D = x2.shape
    B = N // T
    tm = min(512, T)
    per_b = T // tm
    W = WIDTH
    w1 = w_out[:W].astype(BF16)
    w2 = w_out[W:2 * W].astype(BF16)
    w3 = w_out[2 * W:].astype(BF16)
    row = lambda i: (i, 0)
    const = lambda i: (0, 0)
    bvec = lambda i: (i // per_b, 0, 0)
    return pl.pallas_call(
        _outproj_kernel,
        grid=(N // tm,),
        in_specs=[
            pl.BlockSpec((tm, W), row),
            pl.BlockSpec((tm, W), row),
            pl.BlockSpec((tm, POOL_WIDTH), row),
            pl.BlockSpec((tm, D), row),
            pl.BlockSpec(w1.shape, const),
            pl.BlockSpec(w2.shape, const),
            pl.BlockSpec(w3.shape, const),
            pl.BlockSpec((1, 1, D), bvec),
            pl.BlockSpec((1, D), const),
            pl.BlockSpec((1, 1, D), bvec),
            pl.BlockSpec((1, 1, D), bvec),
            pl.BlockSpec((N_EXPERTS, D), const),
            pl.BlockSpec((N_EXPERTS, 1), const),
        ],
        out_specs=[
            pl.BlockSpec((tm, D), row),
            pl.BlockSpec((tm, D), row),
            pl.BlockSpec((tm, LANES), row),
        ],
        out_shape=[
            jax.ShapeDtypeStruct((N, D), F32),
            jax.ShapeDtypeStruct((N, D), BF16),
            jax.ShapeDtypeStruct((N, LANES), F32),
        ],
        compiler_params=_cparams(("arbitrary",)),
        name="outproj_router",
    )(yr, yt, yp, x2, w1, w2, w3, gate1.reshape(B, 1, D), ng.reshape(1, D),
      scale2.reshape(B, 1, D), shift2.reshape(B, 1, D), w_router.T,
      router_bias.reshape(N_EXPERTS, 1))


def _slot_positions(wc):
    tm = wc.shape[0]
    sel = jnp.where(wc > 0.0, 1.0, 0.0).astype(BF16)
    ti = lax.broadcasted_iota(jnp.int32, (tm, tm), 0)
    si = lax.broadcasted_iota(jnp.int32, (tm, tm), 1)
    earlier = jnp.where(ti > si, 1.0, 0.0).astype(BF16)
    return sel, jnp.dot(earlier, sel, preferred_element_type=F32)


def _dispatch_kernel(h_ref, wc_ref, rept_ref, xs_ref, over_ref, cnt_ref):
    nb, ts, D = h_ref.shape
    tm = nb * ts
    wc = wc_ref[...].reshape(tm, LANES)
    sel, pos = _slot_positions(wc)
    over_ref[...] = jnp.where(pos >= MOE_CAP, wc, 0.0).reshape(nb, ts, LANES)
    cnt_ref[...] = jnp.sum(sel.astype(F32), axis=0, keepdims=True)
    rept = rept_ref[...]
    code = jnp.where(wc > 0.0, pos, -1.0)
    code_rows = _dot_nt(rept, code)
    slot = (lax.broadcasted_iota(jnp.int32, (rept.shape[0], 1), 0) % MOE_CAP).astype(F32)
    onehot = jnp.where(code_rows == slot, 1.0, 0.0).astype(BF16)
    xs_ref[...] = jnp.dot(onehot, h_ref[...].reshape(tm, D),
                          preferred_element_type=F32).astype(BF16)


def _dispatch(h, wc, B, T):
    N, D = h.shape
    ts = MOE_TILE_SEQ
    nt = T // ts
    L = N_EXPERTS * MOE_CAP
    rows = np.arange(L) // MOE_CAP
    rept = jnp.asarray((rows[:, None] == np.arange(LANES)[None, :]).astype(np.float32)).astype(BF16)
    tile = lambda i: (0, i, 0, 0)
    xs, over, cnt = pl.pallas_call(
        _dispatch_kernel,
        grid=(nt,),
        in_specs=[
            pl.BlockSpec((B, None, ts, D), tile),
            pl.BlockSpec((B, None, ts, LANES), tile),
            pl.BlockSpec((L, LANES), lambda i: (0, 0)),
        ],
        out_specs=[
            pl.BlockSpec((None, L, D), lambda i: (i, 0, 0)),
            pl.BlockSpec((B, None, ts, LANES), tile),
            pl.BlockSpec((None, 1, LANES), lambda i: (i, 0, 0)),
        ],
        out_shape=[
            jax.ShapeDtypeStruct((nt, L, D), BF16),
            jax.ShapeDtypeStruct((B, nt, ts, LANES), F32),
            jax.ShapeDtypeStruct((nt, 1, LANES), F32),
        ],
        compiler_params=_cparams(("arbitrary",)),
        name="moe_dispatch",
    )(h.reshape(B, nt, ts, D), wc.reshape(B, nt, ts, LANES), rept)
    return xs, over.reshape(N, LANES), cnt.reshape(nt, LANES)[:, :N_EXPERTS]


def _slot_pieces(cap):
    bounds = [0] + [n for n in MOE_USED_STEPS if n < cap] + [cap]
    return list(zip(bounds[:-1], bounds[1:]))


def _experts_kernel(used_ref, *refs):
    pieces = _slot_pieces(refs[-4].shape[1])
    x_refs = refs[:len(pieces)]
    wg_ref, wu_ref, wd_ref, o_ref, wg_s, wu_s, wd_s = refs[len(pieces):]

    @pl.when(pl.program_id(1) == 0)
    def _():
        wg_s[...] = wg_ref[0, 0].astype(BF16)
        wu_s[...] = wu_ref[0, 0].astype(BF16)
        wd_s[...] = wd_ref[0, 0].astype(BF16)

    tg, cap, D = o_ref.shape
    used = used_ref[pl.program_id(0), pl.program_id(1)]

    def run(k):
        x = jnp.concatenate([x_refs[j][...].reshape(tg * (hi - lo), D)
                             for j, (lo, hi) in enumerate(pieces[:k + 1])], axis=0)
        hid = _silu(jnp.dot(x, wg_s[...], preferred_element_type=F32)) * jnp.dot(
            x, wu_s[...], preferred_element_type=F32)
        y = _dot(hid, wd_s[...]).astype(BF16)
        for lo, hi in pieces[:k + 1]:
            o_ref[:, lo:hi, :] = y[tg * lo:tg * hi].reshape(tg, hi - lo, D)
        if pieces[k][1] < cap:
            o_ref[:, pieces[k][1]:cap, :] = jnp.zeros((tg, cap - pieces[k][1], D), BF16)

    for k, (lo, hi) in enumerate(pieces):
        pl.when((used > lo) & (used <= hi) if hi < cap else used > lo)(functools.partial(run, k))

    @pl.when(used == 0)
    def _():
        o_ref[...] = jnp.zeros_like(o_ref)


def _experts(xs, cnt, wg, wu, wd, layer):
    nt, L, D = xs.shape
    _, E, _, Hd = wg.shape
    cap = L // E
    tg = math.gcd(nt, 16)
    used = jnp.minimum(jnp.max(cnt.reshape(nt // tg, tg, E), axis=1), cap).T.astype(jnp.int32)
    slots = pl.BlockSpec((tg, None, cap, D), lambda e, g, u: (g, e, 0, 0))

    def piece_spec(lo, hi):
        assert lo % (hi - lo) == 0

        def index(e, g, u):
            live = u[e, g] > lo
            return (jnp.where(live, g, 0), jnp.where(live, e, 0), lo // (hi - lo), 0)

        return pl.BlockSpec((tg, None, hi - lo, D), index)

    pieces = _slot_pieces(cap)
    xs4 = xs.reshape(nt, E, cap, D)
    grid_spec = pltpu.PrefetchScalarGridSpec(
        num_scalar_prefetch=1,
        grid=(E, nt // tg),
        in_specs=[piece_spec(lo, hi) for lo, hi in pieces] + [
            pl.BlockSpec((1, 1, D, Hd), lambda e, g, u: (layer, e, 0, 0)),
            pl.BlockSpec((1, 1, D, Hd), lambda e, g, u: (layer, e, 0, 0)),
            pl.BlockSpec((1, 1, Hd, D), lambda e, g, u: (layer, e, 0, 0)),
        ],
        out_specs=slots,
        scratch_shapes=[pltpu.VMEM((D, Hd), BF16), pltpu.VMEM((D, Hd), BF16),
                        pltpu.VMEM((Hd, D), BF16)],
    )
    return pl.pallas_call(
        _experts_kernel,
        grid_spec=grid_spec,
        out_shape=jax.ShapeDtypeStruct((nt, E, cap, D), BF16),
        compiler_params=_cparams(("arbitrary", "arbitrary")),
        name="moe_experts",
    )(used, *([xs4] * len(pieces)), wg, wu, wd).reshape(nt, L, D)


def _combine_kernel(y_ref, wc_ref, rep_ref, h_ref, x1_ref, g2_ref, sg_ref, su_ref, sd_ref, fg_ref,
                    *rest, final_norm):
    ex_ref = rest[0] if len(rest) == 2 else None
    o_ref = rest[-1]
    nb, ts, D = h_ref.shape
    tm = nb * ts
    wc = wc_ref[...].reshape(tm, LANES)
    _, pos = _slot_positions(wc)
    rep = rep_ref[...]
    pos_lanes = _dot(pos, rep)
    w_lanes = _dot(wc, rep)
    slot = (lax.broadcasted_iota(jnp.int32, (1, rep.shape[1]), 1) % MOE_CAP).astype(F32)
    weighted = jnp.where(pos_lanes == slot, w_lanes, 0.0).astype(BF16)
    routed = jnp.dot(weighted, y_ref[...], preferred_element_type=F32)
    h = h_ref[...].reshape(tm, D)
    hid = _silu(jnp.dot(h, sg_ref[...], preferred_element_type=F32)) * jnp.dot(
        h, su_ref[...], preferred_element_type=F32)
    y = routed + _dot(hid, sd_ref[...])
    if ex_ref is not None:
        y = y + ex_ref[...].reshape(tm, D)
    gate = jnp.broadcast_to(g2_ref[...], (nb, ts, D)).reshape(tm, D)
    xo = x1_ref[...].reshape(tm, D) + gate * y
    if final_norm:
        ms = jnp.mean(xo * xo, axis=-1, keepdims=True)
        xo = xo * lax.rsqrt(ms + NORM_EPS) * fg_ref[...]
    o_ref[...] = xo.reshape(nb, ts, D)


def _combine(ys, wc, h, x1, extra, gate2, sg, su, sd, final_g, B, T, final_norm):
    N, D = x1.shape
    ts = MOE_TILE_SEQ
    nt = T // ts
    L = N_EXPERTS * MOE_CAP
    cols = np.arange(L) // MOE_CAP
    rep = jnp.asarray((np.arange(LANES)[:, None] == cols[None, :]).astype(np.float32)).astype(BF16)
    tile = lambda i: (0, i, 0, 0)
    const = lambda i: (0, 0)
    tok = lambda a: a.reshape(B, nt, ts, a.shape[-1])
    extras = [] if extra is None else [tok(extra)]
    out = pl.pallas_call(
        functools.partial(_combine_kernel, final_norm=final_norm),
        grid=(nt,),
        in_specs=[
            pl.BlockSpec((None, L, D), lambda i: (i, 0, 0)),
            pl.BlockSpec((B, None, ts, LANES), tile),
            pl.BlockSpec((LANES, L), const),
            pl.BlockSpec((B, None, ts, D), tile),
            pl.BlockSpec((B, None, ts, D), tile),
            pl.BlockSpec((B, 1, D), lambda i: (0, 0, 0)),
            pl.BlockSpec(sg.shape, const),
            pl.BlockSpec(su.shape, const),
            pl.BlockSpec(sd.shape, const),
            pl.BlockSpec((1, D), const),
        ] + [pl.BlockSpec((B, None, ts, D), tile) for _ in extras],
        out_specs=pl.BlockSpec((B, None, ts, D), tile),
        out_shape=jax.ShapeDtypeStruct((B, nt, ts, D), F32),
        compiler_params=_cparams(("arbitrary",)),
        name="moe_combine",
    )(ys, tok(wc), rep, tok(h), tok(x1), gate2.reshape(B, 1, D), sg, su, sd,
      final_g.reshape(1, D), *extras)
    return out.reshape(N, D)


def _overflow_kernel(h_ref, wc_ref, wg_ref, wu_ref, wd_ref, o_ref):
    e = pl.program_id(1)

    @pl.when(e == 0)
    def _():
        o_ref[...] = jnp.zeros_like(o_ref)

    h = h_ref[...]
    lane = lax.broadcasted_iota(jnp.int32, wc_ref.shape, 1)
    w = jnp.sum(jnp.where(lane == e, wc_ref[...], 0.0), axis=1, keepdims=True)
    hid = _silu(jnp.dot(h, wg_ref[0], preferred_element_type=F32)) * jnp.dot(
        h, wu_ref[0], preferred_element_type=F32)
    o_ref[...] += _dot(hid * w, wd_ref[0])


def _overflow(h, wc_over, wg, wu, wd):
    N, D = h.shape
    tm = math.gcd(N, 1024)
    E, _, Hd = wg.shape
    row = lambda i, e: (i, 0)
    return pl.pallas_call(
        _overflow_kernel,
        grid=(N // tm, E),
        in_specs=[
            pl.BlockSpec((tm, D), row),
            pl.BlockSpec((tm, LANES), row),
            pl.BlockSpec((1, D, Hd), lambda i, e: (e, 0, 0)),
            pl.BlockSpec((1, D, Hd), lambda i, e: (e, 0, 0)),
            pl.BlockSpec((1, Hd, D), lambda i, e: (e, 0, 0)),
        ],
        out_specs=pl.BlockSpec((tm, D), row),
        out_shape=jax.ShapeDtypeStruct((N, D), F32),
        compiler_params=_cparams(("arbitrary", "arbitrary")),
        name="moe_overflow",
    )(h, wc_over, wg, wu, wd)


def _moe(h, wc, wg, wu, wd, layer, sg, su, sd, x1, gate2, final_g, B, T, final_norm):
    xs, wc_over, cnt = _dispatch(h, wc, B, T)
    ys = _experts(xs, cnt, wg, wu, wd, layer)
    def finish(extra):
        return _combine(ys, wc, h, x1, extra, gate2, sg, su, sd, final_g, B, T, final_norm)

    def with_overflow():
        return finish(_overflow(h, wc_over, wg[layer].astype(BF16), wu[layer].astype(BF16),
                                wd[layer].astype(BF16)))

    return lax.cond(jnp.any(wc_over != 0.0), with_overflow, lambda: finish(None))


def _split_w_in(w_in):
    w_rwkv = w_in[:, :RWKV_COLS]
    w_ret = w_in[:, RWKV_COLS:RWKV_COLS + RET_COLS]
    w_pool = w_in[:, RWKV_COLS + RET_COLS:]
    return w_rwkv.astype(BF16), w_ret.astype(BF16), w_pool.astype(BF16)


def kernel(x, c, norm1_g, norm2_g, w_ada, b_ada, w_in, w_out, rwkv_mu, rwkv_w_up, rwkv_w0,
           rwkv_a_up, rwkv_a0, rwkv_g_up, rwkv_k_k, rwkv_k_a, rwkv_r_k, rwkv_lnx_w, rwkv_lnx_b,
           pool_w, pool_scale, w_router, router_bias, we_gate, we_up, we_down,
           ws_gate, ws_up, ws_down, final_g):
    B, T, D = x.shape
    L = w_in.shape[0]
    N = B * T
    mod = _adaln(c, w_ada, b_ada)
    x2 = x.reshape(N, D)
    for l in range(L):
        shift1, scale1, gate1, shift2, scale2, gate2 = jnp.split(mod[l], 6, axis=-1)
        w1, w2, w3 = _split_w_in(w_in[l])
        p_rwkv, p_ret, p_pool = _inproj(x2, norm1_g[l], scale1, shift1, w1, w2, w3, T)
        y_rwkv = _rwkv(p_rwkv.reshape(B, T, -1), rwkv_mu[l], rwkv_w_up[l], rwkv_w0[l],
                       rwkv_a_up[l], rwkv_a0[l], rwkv_g_up[l], rwkv_k_k[l], rwkv_k_a[l],
                       rwkv_r_k[l], rwkv_lnx_w[l], rwkv_lnx_b[l])
        y_ret = _retention(p_ret.reshape(B, T, -1))
        y_pool = _pool(p_pool.reshape(B, T, -1), pool_w[l], pool_scale[l])
        x1, h, wc = _outproj(y_rwkv.reshape(N, -1), y_ret.reshape(N, -1), y_pool.reshape(N, -1),
                             x2, w_out[l], gate1, norm2_g[l], scale2, shift2,
                             w_router[l], router_bias[l], T)
        x2 = _moe(h, wc, we_gate, we_up, we_down, l,
                  ws_gate[l].astype(BF16), ws_up[l].astype(BF16),
                  ws_down[l].astype(BF16), x1, gate2, final_g, B, T,
                  final_norm=(l == L - 1))
    return x2.reshape(B, T, D)
```

```python
import functools
import math

import numpy as np
import jax
import jax.numpy as jnp
from jax import lax
from jax.experimental import pallas as pl
from jax.experimental.pallas import tpu as pltpu

F32 = jnp.float32
BF16 = jnp.bfloat16
HIGHEST = lax.Precision.HIGHEST

D_MODEL = 1024
HEADS = 6
HEAD_DIM = 64
WIDTH = HEADS * HEAD_DIM
W_LORA = 64
A_LORA = 64
G_LORA = 128
RWKV_COLS = 3 * WIDTH + W_LORA + A_LORA + G_LORA
RET_COLS = 4 * WIDTH
RET_CHUNK = 128
RWKV_CHUNK = 128
RWKV_SUB = 16
RET_ROWS = 4
RWKV_ROWS = 4
POOL_GROUPS = 4
POOL_GROUP_DIM = 64
POOL_WIDTH = POOL_GROUPS * POOL_GROUP_DIM
POOL_WINDOWS = (2, 4, 8, 16)
N_EXPERTS = 64
TOP_K = 8
N_GROUPS = 8
TOPK_GROUPS = 4
EXPERT_HIDDEN = 256
ROUTED_SCALE = 2.5
NORM_EPS = 1e-6
RWKV_LNX_EPS = 64e-5
RET_NORM_EPS = 1e-6
ROPE_BASE = 10000.0
LANES = 128
MOE_TILE_SEQ = 16
MOE_CAP = 80
MOE_USED_STEPS = (48, 64)
VMEM_LIMIT = 48 * 1024 * 1024


def _cparams(sem):
    return pltpu.CompilerParams(dimension_semantics=sem, vmem_limit_bytes=VMEM_LIMIT)


def _dot(a, b):
    return jnp.dot(a.astype(BF16), b.astype(BF16), preferred_element_type=F32)


def _dot_nt(a, b):
    return lax.dot_general(a.astype(BF16), b.astype(BF16), (((1,), (1,)), ((), ())),
                           preferred_element_type=F32)


def _dot_tn(a, b):
    return lax.dot_general(a.astype(BF16), b.astype(BF16), (((0,), (0,)), ((), ())),
                           preferred_element_type=F32)


def _dot_f32(a, b):
    return jnp.dot(a, b, preferred_element_type=F32, precision=HIGHEST)


def _sigmoid(x):
    return 1.0 / (1.0 + jnp.exp(-x))


def _silu(x):
    return x * _sigmoid(x)


def _head_masks(rows, width):
    lane = lax.broadcasted_iota(jnp.int32, (rows, width), 1)
    return [lane // HEAD_DIM == h for h in range(width // HEAD_DIM)]


def _stack_heads(x, masks):
    return jnp.concatenate([jnp.where(m, x, 0.0) for m in masks], axis=0).astype(BF16)


def _select_heads(stacked, masks, c):
    out = stacked[0:c]
    for h in range(1, len(masks)):
        out = jnp.where(masks[h], stacked[h * c:(h + 1) * c], out)
    return out


def _adaln_kernel(c_ref, w_ref, b_ref, o_ref):
    o_ref[0] = _dot_f32(_silu(c_ref[...]), w_ref[0]) + b_ref[0]


def _adaln(c, w_ada, b_ada):
    L, D, M = w_ada.shape
    B = c.shape[0]
    tn = 1536
    return pl.pallas_call(
        _adaln_kernel,
        grid=(L, M // tn),
        in_specs=[
            pl.BlockSpec((B, D), lambda l, j: (0, 0)),
            pl.BlockSpec((1, D, tn), lambda l, j: (l, 0, j)),
            pl.BlockSpec((1, 1, tn), lambda l, j: (l, 0, j)),
        ],
        out_specs=pl.BlockSpec((1, B, tn), lambda l, j: (l, 0, j)),
        out_shape=jax.ShapeDtypeStruct((L, B, M), F32),
        compiler_params=_cparams(("arbitrary", "arbitrary")),
        name="adaln",
    )(c, w_ada, b_ada.reshape(L, 1, M))


def _modulated_norm(x, g, scale, shift):
    ms = jnp.mean(x * x, axis=-1, keepdims=True)
    return x * lax.rsqrt(ms + NORM_EPS) * g * (1.0 + scale) + shift


def _inproj_kernel(x_ref, g_ref, sc_ref, sh_ref, w1_ref, w2_ref, w3_ref, o1_ref, o2_ref, o3_ref):
    h = _modulated_norm(x_ref[...], g_ref[...], sc_ref[0], sh_ref[0]).astype(BF16)
    o1_ref[...] = jnp.dot(h, w1_ref[...], preferred_element_type=F32)
    o2_ref[...] = jnp.dot(h, w2_ref[...], preferred_element_type=F32)
    o3_ref[...] = jnp.dot(h, w3_ref[...], preferred_element_type=F32)


def _inproj(x2, g, scale, shift, w1, w2, w3, T):
    N, D = x2.shape
    B = N // T
    tm = min(512, T)
    per_b = T // tm
    row = lambda i: (i, 0)
    const = lambda i: (0, 0)
    bvec = lambda i: (i // per_b, 0, 0)
    return pl.pallas_call(
        _inproj_kernel,
        grid=(N // tm,),
        in_specs=[
            pl.BlockSpec((tm, D), row),
            pl.BlockSpec((1, D), const),
            pl.BlockSpec((1, 1, D), bvec),
            pl.BlockSpec((1, 1, D), bvec),
            pl.BlockSpec(w1.shape, const),
            pl.BlockSpec(w2.shape, const),
            pl.BlockSpec(w3.shape, const),
        ],
        out_specs=[
            pl.BlockSpec((tm, w1.shape[1]), row),
            pl.BlockSpec((tm, w2.shape[1]), row),
            pl.BlockSpec((tm, w3.shape[1]), row),
        ],
        out_shape=[
            jax.ShapeDtypeStruct((N, w1.shape[1]), F32),
            jax.ShapeDtypeStruct((N, w2.shape[1]), F32),
            jax.ShapeDtypeStruct((N, w3.shape[1]), F32),
        ],
        compiler_params=_cparams(("arbitrary",)),
        name="inproj",
    )(x2, g.reshape(1, D), scale.reshape(B, 1, D), shift.reshape(B, 1, D), w1, w2, w3)


def _unit_lower_inverse(a3):
    H, C, _ = a3.shape
    ri = lax.broadcasted_iota(jnp.int32, (H, C, C), 1)
    ci = lax.broadcasted_iota(jnp.int32, (H, C, C), 2)
    eye = (ri == ci).astype(F32)
    same = (ri // RWKV_SUB) == (ci // RWKV_SUB)
    dm = jnp.where(same, a3, 0.0)
    off = jnp.where(same, 0.0, a3)

    def bmm(x, y):
        return jnp.einsum('hij,hjk->hik', x.astype(BF16), y.astype(BF16),
                          preferred_element_type=F32)

    d2 = bmm(dm, dm)
    d4 = bmm(d2, d2)
    d8 = bmm(d4, d4)
    x = eye - dm
    x = x + bmm(x, d2)
    x = x + bmm(x, d4)
    x = x + bmm(x, d8)
    n = bmm(x, off)
    y = eye - n
    power = bmm(n, n)
    order = 2
    while order < C // RWKV_SUB:
        y = y + bmm(y, power)
        order *= 2
        if order < C // RWKV_SUB:
            power = bmm(power, power)
    return bmm(y, x)


def _rwkv_prepare(p, carry, prm):
    (mu, wup, w0, aup, a0, gup, k_k, k_a, r_k, lnw, lnb, hsum) = prm
    C = RWKV_CHUNK
    W = WIDTH
    H = HEADS
    row = lax.broadcasted_iota(jnp.int32, (C, 1), 0)
    prev = jnp.where(row == 0, carry, pltpu.roll(p, 1, 0))
    xs = p + (prev - p) * mu

    r = xs[:, 0:W]
    k = xs[:, W:2 * W]
    v = xs[:, 2 * W:3 * W]
    xwa = xs[:, 3 * W:3 * W + W_LORA + A_LORA]
    xg = xs[:, 3 * W + W_LORA + A_LORA:]

    z = w0 + _dot(jnp.tanh(xwa), wup)
    log_w = -math.exp(-0.5) * _sigmoid(z)
    a = _sigmoid(a0 + _dot(xwa, aup))
    g = _dot(_sigmoid(xg), gup)
    kk = k * k_k
    kk = kk / jnp.maximum(jnp.sqrt(_dot(kk * kk, hsum)), 1e-12)
    k = k * (1.0 + (a - 1.0) * k_a)

    ti = lax.broadcasted_iota(jnp.int32, (C, C), 0)
    si = lax.broadcasted_iota(jnp.int32, (C, C), 1)
    tri = jnp.where(ti >= si, 1.0, 0.0).astype(BF16)
    log_w_hi = log_w.astype(BF16)
    cum = (jnp.dot(tri, log_w_hi, preferred_element_type=F32)
           + _dot(tri, log_w - log_w_hi.astype(F32)))
    g_in = jnp.exp(cum)
    g_inv = jnp.exp(-cum)
    g_end = g_in[C - 1:C, :]
    kt = kk * jnp.exp(cum - log_w)
    bt = kk * a * g_inv
    kq = k * g_inv
    rt = r * g_in

    masks = _head_masks(C, W)
    t3 = lax.broadcasted_iota(jnp.int32, (H, C, C), 1)
    s3 = lax.broadcasted_iota(jnp.int32, (H, C, C), 2)
    a_ab = jnp.where(t3 > s3, _dot_nt(_stack_heads(kt, masks), bt).reshape(H, C, C), 0.0)
    bt_rows = _stack_heads(bt, masks)
    kq_rows = _stack_heads(kq, masks)
    t1 = lax.broadcasted_iota(jnp.int32, (C, H * C), 0)
    s1 = lax.broadcasted_iota(jnp.int32, (C, H * C), 1) % C
    a_ak = jnp.where(t1 > s1, _dot_nt(kt, kq_rows), 0.0)
    t2 = lax.broadcasted_iota(jnp.int32, (C, 2 * H * C), 0)
    s2 = lax.broadcasted_iota(jnp.int32, (C, 2 * H * C), 1) % C
    p_r = jnp.where(t2 >= s2, _dot_nt(rt, jnp.concatenate([bt_rows, kq_rows], axis=0)), 0.0)
    return a_ab, (a_ak, p_r, kt, rt, bt, kq, g_end, r, k, v, g)


def _rwkv_finish(vals, t_inv, s0, prm):
    (a_ak, p_r, kt, rt, bt, kq, g_end, r, k, v, g) = vals
    (mu, wup, w0, aup, a0, gup, k_k, k_a, r_k, lnw, lnb, hsum) = prm
    C = RWKV_CHUNK
    W = WIDTH
    H = HEADS
    masks = _head_masks(C, W)
    ks = _dot_nt(kt, s0)
    rs = _dot_nt(rt, s0)
    v_rows = _stack_heads(v, masks)
    av = _dot(a_ak, v_rows)
    u = _select_heads(_dot(t_inv.reshape(H * C, C), -(ks + av)), masks, C)
    y = rs + _dot(p_r, jnp.concatenate([_stack_heads(u, masks), v_rows], axis=0))
    upd = _dot_tn(jnp.concatenate([u, v], axis=0),
                  jnp.concatenate([bt * g_end, kq * g_end], axis=0))
    hi = lax.broadcasted_iota(jnp.int32, (W, W), 0) // HEAD_DIM
    hj = lax.broadcasted_iota(jnp.int32, (W, W), 1) // HEAD_DIM
    s_new = s0 * g_end + jnp.where(hi == hj, upd, 0.0)

    inv_d = 1.0 / HEAD_DIM
    mean = _dot(y, hsum) * inv_d
    yc = y - mean
    var = _dot(yc * yc, hsum) * inv_d
    yn = yc * lax.rsqrt(var + RWKV_LNX_EPS) * lnw + lnb
    bonus = _dot(r * k * r_k, hsum) * v
    return (yn + bonus) * g, s_new


def _rwkv_kernel(p_ref, mu_ref, wup_ref, w0_ref, aup_ref, a0_ref, gup_ref, kk_ref, ka_ref,
                 rk_ref, lnw_ref, lnb_ref, hsum_ref, o_ref, carry_ref, s_ref):
    @pl.when(pl.program_id(1) == 0)
    def _():
        carry_ref[...] = jnp.zeros_like(carry_ref)
        s_ref[...] = jnp.zeros_like(s_ref)

    prm = tuple(ref[...] for ref in (mu_ref, wup_ref, w0_ref, aup_ref, a0_ref, gup_ref, kk_ref,
                                     ka_ref, rk_ref, lnw_ref, lnb_ref, hsum_ref))
    G = p_ref.shape[0]
    H = HEADS
    a_abs, vals = [], []
    for i in range(G):
        p = p_ref[i]
        a_ab, val = _rwkv_prepare(p, carry_ref[i], prm)
        carry_ref[i] = p[RWKV_CHUNK - 1:RWKV_CHUNK, :]
        a_abs.append(a_ab)
        vals.append(val)
    t_inv = _unit_lower_inverse(jnp.concatenate(a_abs, axis=0))
    for i in range(G):
        out, s_new = _rwkv_finish(vals[i], t_inv[i * H:(i + 1) * H], s_ref[i], prm)
        s_ref[i] = s_new
        o_ref[i] = out


def _rwkv(p, mu, wup, w0, aup, a0, gup, k_k, k_a, r_k, lnx_w, lnx_b):
    B, T, _ = p.shape
    C = RWKV_CHUNK
    W = WIDTH
    G = RWKV_ROWS if B % RWKV_ROWS == 0 else 1
    lora_in = W_LORA + A_LORA
    wup_pad = jnp.zeros((lora_in, W), F32).at[:W_LORA].set(wup).astype(BF16)
    aup_pad = jnp.zeros((lora_in, W), F32).at[W_LORA:].set(aup).astype(BF16)
    head = np.arange(W) // HEAD_DIM
    hsum = jnp.asarray((head[:, None] == head[None, :]).astype(np.float32)).astype(BF16)
    vec = lambda a: a.reshape(1, -1)
    const = lambda b, t: (0, 0)
    params = [vec(mu), wup_pad, vec(w0), aup_pad, vec(a0), gup.astype(BF16), vec(k_k), vec(k_a),
              vec(r_k), vec(lnx_w), vec(lnx_b), hsum]
    return pl.pallas_call(
        _rwkv_kernel,
        grid=(B // G, T // C),
        in_specs=[pl.BlockSpec((G, C, RWKV_COLS), lambda b, t: (b, t, 0))]
        + [pl.BlockSpec(a.shape, const) for a in params],
        out_specs=pl.BlockSpec((G, C, W), lambda b, t: (b, t, 0)),
        out_shape=jax.ShapeDtypeStruct((B, T, W), F32),
        scratch_shapes=[pltpu.VMEM((G, 1, RWKV_COLS), F32), pltpu.VMEM((G, W, W), F32)],
        compiler_params=_cparams(("arbitrary", "arbitrary")),
        name="rwkv7",
    )(p, *params)


def _retention_kernel(p_ref, cos_ref, sin_ref, dec_ref, xi_ref, zeta_ref, cd_ref, hsum_ref,
                      o_ref, s_ref):
    C = RET_CHUNK
    W = WIDTH

    @pl.when(pl.program_id(1) == 0)
    def _():
        s_ref[...] = jnp.zeros_like(s_ref)

    cos = cos_ref[...]
    sin = sin_ref[...]
    masks = _head_masks(C, W)
    hi = lax.broadcasted_iota(jnp.int32, (W, W), 0) // HEAD_DIM
    hj = lax.broadcasted_iota(jnp.int32, (W, W), 1) // HEAD_DIM
    first_half = lax.broadcasted_iota(jnp.int32, (C, W), 1) % HEAD_DIM < HEAD_DIM // 2

    def swap_halves(x):
        return jnp.where(first_half, pltpu.roll(x, W - HEAD_DIM // 2, 1),
                         pltpu.roll(x, HEAD_DIM // 2, 1))

    for i in range(p_ref.shape[0]):
        p = p_ref[i]
        q = p[:, 0:W] * cos + swap_halves(p[:, 0:W]) * sin
        k = (p[:, W:2 * W] * cos + swap_halves(p[:, W:2 * W]) * sin) * (HEAD_DIM ** -0.5)
        v = p[:, 2 * W:3 * W]
        gate = p[:, 3 * W:4 * W]
        scores = _dot_nt(q, _stack_heads(k, masks)) * dec_ref[...]
        y = _dot(scores, _stack_heads(v, masks))
        s0 = s_ref[i]
        y = y + _dot(q * xi_ref[...], s0)
        kv = _dot_tn(k * zeta_ref[...], v)
        s_ref[i] = s0 * cd_ref[...] + jnp.where(hi == hj, kv, 0.0)
        ms = _dot(y * y, hsum_ref[...]) * (1.0 / HEAD_DIM)
        o_ref[i] = _silu(gate) * (y * lax.rsqrt(ms + RET_NORM_EPS))


def _retention_tables(T):
    C, H, d = RET_CHUNK, HEADS, HEAD_DIM
    pos = jnp.arange(T, dtype=F32)
    inv_freq = ROPE_BASE ** (-jnp.arange(0, d, 2, dtype=F32) / d)
    ang = pos[:, None] * inv_freq[None, :]
    cos = jnp.cos(ang)
    sin = jnp.sin(ang)
    cos_full = jnp.tile(jnp.concatenate([cos, cos], -1), (1, H))
    sin_full = jnp.tile(jnp.concatenate([-sin, sin], -1), (1, H))
    log_gamma = jnp.log1p(-(2.0 ** (-5.0 - jnp.arange(H, dtype=F32))))
    idx = jnp.arange(C, dtype=F32)
    diff = idx[:, None] - idx[None, :]
    dec = jnp.where(diff >= 0, jnp.exp(log_gamma[:, None, None] * jnp.maximum(diff, 0.0)), 0.0)
    xi = jnp.exp(log_gamma[:, None] * (idx + 1.0))
    zeta = jnp.exp(log_gamma[:, None] * (C - 1.0 - idx))
    cd = jnp.exp(log_gamma * C)
    per_lane = lambda a: jnp.repeat(a.T, d, axis=1)
    return (cos_full, sin_full, dec.transpose(1, 0, 2).reshape(C, H * C), per_lane(xi), per_lane(zeta),
            jnp.repeat(cd, d).reshape(1, H * d))


def _retention(p):
    B, T, cols = p.shape
    C = RET_CHUNK
    W = WIDTH
    cos, sin, dec, xi, zeta, cd = _retention_tables(T)
    head = np.arange(W) // HEAD_DIM
    hsum = jnp.asarray((head[:, None] == head[None, :]).astype(np.float32)).astype(BF16)
    const = lambda b, t: (0, 0)
    G = RET_ROWS if B % RET_ROWS == 0 else 1
    return pl.pallas_call(
        _retention_kernel,
        grid=(B // G, T // C),
        in_specs=[
            pl.BlockSpec((G, C, cols), lambda b, t: (b, t, 0)),
            pl.BlockSpec((C, W), lambda b, t: (t, 0)),
            pl.BlockSpec((C, W), lambda b, t: (t, 0)),
            pl.BlockSpec(dec.shape, const),
            pl.BlockSpec(xi.shape, const),
            pl.BlockSpec(zeta.shape, const),
            pl.BlockSpec(cd.shape, const),
            pl.BlockSpec(hsum.shape, const),
        ],
        out_specs=pl.BlockSpec((G, C, W), lambda b, t: (b, t, 0)),
        out_shape=jax.ShapeDtypeStruct((B, T, W), F32),
        scratch_shapes=[pltpu.VMEM((G, W, W), F32)],
        compiler_params=_cparams(("arbitrary", "arbitrary")),
        name="retention",
    )(p, cos, sin, dec, xi, zeta, cd, hsum)


def _pool_kernel(u_ref, w_ref, scale_ref, o_ref):
    u = u_ref[0]
    T = u.shape[0]
    row = lax.broadcasted_iota(jnp.int32, (T, 1), 0)

    def lag(x, k):
        return jnp.where(row >= k, pltpu.roll(x, k, 0), 0.0)

    s2 = u + lag(u, 1)
    s4 = s2 + lag(s2, 2)
    s8 = s4 + lag(s4, 4)
    s16 = s8 + lag(s8, 8)
    grp = lax.broadcasted_iota(jnp.int32, (1, POOL_WIDTH), 1) // POOL_GROUP_DIM
    s = jnp.where(grp == 0, s2, jnp.where(grp == 1, s4, jnp.where(grp == 2, s8, s16)))
    win = jnp.where(grp == 0, POOL_WINDOWS[0],
                    jnp.where(grp == 1, POOL_WINDOWS[1],
                              jnp.where(grp == 2, POOL_WINDOWS[2], POOL_WINDOWS[3])))
    count = jnp.minimum(row + 1, win).astype(F32)
    pooled = s / count - u
    o_ref[0] = _dot(pooled, w_ref[...]) * scale_ref[...]


def _pool(u, pool_w, pool_scale):
    B, T, Wp = u.shape
    G, d = POOL_GROUPS, POOL_GROUP_DIM
    wbd = jnp.zeros((Wp, Wp), F32)
    for gi in range(G):
        wbd = wbd.at[gi * d:(gi + 1) * d, gi * d:(gi + 1) * d].set(pool_w[gi])
    return pl.pallas_call(
        _pool_kernel,
        grid=(B,),
        in_specs=[
            pl.BlockSpec((1, T, Wp), lambda b: (b, 0, 0)),
            pl.BlockSpec((Wp, Wp), lambda b: (0, 0)),
            pl.BlockSpec((1, Wp), lambda b: (0, 0)),
        ],
        out_specs=pl.BlockSpec((1, T, Wp), lambda b: (b, 0, 0)),
        out_shape=jax.ShapeDtypeStruct((B, T, Wp), F32),
        compiler_params=_cparams(("arbitrary",)),
        name="pool",
    )(u, wbd.astype(BF16), pool_scale.reshape(1, Wp))


def _route(logits_t, bias_col):
    E, tm = logits_t.shape
    per_group = E // N_GROUPS
    neg_inf = -jnp.inf
    scores = _sigmoid(logits_t)
    choice = scores + bias_col
    c3 = choice.reshape(N_GROUPS, per_group, tm)
    sub = lax.broadcasted_iota(jnp.int32, c3.shape, 1)
    m1 = jnp.max(c3, axis=1, keepdims=True)
    first = jnp.min(jnp.where(c3 == m1, sub, per_group), axis=1, keepdims=True)
    m2 = jnp.max(jnp.where(sub == first, neg_inf, c3), axis=1, keepdims=True)
    gs = m1 + m2
    gidx = lax.broadcasted_iota(jnp.int32, gs.shape, 0)
    grank = jnp.zeros(gs.shape, jnp.int32)
    for j in range(N_GROUPS):
        other = gs[j:j + 1]
        ahead = jnp.where(other > gs, 1, jnp.where((other == gs) & (gidx > j), 1, 0))
        grank = grank + ahead
    gmask = jnp.broadcast_to(grank < TOPK_GROUPS, c3.shape)
    masked = jnp.where(gmask, c3, neg_inf).reshape(E, tm)
    eidx = lax.broadcasted_iota(jnp.int32, (E, tm), 0)
    top = jnp.zeros((E, tm), F32)
    for _ in range(TOP_K):
        best = jnp.max(masked, axis=0, keepdims=True)
        first = jnp.min(jnp.where(masked == best, eidx, E), axis=0, keepdims=True)
        hit = eidx == first
        top = jnp.where(hit, scores, top)
        masked = jnp.where(hit, neg_inf, masked)
    return top / jnp.sum(top, axis=0, keepdims=True) * ROUTED_SCALE


def _outproj_kernel(yr_ref, yt_ref, yp_ref, x_ref, w1_ref, w2_ref, w3_ref, g1_ref, ng_ref,
                    sc_ref, sh_ref, wr_ref, rb_ref, x1_ref, h_ref, wc_ref):
    mixed = (_dot(yr_ref[...], w1_ref[...]) + _dot(yt_ref[...], w2_ref[...])
             + _dot(yp_ref[...], w3_ref[...]))
    x1 = x_ref[...] + g1_ref[0] * mixed
    x1_ref[...] = x1
    h = _modulated_norm(x1, ng_ref[...], sc_ref[0], sh_ref[0])
    h_ref[...] = h.astype(BF16)
    logits_t = lax.dot_general(wr_ref[...], h, (((1,), (1,)), ((), ())),
                               preferred_element_type=F32, precision=HIGHEST)
    wc_t = _route(logits_t, rb_ref[...])
    pad = jnp.zeros((LANES - N_EXPERTS, wc_t.shape[1]), F32)
    wc_ref[...] = jnp.concatenate([wc_t, pad], axis=0).T


def _outproj(yr, yt, yp, x2, w_out, gate1, ng, scale2, shift2, w_router, router_bias, T):
    N, D = x2.shape
    B = N // T
    tm = min(1024, T)
    per_b = T // tm
    W = WIDTH
    w1 = w_out[:W].astype(BF16)
    w2 = w_out[W:2 * W].astype(BF16)
    w3 = w_out[2 * W:].astype(BF16)
    row = lambda i: (i, 0)
    const = lambda i: (0, 0)
    bvec = lambda i: (i // per_b, 0, 0)
    return pl.pallas_call(
        _outproj_kernel,
        grid=(N // tm,),
        in_specs=[
            pl.BlockSpec((tm, W), row),
            pl.BlockSpec((tm, W), row),
            pl.BlockSpec((tm, POOL_WIDTH), row),
            pl.BlockSpec((tm, D), row),
            pl.BlockSpec(w1.shape, const),
            pl.BlockSpec(w2.shape, const),
            pl.BlockSpec(w3.shape, const),
            pl.BlockSpec((1, 1, D), bvec),
            pl.BlockSpec((1, D), const),
            pl.BlockSpec((1, 1, D), bvec),
            pl.BlockSpec((1, 1, D), bvec),
            pl.BlockSpec((N_EXPERTS, D), const),
            pl.BlockSpec((N_EXPERTS, 1), const),
        ],
        out_specs=[
            pl.BlockSpec((tm, D), row),
            pl.BlockSpec((tm, D), row),
            pl.BlockSpec((tm, LANES), row),
        ],
        out_shape=[
            jax.ShapeDtypeStruct((N, D), F32),
            jax.ShapeDtypeStruct((N, D), BF16),
            jax.ShapeDtypeStruct((N, LANES), F32),
        ],
        compiler_params=_cparams(("arbitrary",)),
        name="outproj_router",
    )(yr, yt, yp, x2, w1, w2, w3, gate1.reshape(B, 1, D), ng.reshape(1, D),
      scale2.reshape(B, 1, D), shift2.reshape(B, 1, D), w_router.T,
      router_bias.reshape(N_EXPERTS, 1))


def _slot_positions(wc):
    tm = wc.shape[0]
    sel = jnp.where(wc > 0.0, 1.0, 0.0).astype(BF16)
    ti = lax.broadcasted_iota(jnp.int32, (tm, tm), 0)
    si = lax.broadcasted_iota(jnp.int32, (tm, tm), 1)
    earlier = jnp.where(ti > si, 1.0, 0.0).astype(BF16)
    return sel, jnp.dot(earlier, sel, preferred_element_type=F32)


def _dispatch_kernel(h_ref, wc_ref, rept_ref, xs_ref, over_ref, cnt_ref):
    nb, ts, D = h_ref.shape
    tm = nb * ts
    wc = wc_ref[...].reshape(tm, LANES)
    sel, pos = _slot_positions(wc)
    over_ref[...] = jnp.where(pos >= MOE_CAP, wc, 0.0).reshape(nb, ts, LANES)
    cnt_ref[...] = jnp.sum(sel.astype(F32), axis=0, keepdims=True)
    rept = rept_ref[...]
    code = jnp.where(wc > 0.0, pos, -1.0)
    code_rows = _dot_nt(rept, code)
    slot = (lax.broadcasted_iota(jnp.int32, (rept.shape[0], 1), 0) % MOE_CAP).astype(F32)
    onehot = jnp.where(code_rows == slot, 1.0, 0.0).astype(BF16)
    xs_ref[...] = jnp.dot(onehot, h_ref[...].reshape(tm, D),
                          preferred_element_type=F32).astype(BF16)


def _dispatch(h, wc, B, T):
    N, D = h.shape
    ts = MOE_TILE_SEQ
    nt = T // ts
    L = N_EXPERTS * MOE_CAP
    rows = np.arange(L) // MOE_CAP
    rept = jnp.asarray((rows[:, None] == np.arange(LANES)[None, :]).astype(np.float32)).astype(BF16)
    tile = lambda i: (0, i, 0, 0)
    xs, over, cnt = pl.pallas_call(
        _dispatch_kernel,
        grid=(nt,),
        in_specs=[
            pl.BlockSpec((B, None, ts, D), tile),
            pl.BlockSpec((B, None, ts, LANES), tile),
            pl.BlockSpec((L, LANES), lambda i: (0, 0)),
        ],
        out_specs=[
            pl.BlockSpec((None, L, D), lambda i: (i, 0, 0)),
            pl.BlockSpec((B, None, ts, LANES), tile),
            pl.BlockSpec((None, 1, LANES), lambda i: (i, 0, 0)),
        ],
        out_shape=[
            jax.ShapeDtypeStruct((nt, L, D), BF16),
            jax.ShapeDtypeStruct((B, nt, ts, LANES), F32),
            jax.ShapeDtypeStruct((nt, 1, LANES), F32),
        ],
        compiler_params=_cparams(("arbitrary",)),
        name="moe_dispatch",
    )(h.reshape(B, nt, ts, D), wc.reshape(B, nt, ts, LANES), rept)
    return xs, over.reshape(N, LANES), cnt.reshape(nt, LANES)[:, :N_EXPERTS]


def _slot_pieces(cap):
    bounds = [0] + [n for n in MOE_USED_STEPS if n < cap] + [cap]
    return list(zip(bounds[:-1], bounds[1:]))


def _experts_kernel(used_ref, *refs):
    pieces = _slot_pieces(refs[-4].shape[1])
    x_refs = refs[:len(pieces)]
    wg_ref, wu_ref, wd_ref, o_ref, wg_s, wu_s, wd_s = refs[len(pieces):]

    @pl.when(pl.program_id(1) == 0)
    def _():
        wg_s[...] = wg_ref[0, 0].astype(BF16)
        wu_s[...] = wu_ref[0, 0].astype(BF16)
        wd_s[...] = wd_ref[0, 0].astype(BF16)

    tg, cap, D = o_ref.shape
    used = used_ref[pl.program_id(0), pl.program_id(1)]

    def run(k):
        x = jnp.concatenate([x_refs[j][...].reshape(tg * (hi - lo), D)
                             for j, (lo, hi) in enumerate(pieces[:k + 1])], axis=0)
        hid = _silu(jnp.dot(x, wg_s[...], preferred_element_type=F32)) * jnp.dot(
            x, wu_s[...], preferred_element_type=F32)
        y = _dot(hid, wd_s[...]).astype(BF16)
        for lo, hi in pieces[:k + 1]:
            o_ref[:, lo:hi, :] = y[tg * lo:tg * hi].reshape(tg, hi - lo, D)
        if pieces[k][1] < cap:
            o_ref[:, pieces[k][1]:cap, :] = jnp.zeros((tg, cap - pieces[k][1], D), BF16)

    for k, (lo, hi) in enumerate(pieces):
        pl.when((used > lo) & (used <= hi) if hi < cap else used > lo)(functools.partial(run, k))

    @pl.when(used == 0)
    def _():
        o_ref[...] = jnp.zeros_like(o_ref)


def _experts(xs, cnt, wg, wu, wd, layer):
    nt, L, D = xs.shape
    _, E, _, Hd = wg.shape
    cap = L // E
    tg = math.gcd(nt, 16)
    used = jnp.minimum(jnp.max(cnt.reshape(nt // tg, tg, E), axis=1), cap).T.astype(jnp.int32)
    slots = pl.BlockSpec((tg, None, cap, D), lambda e, g, u: (g, e, 0, 0))

    def piece_spec(lo, hi):
        assert lo % (hi - lo) == 0

        def index(e, g, u):
            live = u[e, g] > lo
            return (jnp.where(live, g, 0), jnp.where(live, e, 0), lo // (hi - lo), 0)

        return pl.BlockSpec((tg, None, hi - lo, D), index)

    pieces = _slot_pieces(cap)
    xs4 = xs.reshape(nt, E, cap, D)
    grid_spec = pltpu.PrefetchScalarGridSpec(
        num_scalar_prefetch=1,
        grid=(E, nt // tg),
        in_specs=[piece_spec(lo, hi) for lo, hi in pieces] + [
            pl.BlockSpec((1, 1, D, Hd), lambda e, g, u: (layer, e, 0, 0)),
            pl.BlockSpec((1, 1, D, Hd), lambda e, g, u: (layer, e, 0, 0)),
            pl.BlockSpec((1, 1, Hd, D), lambda e, g, u: (layer, e, 0, 0)),
        ],
        out_specs=slots,
        scratch_shapes=[pltpu.VMEM((D, Hd), BF16), pltpu.VMEM((D, Hd), BF16),
                        pltpu.VMEM((Hd, D), BF16)],
    )
    return pl.pallas_call(
        _experts_kernel,
        grid_spec=grid_spec,
        out_shape=jax.ShapeDtypeStruct((nt, E, cap, D), BF16),
        compiler_params=_cparams(("arbitrary", "arbitrary")),
        name="moe_experts",
    )(used, *([xs4] * len(pieces)), wg, wu, wd).reshape(nt, L, D)


def _combine_kernel(y_ref, wc_ref, rep_ref, h_ref, x1_ref, g2_ref, sg_ref, su_ref, sd_ref, fg_ref,
                    *rest, final_norm):
    ex_ref = rest[0] if len(rest) == 2 else None
    o_ref = rest[-1]
    nb, ts, D = h_ref.shape
    tm = nb * ts
    wc = wc_ref[...].reshape(tm, LANES)
    _, pos = _slot_positions(wc)
    rep = rep_ref[...]
    pos_lanes = _dot(pos, rep)
    w_lanes = _dot(wc, rep)
    slot = (lax.broadcasted_iota(jnp.int32, (1, rep.shape[1]), 1) % MOE_CAP).astype(F32)
    weighted = jnp.where(pos_lanes == slot, w_lanes, 0.0).astype(BF16)
    routed = jnp.dot(weighted, y_ref[...], preferred_element_type=F32)
    h = h_ref[...].reshape(tm, D)
    hid = _silu(jnp.dot(h, sg_ref[...], preferred_element_type=F32)) * jnp.dot(
        h, su_ref[...], preferred_element_type=F32)
    y = routed + _dot(hid, sd_ref[...])
    if ex_ref is not None:
        y = y + ex_ref[...].reshape(tm, D)
    gate = jnp.broadcast_to(g2_ref[...], (nb, ts, D)).reshape(tm, D)
    xo = x1_ref[...].reshape(tm, D) + gate * y
    if final_norm:
        ms = jnp.mean(xo * xo, axis=-1, keepdims=True)
        xo = xo * lax.rsqrt(ms + NORM_EPS) * fg_ref[...]
    o_ref[...] = xo.reshape(nb, ts, D)


def _combine(ys, wc, h, x1, extra, gate2, sg, su, sd, final_g, B, T, final_norm):
    N, D = x1.shape
    ts = MOE_TILE_SEQ
    nt = T // ts
    L = N_EXPERTS * MOE_CAP
    cols = np.arange(L) // MOE_CAP
    rep = jnp.asarray((np.arange(LANES)[:, None] == cols[None, :]).astype(np.float32)).astype(BF16)
    tile = lambda i: (0, i, 0, 0)
    const = lambda i: (0, 0)
    tok = lambda a: a.reshape(B, nt, ts, a.shape[-1])
    extras = [] if extra is None else [tok(extra)]
    out = pl.pallas_call(
        functools.partial(_combine_kernel, final_norm=final_norm),
        grid=(nt,),
        in_specs=[
            pl.BlockSpec((None, L, D), lambda i: (i, 0, 0)),
            pl.BlockSpec((B, None, ts, LANES), tile),
            pl.BlockSpec((LANES, L), const),
            pl.BlockSpec((B, None, ts, D), tile),
            pl.BlockSpec((B, None, ts, D), tile),
            pl.BlockSpec((B, 1, D), lambda i: (0, 0, 0)),
            pl.BlockSpec(sg.shape, const),
            pl.BlockSpec(su.shape, const),
            pl.BlockSpec(sd.shape, const),
            pl.BlockSpec((1, D), const),
        ] + [pl.BlockSpec((B, None, ts, D), tile) for _ in extras],
        out_specs=pl.BlockSpec((B, None, ts, D), tile),
        out_shape=jax.ShapeDtypeStruct((B, nt, ts, D), F32),
        compiler_params=_cparams(("arbitrary",)),
        name="moe_combine",
    )(ys, tok(wc), rep, tok(h), tok(x1), gate2.reshape(B, 1, D), sg, su, sd,
      final_g.reshape(1, D), *extras)
    return out.reshape(N, D)


def _overflow_kernel(h_ref, wc_ref, wg_ref, wu_ref, wd_ref, o_ref):
    e = pl.program_id(1)

    @pl.when(e == 0)
    def _():
        o_ref[...] = jnp.zeros_like(o_ref)

    h = h_ref[...]
    lane = lax.broadcasted_iota(jnp.int32, wc_ref.shape, 1)
    w = jnp.sum(jnp.where(lane == e, wc_ref[...], 0.0), axis=1, keepdims=True)
    hid = _silu(jnp.dot(h, wg_ref[0], preferred_element_type=F32)) * jnp.dot(
        h, wu_ref[0], preferred_element_type=F32)
    o_ref[...] += _dot(hid * w, wd_ref[0])


def _overflow(h, wc_over, wg, wu, wd):
    N, D = h.shape
    tm = math.gcd(N, 1024)
    E, _, Hd = wg.shape
    row = lambda i, e: (i, 0)
    return pl.pallas_call(
        _overflow_kernel,
        grid=(N // tm, E),
        in_specs=[
            pl.BlockSpec((tm, D), row),
            pl.BlockSpec((tm, LANES), row),
            pl.BlockSpec((1, D, Hd), lambda i, e: (e, 0, 0)),
            pl.BlockSpec((1, D, Hd), lambda i, e: (e, 0, 0)),
            pl.BlockSpec((1, Hd, D), lambda i, e: (e, 0, 0)),
        ],
        out_specs=pl.BlockSpec((tm, D), row),
        out_shape=jax.ShapeDtypeStruct((N, D), F32),
        compiler_params=_cparams(("arbitrary", "arbitrary")),
        name="moe_overflow",
    )(h, wc_over, wg, wu, wd)


def _moe(h, wc, wg, wu, wd, layer, sg, su, sd, x1, gate2, final_g, B, T, final_norm):
    xs, wc_over, cnt = _dispatch(h, wc, B, T)
    ys = _experts(xs, cnt, wg, wu, wd, layer)
    def finish(extra):
        return _combine(ys, wc, h, x1, extra, gate2, sg, su, sd, final_g, B, T, final_norm)

    def with_overflow():
        return finish(_overflow(h, wc_over, wg[layer].astype(BF16), wu[layer].astype(BF16),
                                wd[layer].astype(BF16)))

    return lax.cond(jnp.any(wc_over != 0.0), with_overflow, lambda: finish(None))


def _split_w_in(w_in):
    w_rwkv = w_in[:, :RWKV_COLS]
    w_ret = w_in[:, RWKV_COLS:RWKV_COLS + RET_COLS]
    w_pool = w_in[:, RWKV_COLS + RET_COLS:]
    return w_rwkv.astype(BF16), w_ret.astype(BF16), w_pool.astype(BF16)


def kernel(x, c, norm1_g, norm2_g, w_ada, b_ada, w_in, w_out, rwkv_mu, rwkv_w_up, rwkv_w0,
           rwkv_a_up, rwkv_a0, rwkv_g_up, rwkv_k_k, rwkv_k_a, rwkv_r_k, rwkv_lnx_w, rwkv_lnx_b,
           pool_w, pool_scale, w_router, router_bias, we_gate, we_up, we_down,
           ws_gate, ws_up, ws_down, final_g):
    B, T, D = x.shape
    L = w_in.shape[0]
    N = B * T
    mod = _adaln(c, w_ada, b_ada)
    x2 = x.reshape(N, D)
    for l in range(L):
        shift1, scale1, gate1, shift2, scale2, gate2 = jnp.split(mod[l], 6, axis=-1)
        w1, w2, w3 = _split_w_in(w_in[l])
        p_rwkv, p_ret, p_pool = _inproj(x2, norm1_g[l], scale1, shift1, w1, w2, w3, T)
        y_rwkv = _rwkv(p_rwkv.reshape(B, T, -1), rwkv_mu[l], rwkv_w_up[l], rwkv_w0[l],
                       rwkv_a_up[l], rwkv_a0[l], rwkv_g_up[l], rwkv_k_k[l], rwkv_k_a[l],
                       rwkv_r_k[l], rwkv_lnx_w[l], rwkv_lnx_b[l])
        y_ret = _retention(p_ret.reshape(B, T, -1))
        y_pool = _pool(p_pool.reshape(B, T, -1), pool_w[l], pool_scale[l])
        x1, h, wc = _outproj(y_rwkv.reshape(N, -1), y_ret.reshape(N, -1), y_pool.reshape(N, -1),
                             x2, w_out[l], gate1, norm2_g[l], scale2, shift2,
                             w_router[l], router_bias[l], T)
        x2 = _moe(h, wc, we_gate, we_up, we_down, l,
                  ws_gate[l].astype(BF16), ws_up[l].astype(BF16),
                  ws_down[l].astype(BF16), x1, gate2, final_g, B, T,
                  final_norm=(l == L - 1))
    return x2.reshape(B, T, D)
```

```python
import functools
import math

import numpy as np
import jax
import jax.numpy as jnp
from jax import lax
from jax.experimental import pallas as pl
from jax.experimental.pallas import tpu as pltpu

F32 = jnp.float32
BF16 = jnp.bfloat16
HIGHEST = lax.Precision.HIGHEST

D_MODEL = 1024
HEADS = 6
HEAD_DIM = 64
WIDTH = HEADS * HEAD_DIM
W_LORA = 64
A_LORA = 64
G_LORA = 128
RWKV_COLS = 3 * WIDTH + W_LORA + A_LORA + G_LORA
RET_COLS = 4 * WIDTH
RET_CHUNK = 128
RWKV_CHUNK = 128
RWKV_SUB = 16
RET_ROWS = 4
RWKV_ROWS = 4
POOL_GROUPS = 4
POOL_GROUP_DIM = 64
POOL_WIDTH = POOL_GROUPS * POOL_GROUP_DIM
POOL_WINDOWS = (2, 4, 8, 16)
N_EXPERTS = 64
TOP_K = 8
N_GROUPS = 8
TOPK_GROUPS = 4
EXPERT_HIDDEN = 256
ROUTED_SCALE = 2.5
NORM_EPS = 1e-6
RWKV_LNX_EPS = 64e-5
RET_NORM_EPS = 1e-6
ROPE_BASE = 10000.0
LANES = 128
MOE_TILE_SEQ = 16
MOE_CAP = 80
MOE_USED_STEPS = (48, 64)
VMEM_LIMIT = 48 * 1024 * 1024


def _cparams(sem):
    return pltpu.CompilerParams(dimension_semantics=sem, vmem_limit_bytes=VMEM_LIMIT)


def _dot(a, b):
    return jnp.dot(a.astype(BF16), b.astype(BF16), preferred_element_type=F32)


def _dot_nt(a, b):
    return lax.dot_general(a.astype(BF16), b.astype(BF16), (((1,), (1,)), ((), ())),
                           preferred_element_type=F32)


def _dot_tn(a, b):
    return lax.dot_general(a.astype(BF16), b.astype(BF16), (((0,), (0,)), ((), ())),
                           preferred_element_type=F32)


def _dot_f32(a, b):
    return jnp.dot(a, b, preferred_element_type=F32, precision=HIGHEST)


def _sigmoid(x):
    return 1.0 / (1.0 + jnp.exp(-x))


def _silu(x):
    return x * _sigmoid(x)


def _head_masks(rows, width):
    lane = lax.broadcasted_iota(jnp.int32, (rows, width), 1)
    return [lane // HEAD_DIM == h for h in range(width // HEAD_DIM)]


def _stack_heads(x, masks):
    return jnp.concatenate([jnp.where(m, x, 0.0) for m in masks], axis=0).astype(BF16)


def _select_heads(stacked, masks, c):
    out = stacked[0:c]
    for h in range(1, len(masks)):
        out = jnp.where(masks[h], stacked[h * c:(h + 1) * c], out)
    return out


def _adaln_kernel(c_ref, w_ref, b_ref, o_ref):
    o_ref[0] = _dot_f32(_silu(c_ref[...]), w_ref[0]) + b_ref[0]


def _adaln(c, w_ada, b_ada):
    L, D, M = w_ada.shape
    B = c.shape[0]
    tn = 1536
    return pl.pallas_call(
        _adaln_kernel,
        grid=(L, M // tn),
        in_specs=[
            pl.BlockSpec((B, D), lambda l, j: (0, 0)),
            pl.BlockSpec((1, D, tn), lambda l, j: (l, 0, j)),
            pl.BlockSpec((1, 1, tn), lambda l, j: (l, 0, j)),
        ],
        out_specs=pl.BlockSpec((1, B, tn), lambda l, j: (l, 0, j)),
        out_shape=jax.ShapeDtypeStruct((L, B, M), F32),
        compiler_params=_cparams(("arbitrary", "arbitrary")),
        name="adaln",
    )(c, w_ada, b_ada.reshape(L, 1, M))


def _modulated_norm(x, g, scale, shift):
    ms = jnp.mean(x * x, axis=-1, keepdims=True)
    return x * lax.rsqrt(ms + NORM_EPS) * g * (1.0 + scale) + shift


def _inproj_kernel(x_ref, g_ref, sc_ref, sh_ref, w1_ref, w2_ref, w3_ref, o1_ref, o2_ref, o3_ref):
    h = _modulated_norm(x_ref[...], g_ref[...], sc_ref[0], sh_ref[0]).astype(BF16)
    o1_ref[...] = jnp.dot(h, w1_ref[...], preferred_element_type=F32)
    o2_ref[...] = jnp.dot(h, w2_ref[...], preferred_element_type=F32)
    o3_ref[...] = jnp.dot(h, w3_ref[...], preferred_element_type=F32)


def _inproj(x2, g, scale, shift, w1, w2, w3, T):
    N, D = x2.shape
    B = N // T
    tm = min(512, T)
    per_b = T // tm
    row = lambda i: (i, 0)
    const = lambda i: (0, 0)
    bvec = lambda i: (i // per_b, 0, 0)
    return pl.pallas_call(
        _inproj_kernel,
        grid=(N // tm,),
        in_specs=[
            pl.BlockSpec((tm, D), row),
            pl.BlockSpec((1, D), const),
            pl.BlockSpec((1, 1, D), bvec),
            pl.BlockSpec((1, 1, D), bvec),
            pl.BlockSpec(w1.shape, const),
            pl.BlockSpec(w2.shape, const),
            pl.BlockSpec(w3.shape, const),
        ],
        out_specs=[
            pl.BlockSpec((tm, w1.shape[1]), row),
            pl.BlockSpec((tm, w2.shape[1]), row),
            pl.BlockSpec((tm, w3.shape[1]), row),
        ],
        out_shape=[
            jax.ShapeDtypeStruct((N, w1.shape[1]), F32),
            jax.ShapeDtypeStruct((N, w2.shape[1]), F32),
            jax.ShapeDtypeStruct((N, w3.shape[1]), F32),
        ],
        compiler_params=_cparams(("arbitrary",)),
        name="inproj",
    )(x2, g.reshape(1, D), scale.reshape(B, 1, D), shift.reshape(B, 1, D), w1, w2, w3)


def _unit_lower_inverse(a3):
    H, C, _ = a3.shape
    ri = lax.broadcasted_iota(jnp.int32, (H, C, C), 1)
    ci = lax.broadcasted_iota(jnp.int32, (H, C, C), 2)
    eye = (ri == ci).astype(F32)
    same = (ri // RWKV_SUB) == (ci // RWKV_SUB)
    dm = jnp.where(same, a3, 0.0)
    off = jnp.where(same, 0.0, a3)

    def bmm(x, y):
        return jnp.einsum('hij,hjk->hik', x.astype(BF16), y.astype(BF16),
                          preferred_element_type=F32)

    d2 = bmm(dm, dm)
    d4 = bmm(d2, d2)
    d8 = bmm(d4, d4)
    x = eye - dm
    x = x + bmm(x, d2)
    x = x + bmm(x, d4)
    x = x + bmm(x, d8)
    n = bmm(x, off)
    y = eye - n
    power = bmm(n, n)
    order = 2
    while order < C // RWKV_SUB:
        y = y + bmm(y, power)
        order *= 2
        if order < C // RWKV_SUB:
            power = bmm(power, power)
    return bmm(y, x)


def _rwkv_prepare(p, carry, prm):
    (mu, wup, w0, aup, a0, gup, k_k, k_a, r_k, lnw, lnb, hsum) = prm
    C = RWKV_CHUNK
    W = WIDTH
    H = HEADS
    row = lax.broadcasted_iota(jnp.int32, (C, 1), 0)
    prev = jnp.where(row == 0, carry, pltpu.roll(p, 1, 0))
    xs = p + (prev - p) * mu

    r = xs[:, 0:W]
    k = xs[:, W:2 * W]
    v = xs[:, 2 * W:3 * W]
    xwa = xs[:, 3 * W:3 * W + W_LORA + A_LORA]
    xg = xs[:, 3 * W + W_LORA + A_LORA:]

    z = w0 + _dot(jnp.tanh(xwa), wup)
    log_w = -math.exp(-0.5) * _sigmoid(z)
    a = _sigmoid(a0 + _dot(xwa, aup))
    g = _dot(_sigmoid(xg), gup)
    kk = k * k_k
    kk = kk / jnp.maximum(jnp.sqrt(_dot(kk * kk, hsum)), 1e-12)
    k = k * (1.0 + (a - 1.0) * k_a)

    ti = lax.broadcasted_iota(jnp.int32, (C, C), 0)
    si = lax.broadcasted_iota(jnp.int32, (C, C), 1)
    tri = jnp.where(ti >= si, 1.0, 0.0).astype(BF16)
    log_w_hi = log_w.astype(BF16)
    cum = (jnp.dot(tri, log_w_hi, preferred_element_type=F32)
           + _dot(tri, log_w - log_w_hi.astype(F32)))
    g_in = jnp.exp(cum)
    g_inv = jnp.exp(-cum)
    g_end = g_in[C - 1:C, :]
    kt = kk * jnp.exp(cum - log_w)
    bt = kk * a * g_inv
    kq = k * g_inv
    rt = r * g_in

    masks = _head_masks(C, W)
    t3 = lax.broadcasted_iota(jnp.int32, (H, C, C), 1)
    s3 = lax.broadcasted_iota(jnp.int32, (H, C, C), 2)
    a_ab = jnp.where(t3 > s3, _dot_nt(_stack_heads(kt, masks), bt).reshape(H, C, C), 0.0)
    bt_rows = _stack_heads(bt, masks)
    kq_rows = _stack_heads(kq, masks)
    t1 = lax.broadcasted_iota(jnp.int32, (C, H * C), 0)
    s1 = lax.broadcasted_iota(jnp.int32, (C, H * C), 1) % C
    a_ak = jnp.where(t1 > s1, _dot_nt(kt, kq_rows), 0.0)
    t2 = lax.broadcasted_iota(jnp.int32, (C, 2 * H * C), 0)
    s2 = lax.broadcasted_iota(jnp.int32, (C, 2 * H * C), 1) % C
    p_r = jnp.where(t2 >= s2, _dot_nt(rt, jnp.concatenate([bt_rows, kq_rows], axis=0)), 0.0)
    return a_ab, (a_ak, p_r, kt, rt, bt, kq, g_end, r, k, v, g)


def _rwkv_finish(vals, t_inv, s0, prm):
    (a_ak, p_r, kt, rt, bt, kq, g_end, r, k, v, g) = vals
    (mu, wup, w0, aup, a0, gup, k_k, k_a, r_k, lnw, lnb, hsum) = prm
    C = RWKV_CHUNK
    W = WIDTH
    H = HEADS
    masks = _head_masks(C, W)
    ks = _dot_nt(kt, s0)
    rs = _dot_nt(rt, s0)
    v_rows = _stack_heads(v, masks)
    av = _dot(a_ak, v_rows)
    u = _select_heads(_dot(t_inv.reshape(H * C, C), -(ks + av)), masks, C)
    y = rs + _dot(p_r, jnp.concatenate([_stack_heads(u, masks), v_rows], axis=0))
    upd = _dot_tn(jnp.concatenate([u, v], axis=0),
                  jnp.concatenate([bt * g_end, kq * g_end], axis=0))
    hi = lax.broadcasted_iota(jnp.int32, (W, W), 0) // HEAD_DIM
    hj = lax.broadcasted_iota(jnp.int32, (W, W), 1) // HEAD_DIM
    s_new = s0 * g_end + jnp.where(hi == hj, upd, 0.0)

    inv_d = 1.0 / HEAD_DIM
    mean = _dot(y, hsum) * inv_d
    yc = y - mean
    var = _dot(yc * yc, hsum) * inv_d
    yn = yc * lax.rsqrt(var + RWKV_LNX_EPS) * lnw + lnb
    bonus = _dot(r * k * r_k, hsum) * v
    return (yn + bonus) * g, s_new


def _rwkv_kernel(p_ref, mu_ref, wup_ref, w0_ref, aup_ref, a0_ref, gup_ref, kk_ref, ka_ref,
                 rk_ref, lnw_ref, lnb_ref, hsum_ref, o_ref, carry_ref, s_ref):
    @pl.when(pl.program_id(1) == 0)
    def _():
        carry_ref[...] = jnp.zeros_like(carry_ref)
        s_ref[...] = jnp.zeros_like(s_ref)

    prm = tuple(ref[...] for ref in (mu_ref, wup_ref, w0_ref, aup_ref, a0_ref, gup_ref, kk_ref,
                                     ka_ref, rk_ref, lnw_ref, lnb_ref, hsum_ref))
    G = p_ref.shape[0]
    H = HEADS
    a_abs, vals = [], []
    for i in range(G):
        p = p_ref[i]
        a_ab, val = _rwkv_prepare(p, carry_ref[i], prm)
        carry_ref[i] = p[RWKV_CHUNK - 1:RWKV_CHUNK, :]
        a_abs.append(a_ab)
        vals.append(val)
    t_inv = _unit_lower_inverse(jnp.concatenate(a_abs, axis=0))
    for i in range(G):
        out, s_new = _rwkv_finish(vals[i], t_inv[i * H:(i + 1) * H], s_ref[i], prm)
        s_ref[i] = s_new
        o_ref[i] = out


def _rwkv(p, mu, wup, w0, aup, a0, gup, k_k, k_a, r_k, lnx_w, lnx_b):
    B, T, _ = p.shape
    C = RWKV_CHUNK
    W = WIDTH
    G = RWKV_ROWS if B % RWKV_ROWS == 0 else 1
    lora_in = W_LORA + A_LORA
    wup_pad = jnp.zeros((lora_in, W), F32).at[:W_LORA].set(wup).astype(BF16)
    aup_pad = jnp.zeros((lora_in, W), F32).at[W_LORA:].set(aup).astype(BF16)
    head = np.arange(W) // HEAD_DIM
    hsum = jnp.asarray((head[:, None] == head[None, :]).astype(np.float32)).astype(BF16)
    vec = lambda a: a.reshape(1, -1)
    const = lambda b, t: (0, 0)
    params = [vec(mu), wup_pad, vec(w0), aup_pad, vec(a0), gup.astype(BF16), vec(k_k), vec(k_a),
              vec(r_k), vec(lnx_w), vec(lnx_b), hsum]
    return pl.pallas_call(
        _rwkv_kernel,
        grid=(B // G, T // C),
        in_specs=[pl.BlockSpec((G, C, RWKV_COLS), lambda b, t: (b, t, 0))]
        + [pl.BlockSpec(a.shape, const) for a in params],
        out_specs=pl.BlockSpec((G, C, W), lambda b, t: (b, t, 0)),
        out_shape=jax.ShapeDtypeStruct((B, T, W), F32),
        scratch_shapes=[pltpu.VMEM((G, 1, RWKV_COLS), F32), pltpu.VMEM((G, W, W), F32)],
        compiler_params=_cparams(("arbitrary", "arbitrary")),
        name="rwkv7",
    )(p, *params)


def _retention_kernel(p_ref, cos_ref, sin_ref, dec_ref, xi_ref, zeta_ref, cd_ref, hsum_ref,
                      o_ref, s_ref):
    C = RET_CHUNK
    W = WIDTH

    @pl.when(pl.program_id(1) == 0)
    def _():
        s_ref[...] = jnp.zeros_like(s_ref)

    cos = cos_ref[...]
    sin = sin_ref[...]
    masks = _head_masks(C, W)
    hi = lax.broadcasted_iota(jnp.int32, (W, W), 0) // HEAD_DIM
    hj = lax.broadcasted_iota(jnp.int32, (W, W), 1) // HEAD_DIM
    first_half = lax.broadcasted_iota(jnp.int32, (C, W), 1) % HEAD_DIM < HEAD_DIM // 2

    def swap_halves(x):
        return jnp.where(first_half, pltpu.roll(x, W - HEAD_DIM // 2, 1),
                         pltpu.roll(x, HEAD_DIM // 2, 1))

    for i in range(p_ref.shape[0]):
        p = p_ref[i]
        q = p[:, 0:W] * cos + swap_halves(p[:, 0:W]) * sin
        k = (p[:, W:2 * W] * cos + swap_halves(p[:, W:2 * W]) * sin) * (HEAD_DIM ** -0.5)
        v = p[:, 2 * W:3 * W]
        gate = p[:, 3 * W:4 * W]
        scores = _dot_nt(q, _stack_heads(k, masks)) * dec_ref[...]
        y = _dot(scores, _stack_heads(v, masks))
        s0 = s_ref[i]
        y = y + _dot(q * xi_ref[...], s0)
        kv = _dot_tn(k * zeta_ref[...], v)
        s_ref[i] = s0 * cd_ref[...] + jnp.where(hi == hj, kv, 0.0)
        ms = _dot(y * y, hsum_ref[...]) * (1.0 / HEAD_DIM)
        o_ref[i] = _silu(gate) * (y * lax.rsqrt(ms + RET_NORM_EPS))


def _retention_tables(T):
    C, H, d = RET_CHUNK, HEADS, HEAD_DIM
    pos = jnp.arange(T, dtype=F32)
    inv_freq = ROPE_BASE ** (-jnp.arange(0, d, 2, dtype=F32) / d)
    ang = pos[:, None] * inv_freq[None, :]
    cos = jnp.cos(ang)
    sin = jnp.sin(ang)
    cos_full = jnp.tile(jnp.concatenate([cos, cos], -1), (1, H))
    sin_full = jnp.tile(jnp.concatenate([-sin, sin], -1), (1, H))
    log_gamma = jnp.log1p(-(2.0 ** (-5.0 - jnp.arange(H, dtype=F32))))
    idx = jnp.arange(C, dtype=F32)
    diff = idx[:, None] - idx[None, :]
    dec = jnp.where(diff >= 0, jnp.exp(log_gamma[:, None, None] * jnp.maximum(diff, 0.0)), 0.0)
    xi = jnp.exp(log_gamma[:, None] * (idx + 1.0))
    zeta = jnp.exp(log_gamma[:, None] * (C - 1.0 - idx))
    cd = jnp.exp(log_gamma * C)
    per_lane = lambda a: jnp.repeat(a.T, d, axis=1)
    return (cos_full, sin_full, dec.transpose(1, 0, 2).reshape(C, H * C), per_lane(xi), per_lane(zeta),
            jnp.repeat(cd, d).reshape(1, H * d))


def _retention(p):
    B, T, cols = p.shape
    C = RET_CHUNK
    W = WIDTH
    cos, sin, dec, xi, zeta, cd = _retention_tables(T)
    head = np.arange(W) // HEAD_DIM
    hsum = jnp.asarray((head[:, None] == head[None, :]).astype(np.float32)).astype(BF16)
    const = lambda b, t: (0, 0)
    G = RET_ROWS if B % RET_ROWS == 0 else 1
    return pl.pallas_call(
        _retention_kernel,
        grid=(B // G, T // C),
        in_specs=[
            pl.BlockSpec((G, C, cols), lambda b, t: (b, t, 0)),
            pl.BlockSpec((C, W), lambda b, t: (t, 0)),
            pl.BlockSpec((C, W), lambda b, t: (t, 0)),
            pl.BlockSpec(dec.shape, const),
            pl.BlockSpec(xi.shape, const),
            pl.BlockSpec(zeta.shape, const),
            pl.BlockSpec(cd.shape, const),
            pl.BlockSpec(hsum.shape, const),
        ],
        out_specs=pl.BlockSpec((G, C, W), lambda b, t: (b, t, 0)),
        out_shape=jax.ShapeDtypeStruct((B, T, W), F32),
        scratch_shapes=[pltpu.VMEM((G, W, W), F32)],
        compiler_params=_cparams(("arbitrary", "arbitrary")),
        name="retention",
    )(p, cos, sin, dec, xi, zeta, cd, hsum)


def _pool_kernel(u_ref, w_ref, scale_ref, o_ref):
    u = u_ref[0]
    T = u.shape[0]
    row = lax.broadcasted_iota(jnp.int32, (T, 1), 0)

    def lag(x, k):
        return jnp.where(row >= k, pltpu.roll(x, k, 0), 0.0)

    s2 = u + lag(u, 1)
    s4 = s2 + lag(s2, 2)
    s8 = s4 + lag(s4, 4)
    s16 = s8 + lag(s8, 8)
    grp = lax.broadcasted_iota(jnp.int32, (1, POOL_WIDTH), 1) // POOL_GROUP_DIM
    s = jnp.where(grp == 0, s2, jnp.where(grp == 1, s4, jnp.where(grp == 2, s8, s16)))
    win = jnp.where(grp == 0, POOL_WINDOWS[0],
                    jnp.where(grp == 1, POOL_WINDOWS[1],
                              jnp.where(grp == 2, POOL_WINDOWS[2], POOL_WINDOWS[3])))
    count = jnp.minimum(row + 1, win).astype(F32)
    pooled = s / count - u
    o_ref[0] = _dot(pooled, w_ref[...]) * scale_ref[...]


def _pool(u, pool_w, pool_scale):
    B, T, Wp = u.shape
    G, d = POOL_GROUPS, POOL_GROUP_DIM
    wbd = jnp.zeros((Wp, Wp), F32)
    for gi in range(G):
        wbd = wbd.at[gi * d:(gi + 1) * d, gi * d:(gi + 1) * d].set(pool_w[gi])
    return pl.pallas_call(
        _pool_kernel,
        grid=(B,),
        in_specs=[
            pl.BlockSpec((1, T, Wp), lambda b: (b, 0, 0)),
            pl.BlockSpec((Wp, Wp), lambda b: (0, 0)),
            pl.BlockSpec((1, Wp), lambda b: (0, 0)),
        ],
        out_specs=pl.BlockSpec((1, T, Wp), lambda b: (b, 0, 0)),
        out_shape=jax.ShapeDtypeStruct((B, T, Wp), F32),
        compiler_params=_cparams(("arbitrary",)),
        name="pool",
    )(u, wbd.astype(BF16), pool_scale.reshape(1, Wp))


def _route(logits_t, bias_col):
    E, tm = logits_t.shape
    per_group = E // N_GROUPS
    neg_inf = -jnp.inf
    scores = _sigmoid(logits_t)
    choice = scores + bias_col
    c3 = choice.reshape(N_GROUPS, per_group, tm)
    sub = lax.broadcasted_iota(jnp.int32, c3.shape, 1)
    m1 = jnp.max(c3, axis=1, keepdims=True)
    first = jnp.min(jnp.where(c3 == m1, sub, per_group), axis=1, keepdims=True)
    m2 = jnp.max(jnp.where(sub == first, neg_inf, c3), axis=1, keepdims=True)
    gs = m1 + m2
    gidx = lax.broadcasted_iota(jnp.int32, gs.shape, 0)
    grank = jnp.zeros(gs.shape, jnp.int32)
    for j in range(N_GROUPS):
        other = gs[j:j + 1]
        ahead = jnp.where(other > gs, 1, jnp.where((other == gs) & (gidx > j), 1, 0))
        grank = grank + ahead
    gmask = jnp.broadcast_to(grank < TOPK_GROUPS, c3.shape)
    masked = jnp.where(gmask, c3, neg_inf).reshape(E, tm)
    eidx = lax.broadcasted_iota(jnp.int32, (E, tm), 0)
    top = jnp.zeros((E, tm), F32)
    for _ in range(TOP_K):
        best = jnp.max(masked, axis=0, keepdims=True)
        first = jnp.min(jnp.where(masked == best, eidx, E), axis=0, keepdims=True)
        hit = eidx == first
        top = jnp.where(hit, scores, top)
        masked = jnp.where(hit, neg_inf, masked)
    return top / jnp.sum(top, axis=0, keepdims=True) * ROUTED_SCALE


def _outproj_kernel(yr_ref, yt_ref, yp_ref, x_ref, w1_ref, w2_ref, w3_ref, g1_ref, ng_ref,
                    sc_ref, sh_ref, wr_ref, rb_ref, x1_ref, h_ref, wc_ref):
    mixed = (_dot(yr_ref[...], w1_ref[...]) + _dot(yt_ref[...], w2_ref[...])
             + _dot(yp_ref[...], w3_ref[...]))
    x1 = x_ref[...] + g1_ref[0] * mixed
    x1_ref[...] = x1
    h = _modulated_norm(x1, ng_ref[...], sc_ref[0], sh_ref[0])
    h_ref[...] = h.astype(BF16)
    logits_t = lax.dot_general(wr_ref[...], h, (((1,), (1,)), ((), ())),
                               preferred_element_type=F32, precision=HIGHEST)
    wc_t = _route(logits_t, rb_ref[...])
    pad = jnp.zeros((LANES - N_EXPERTS, wc_t.shape[1]), F32)
    wc_ref[...] = jnp.concatenate([wc_t, pad], axis=0).T


def _outproj(yr, yt, yp, x2, w_out, gate1, ng, scale2, shift2, w_router, router_bias, T):
    N, D = x2.shape
    B = N // T
    tm = min(1024, T)
    per_b = T // tm
    W = WIDTH
    w1 = w_out[:W].astype(BF16)
    w2 = w_out[W:2 * W].astype(BF16)
    w3 = w_out[2 * W:].astype(BF16)
    row = lambda i: (i, 0)
    const = lambda i: (0, 0)
    bvec = lambda i: (i // per_b, 0, 0)
    return pl.pallas_call(
        _outproj_kernel,
        grid=(N // tm,),
        in_specs=[
            pl.BlockSpec((tm, W), row),
            pl.BlockSpec((tm, W), row),
            pl.BlockSpec((tm, POOL_WIDTH), row),
            pl.BlockSpec((tm, D), row),
            pl.BlockSpec(w1.shape, const),
            pl.BlockSpec(w2.shape, const),
            pl.BlockSpec(w3.shape, const),
            pl.BlockSpec((1, 1, D), bvec),
            pl.BlockSpec((1, D), const),
            pl.BlockSpec((1, 1, D), bvec),
            pl.BlockSpec((1, 1, D), bvec),
            pl.BlockSpec((N_EXPERTS, D), const),
            pl.BlockSpec((N_EXPERTS, 1), const),
        ],
        out_specs=[
            pl.BlockSpec((tm, D), row),
            pl.BlockSpec((tm, D), row),
            pl.BlockSpec((tm, LANES), row),
        ],
        out_shape=[
            jax.ShapeDtypeStruct((N, D), F32),
            jax.ShapeDtypeStruct((N, D), BF16),
            jax.ShapeDtypeStruct((N, LANES), F32),
        ],
        compiler_params=_cparams(("arbitrary",)),
        name="outproj_router",
    )(yr, yt, yp, x2, w1, w2, w3, gate1.reshape(B, 1, D), ng.reshape(1, D),
      scale2.reshape(B, 1, D), shift2.reshape(B, 1, D), w_router.T,
      router_bias.reshape(N_EXPERTS, 1))


def _slot_positions(wc):
    tm = wc.shape[0]
    sel = jnp.where(wc > 0.0, 1.0, 0.0).astype(BF16)
    ti = lax.broadcasted_iota(jnp.int32, (tm, tm), 0)
    si = lax.broadcasted_iota(jnp.int32, (tm, tm), 1)
    earlier = jnp.where(ti > si, 1.0, 0.0).astype(BF16)
    return sel, jnp.dot(earlier, sel, preferred_element_type=F32)


def _slot_rows(code_t, value_t, tm):
    slot = lax.broadcasted_iota(jnp.int32, (MOE_CAP, tm), 0).astype(F32)
    rows = []
    for e in range(N_EXPERTS):
        value = 1.0 if value_t is None else value_t[e:e + 1, :]
        rows.append(jnp.where(code_t[e:e + 1, :] == slot, value, 0.0).astype(BF16))
    return jnp.concatenate(rows, axis=0)


def _dispatch_kernel(h_ref, wc_ref, xs_ref, over_ref, cnt_ref):
    nb, ts, D = h_ref.shape
    tm = nb * ts
    wc = wc_ref[...].reshape(tm, LANES)
    sel, pos = _slot_positions(wc)
    over_ref[...] = jnp.where(pos >= MOE_CAP, wc, 0.0).reshape(nb, ts, LANES)
    cnt_ref[...] = jnp.sum(sel.astype(F32), axis=0, keepdims=True)
    code = jnp.where(wc > 0.0, pos, -1.0)
    onehot = _slot_rows(code.T, None, tm)
    xs_ref[...] = jnp.dot(onehot, h_ref[...].reshape(tm, D),
                          preferred_element_type=F32).astype(BF16)


def _dispatch(h, wc, B, T):
    N, D = h.shape
    ts = MOE_TILE_SEQ
    nt = T // ts
    L = N_EXPERTS * MOE_CAP
    tile = lambda i: (0, i, 0, 0)
    xs, over, cnt = pl.pallas_call(
        _dispatch_kernel,
        grid=(nt,),
        in_specs=[
            pl.BlockSpec((B, None, ts, D), tile),
            pl.BlockSpec((B, None, ts, LANES), tile),
        ],
        out_specs=[
            pl.BlockSpec((None, L, D), lambda i: (i, 0, 0)),
            pl.BlockSpec((B, None, ts, LANES), tile),
            pl.BlockSpec((None, 1, LANES), lambda i: (i, 0, 0)),
        ],
        out_shape=[
            jax.ShapeDtypeStruct((nt, L, D), BF16),
            jax.ShapeDtypeStruct((B, nt, ts, LANES), F32),
            jax.ShapeDtypeStruct((nt, 1, LANES), F32),
        ],
        compiler_params=_cparams(("arbitrary",)),
        name="moe_dispatch",
    )(h.reshape(B, nt, ts, D), wc.reshape(B, nt, ts, LANES))
    return xs, over.reshape(N, LANES), cnt.reshape(nt, LANES)[:, :N_EXPERTS]


def _slot_pieces(cap):
    bounds = [0] + [n for n in MOE_USED_STEPS if n < cap] + [cap]
    return list(zip(bounds[:-1], bounds[1:]))


def _experts_kernel(used_ref, *refs):
    pieces = _slot_pieces(refs[-4].shape[1])
    x_refs = refs[:len(pieces)]
    wg_ref, wu_ref, wd_ref, o_ref, wg_s, wu_s, wd_s = refs[len(pieces):]

    @pl.when(pl.program_id(1) == 0)
    def _():
        wg_s[...] = wg_ref[0, 0].astype(BF16)
        wu_s[...] = wu_ref[0, 0].astype(BF16)
        wd_s[...] = wd_ref[0, 0].astype(BF16)

    tg, cap, D = o_ref.shape
    used = used_ref[pl.program_id(0), pl.program_id(1)]

    def run(k):
        x = jnp.concatenate([x_refs[j][...].reshape(tg * (hi - lo), D)
                             for j, (lo, hi) in enumerate(pieces[:k + 1])], axis=0)
        hid = _silu(jnp.dot(x, wg_s[...], preferred_element_type=F32)) * jnp.dot(
            x, wu_s[...], preferred_element_type=F32)
        y = _dot(hid, wd_s[...]).astype(BF16)
        for lo, hi in pieces[:k + 1]:
            o_ref[:, lo:hi, :] = y[tg * lo:tg * hi].reshape(tg, hi - lo, D)
        if pieces[k][1] < cap:
            o_ref[:, pieces[k][1]:cap, :] = jnp.zeros((tg, cap - pieces[k][1], D), BF16)

    for k, (lo, hi) in enumerate(pieces):
        pl.when((used > lo) & (used <= hi) if hi < cap else used > lo)(functools.partial(run, k))

    @pl.when(used == 0)
    def _():
        o_ref[...] = jnp.zeros_like(o_ref)


def _experts(xs, cnt, wg, wu, wd, layer):
    nt, L, D = xs.shape
    _, E, _, Hd = wg.shape
    cap = L // E
    tg = math.gcd(nt, 16)
    used = jnp.minimum(jnp.max(cnt.reshape(nt // tg, tg, E), axis=1), cap).T.astype(jnp.int32)
    slots = pl.BlockSpec((tg, None, cap, D), lambda e, g, u: (g, e, 0, 0))

    def piece_spec(lo, hi):
        assert lo % (hi - lo) == 0

        def index(e, g, u):
            live = u[e, g] > lo
            return (jnp.where(live, g, 0), jnp.where(live, e, 0), lo // (hi - lo), 0)

        return pl.BlockSpec((tg, None, hi - lo, D), index)

    pieces = _slot_pieces(cap)
    xs4 = xs.reshape(nt, E, cap, D)
    grid_spec = pltpu.PrefetchScalarGridSpec(
        num_scalar_prefetch=1,
        grid=(E, nt // tg),
        in_specs=[piece_spec(lo, hi) for lo, hi in pieces] + [
            pl.BlockSpec((1, 1, D, Hd), lambda e, g, u: (layer, e, 0, 0)),
            pl.BlockSpec((1, 1, D, Hd), lambda e, g, u: (layer, e, 0, 0)),
            pl.BlockSpec((1, 1, Hd, D), lambda e, g, u: (layer, e, 0, 0)),
        ],
        out_specs=slots,
        scratch_shapes=[pltpu.VMEM((D, Hd), BF16), pltpu.VMEM((D, Hd), BF16),
                        pltpu.VMEM((Hd, D), BF16)],
    )
    return pl.pallas_call(
        _experts_kernel,
        grid_spec=grid_spec,
        out_shape=jax.ShapeDtypeStruct((nt, E, cap, D), BF16),
        compiler_params=_cparams(("arbitrary", "arbitrary")),
        name="moe_experts",
    )(used, *([xs4] * len(pieces)), wg, wu, wd).reshape(nt, L, D)


def _combine_kernel(y_ref, wc_ref, h_ref, x1_ref, g2_ref, sg_ref, su_ref, sd_ref, fg_ref,
                    *rest, final_norm):
    ex_ref = rest[0] if len(rest) == 2 else None
    o_ref = rest[-1]
    nb, ts, D = h_ref.shape
    tm = nb * ts
    wc = wc_ref[...].reshape(tm, LANES)
    _, pos = _slot_positions(wc)
    code = jnp.where(wc > 0.0, pos, -1.0)
    weighted = _slot_rows(code.T, wc.T, tm)
    routed = _dot_tn(weighted, y_ref[...])
    h = h_ref[...].reshape(tm, D)
    hid = _silu(jnp.dot(h, sg_ref[...], preferred_element_type=F32)) * jnp.dot(
        h, su_ref[...], preferred_element_type=F32)
    y = routed + _dot(hid, sd_ref[...])
    if ex_ref is not None:
        y = y + ex_ref[...].reshape(tm, D)
    gate = jnp.broadcast_to(g2_ref[...], (nb, ts, D)).reshape(tm, D)
    xo = x1_ref[...].reshape(tm, D) + gate * y
    if final_norm:
        ms = jnp.mean(xo * xo, axis=-1, keepdims=True)
        xo = xo * lax.rsqrt(ms + NORM_EPS) * fg_ref[...]
    o_ref[...] = xo.reshape(nb, ts, D)


def _combine(ys, wc, h, x1, extra, gate2, sg, su, sd, final_g, B, T, final_norm):
    N, D = x1.shape
    ts = MOE_TILE_SEQ
    nt = T // ts
    L = N_EXPERTS * MOE_CAP
    tile = lambda i: (0, i, 0, 0)
    const = lambda i: (0, 0)
    tok = lambda a: a.reshape(B, nt, ts, a.shape[-1])
    extras = [] if extra is None else [tok(extra)]
    out = pl.pallas_call(
        functools.partial(_combine_kernel, final_norm=final_norm),
        grid=(nt,),
        in_specs=[
            pl.BlockSpec((None, L, D), lambda i: (i, 0, 0)),
            pl.BlockSpec((B, None, ts, LANES), tile),
            pl.BlockSpec((B, None, ts, D), tile),
            pl.BlockSpec((B, None, ts, D), tile),
            pl.BlockSpec((B, 1, D), lambda i: (0, 0, 0)),
            pl.BlockSpec(sg.shape, const),
            pl.BlockSpec(su.shape, const),
            pl.BlockSpec(sd.shape, const),
            pl.BlockSpec((1, D), const),
        ] + [pl.BlockSpec((B, None, ts, D), tile) for _ in extras],
        out_specs=pl.BlockSpec((B, None, ts, D), tile),
        out_shape=jax.ShapeDtypeStruct((B, nt, ts, D), F32),
        compiler_params=_cparams(("arbitrary",)),
        name="moe_combine",
    )(ys, tok(wc), tok(h), tok(x1), gate2.reshape(B, 1, D), sg, su, sd,
      final_g.reshape(1, D), *extras)
    return out.reshape(N, D)


def _overflow_kernel(h_ref, wc_ref, wg_ref, wu_ref, wd_ref, o_ref):
    e = pl.program_id(1)

    @pl.when(e == 0)
    def _():
        o_ref[...] = jnp.zeros_like(o_ref)

    h = h_ref[...]
    lane = lax.broadcasted_iota(jnp.int32, wc_ref.shape, 1)
    w = jnp.sum(jnp.where(lane == e, wc_ref[...], 0.0), axis=1, keepdims=True)
    hid = _silu(jnp.dot(h, wg_ref[0], preferred_element_type=F32)) * jnp.dot(
        h, wu_ref[0], preferred_element_type=F32)
    o_ref[...] += _dot(hid * w, wd_ref[0])


def _overflow(h, wc_over, wg, wu, wd):
    N, D = h.shape
    tm = math.gcd(N, 1024)
    E, _, Hd = wg.shape
    row = lambda i, e: (i, 0)
    return pl.pallas_call(
        _overflow_kernel,
        grid=(N // tm, E),
        in_specs=[
            pl.BlockSpec((tm, D), row),
            pl.BlockSpec((tm, LANES), row),
            pl.BlockSpec((1, D, Hd), lambda i, e: (e, 0, 0)),
            pl.BlockSpec((1, D, Hd), lambda i, e: (e, 0, 0)),
            pl.BlockSpec((1, Hd, D), lambda i, e: (e, 0, 0)),
        ],
        out_specs=pl.BlockSpec((tm, D), row),
        out_shape=jax.ShapeDtypeStruct((N, D), F32),
        compiler_params=_cparams(("arbitrary", "arbitrary")),
        name="moe_overflow",
    )(h, wc_over, wg, wu, wd)


def _moe(h, wc, wg, wu, wd, layer, sg, su, sd, x1, gate2, final_g, B, T, final_norm):
    xs, wc_over, cnt = _dispatch(h, wc, B, T)
    ys = _experts(xs, cnt, wg, wu, wd, layer)
    def finish(extra):
        return _combine(ys, wc, h, x1, extra, gate2, sg, su, sd, final_g, B, T, final_norm)

    def with_overflow():
        return finish(_overflow(h, wc_over, wg[layer].astype(BF16), wu[layer].astype(BF16),
                                wd[layer].astype(BF16)))

    return lax.cond(jnp.any(wc_over != 0.0), with_overflow, lambda: finish(None))


def _split_w_in(w_in):
    w_rwkv = w_in[:, :RWKV_COLS]
    w_ret = w_in[:, RWKV_COLS:RWKV_COLS + RET_COLS]
    w_pool = w_in[:, RWKV_COLS + RET_COLS:]
    return w_rwkv.astype(BF16), w_ret.astype(BF16), w_pool.astype(BF16)


def kernel(x, c, norm1_g, norm2_g, w_ada, b_ada, w_in, w_out, rwkv_mu, rwkv_w_up, rwkv_w0,
           rwkv_a_up, rwkv_a0, rwkv_g_up, rwkv_k_k, rwkv_k_a, rwkv_r_k, rwkv_lnx_w, rwkv_lnx_b,
           pool_w, pool_scale, w_router, router_bias, we_gate, we_up, we_down,
           ws_gate, ws_up, ws_down, final_g):
    B, T, D = x.shape
    L = w_in.shape[0]
    N = B * T
    mod = _adaln(c, w_ada, b_ada)
    x2 = x.reshape(N, D)
    for l in range(L):
        shift1, scale1, gate1, shift2, scale2, gate2 = jnp.split(mod[l], 6, axis=-1)
        w1, w2, w3 = _split_w_in(w_in[l])
        p_rwkv, p_ret, p_pool = _inproj(x2, norm1_g[l], scale1, shift1, w1, w2, w3, T)
        y_rwkv = _rwkv(p_rwkv.reshape(B, T, -1), rwkv_mu[l], rwkv_w_up[l], rwkv_w0[l],
                       rwkv_a_up[l], rwkv_a0[l], rwkv_g_up[l], rwkv_k_k[l], rwkv_k_a[l],
                       rwkv_r_k[l], rwkv_lnx_w[l], rwkv_lnx_b[l])
        y_ret = _retention(p_ret.reshape(B, T, -1))
        y_pool = _pool(p_pool.reshape(B, T, -1), pool_w[l], pool_scale[l])
        x1, h, wc = _outproj(y_rwkv.reshape(N, -1), y_ret.reshape(N, -1), y_pool.reshape(N, -1),
                             x2, w_out[l], gate1, norm2_g[l], scale2, shift2,
                             w_router[l], router_bias[l], T)
        x2 = _moe(h, wc, we_gate, we_up, we_down, l,
                  ws_gate[l].astype(BF16), ws_up[l].astype(BF16),
                  ws_down[l].astype(BF16), x1, gate2, final_g, B, T,
                  final_norm=(l == L - 1))
    return x2.reshape(B, T, D)
```

```python
import functools
import math

import numpy as np
import jax
import jax.numpy as jnp
from jax import lax
from jax.experimental import pallas as pl
from jax.experimental.pallas import tpu as pltpu

F32 = jnp.float32
BF16 = jnp.bfloat16
HIGHEST = lax.Precision.HIGHEST

D_MODEL = 1024
HEADS = 6
HEAD_DIM = 64
WIDTH = HEADS * HEAD_DIM
W_LORA = 64
A_LORA = 64
G_LORA = 128
RWKV_COLS = 3 * WIDTH + W_LORA + A_LORA + G_LORA
RET_COLS = 4 * WIDTH
RET_CHUNK = 128
RWKV_CHUNK = 128
RWKV_SUB = 16
RET_ROWS = 4
RWKV_ROWS = 4
POOL_GROUPS = 4
POOL_GROUP_DIM = 64
POOL_WIDTH = POOL_GROUPS * POOL_GROUP_DIM
POOL_WINDOWS = (2, 4, 8, 16)
N_EXPERTS = 64
TOP_K = 8
N_GROUPS = 8
TOPK_GROUPS = 4
EXPERT_HIDDEN = 256
ROUTED_SCALE = 2.5
NORM_EPS = 1e-6
RWKV_LNX_EPS = 64e-5
RET_NORM_EPS = 1e-6
ROPE_BASE = 10000.0
LANES = 128
MOE_TILE_SEQ = 16
MOE_CAP = 80
MOE_USED_STEPS = (32, 48, 64)
VMEM_LIMIT = 48 * 1024 * 1024


def _cparams(sem):
    return pltpu.CompilerParams(dimension_semantics=sem, vmem_limit_bytes=VMEM_LIMIT)


def _dot(a, b):
    return jnp.dot(a.astype(BF16), b.astype(BF16), preferred_element_type=F32)


def _dot_nt(a, b):
    return lax.dot_general(a.astype(BF16), b.astype(BF16), (((1,), (1,)), ((), ())),
                           preferred_element_type=F32)


def _dot_tn(a, b):
    return lax.dot_general(a.astype(BF16), b.astype(BF16), (((0,), (0,)), ((), ())),
                           preferred_element_type=F32)


def _dot_f32(a, b):
    return jnp.dot(a, b, preferred_element_type=F32, precision=HIGHEST)


def _sigmoid(x):
    return 1.0 / (1.0 + jnp.exp(-x))


def _silu(x):
    return x * _sigmoid(x)


def _head_masks(rows, width):
    lane = lax.broadcasted_iota(jnp.int32, (rows, width), 1)
    return [lane // HEAD_DIM == h for h in range(width // HEAD_DIM)]


def _stack_heads(x, masks):
    return jnp.concatenate([jnp.where(m, x, 0.0) for m in masks], axis=0).astype(BF16)


def _select_heads(stacked, masks, c):
    out = stacked[0:c]
    for h in range(1, len(masks)):
        out = jnp.where(masks[h], stacked[h * c:(h + 1) * c], out)
    return out


def _adaln_kernel(c_ref, w_ref, b_ref, o_ref):
    o_ref[0] = _dot_f32(_silu(c_ref[...]), w_ref[0]) + b_ref[0]


def _adaln(c, w_ada, b_ada):
    L, D, M = w_ada.shape
    B = c.shape[0]
    tn = 1536
    return pl.pallas_call(
        _adaln_kernel,
        grid=(L, M // tn),
        in_specs=[
            pl.BlockSpec((B, D), lambda l, j: (0, 0)),
            pl.BlockSpec((1, D, tn), lambda l, j: (l, 0, j)),
            pl.BlockSpec((1, 1, tn), lambda l, j: (l, 0, j)),
        ],
        out_specs=pl.BlockSpec((1, B, tn), lambda l, j: (l, 0, j)),
        out_shape=jax.ShapeDtypeStruct((L, B, M), F32),
        compiler_params=_cparams(("arbitrary", "arbitrary")),
        name="adaln",
    )(c, w_ada, b_ada.reshape(L, 1, M))


def _modulated_norm(x, g, scale, shift):
    ms = jnp.mean(x * x, axis=-1, keepdims=True)
    return x * lax.rsqrt(ms + NORM_EPS) * g * (1.0 + scale) + shift


def _inproj_kernel(x_ref, g_ref, sc_ref, sh_ref, w1_ref, w2_ref, w3_ref, o1_ref, o2_ref, o3_ref):
    h = _modulated_norm(x_ref[...], g_ref[...], sc_ref[0], sh_ref[0]).astype(BF16)
    o1_ref[...] = jnp.dot(h, w1_ref[...], preferred_element_type=F32)
    o2_ref[...] = jnp.dot(h, w2_ref[...], preferred_element_type=F32)
    o3_ref[...] = jnp.dot(h, w3_ref[...], preferred_element_type=F32)


def _inproj(x2, g, scale, shift, w1, w2, w3, T):
    N, D = x2.shape
    B = N // T
    tm = min(512, T)
    per_b = T // tm
    row = lambda i: (i, 0)
    const = lambda i: (0, 0)
    bvec = lambda i: (i // per_b, 0, 0)
    return pl.pallas_call(
        _inproj_kernel,
        grid=(N // tm,),
        in_specs=[
            pl.BlockSpec((tm, D), row),
            pl.BlockSpec((1, D), const),
            pl.BlockSpec((1, 1, D), bvec),
            pl.BlockSpec((1, 1, D), bvec),
            pl.BlockSpec(w1.shape, const),
            pl.BlockSpec(w2.shape, const),
            pl.BlockSpec(w3.shape, const),
        ],
        out_specs=[
            pl.BlockSpec((tm, w1.shape[1]), row),
            pl.BlockSpec((tm, w2.shape[1]), row),
            pl.BlockSpec((tm, w3.shape[1]), row),
        ],
        out_shape=[
            jax.ShapeDtypeStruct((N, w1.shape[1]), F32),
            jax.ShapeDtypeStruct((N, w2.shape[1]), F32),
            jax.ShapeDtypeStruct((N, w3.shape[1]), F32),
        ],
        compiler_params=_cparams(("arbitrary",)),
        name="inproj",
    )(x2, g.reshape(1, D), scale.reshape(B, 1, D), shift.reshape(B, 1, D), w1, w2, w3)


def _unit_lower_inverse(a3):
    H, C, _ = a3.shape
    ri = lax.broadcasted_iota(jnp.int32, (H, C, C), 1)
    ci = lax.broadcasted_iota(jnp.int32, (H, C, C), 2)
    eye = (ri == ci).astype(F32)
    same = (ri // RWKV_SUB) == (ci // RWKV_SUB)
    dm = jnp.where(same, a3, 0.0)
    off = jnp.where(same, 0.0, a3)

    def bmm(x, y):
        return jnp.einsum('hij,hjk->hik', x.astype(BF16), y.astype(BF16),
                          preferred_element_type=F32)

    d2 = bmm(dm, dm)
    d4 = bmm(d2, d2)
    d8 = bmm(d4, d4)
    x = eye - dm
    x = x + bmm(x, d2)
    x = x + bmm(x, d4)
    x = x + bmm(x, d8)
    n = bmm(x, off)
    y = eye - n
    power = bmm(n, n)
    order = 2
    while order < C // RWKV_SUB:
        y = y + bmm(y, power)
        order *= 2
        if order < C // RWKV_SUB:
            power = bmm(power, power)
    return bmm(y, x)


def _rwkv_prepare(p, carry, prm):
    (mu, wup, w0, aup, a0, gup, k_k, k_a, r_k, lnw, lnb, hsum) = prm
    C = RWKV_CHUNK
    W = WIDTH
    H = HEADS
    row = lax.broadcasted_iota(jnp.int32, (C, 1), 0)
    prev = jnp.where(row == 0, carry, pltpu.roll(p, 1, 0))
    xs = p + (prev - p) * mu

    r = xs[:, 0:W]
    k = xs[:, W:2 * W]
    v = xs[:, 2 * W:3 * W]
    xwa = xs[:, 3 * W:3 * W + W_LORA + A_LORA]
    xg = xs[:, 3 * W + W_LORA + A_LORA:]

    z = w0 + _dot(jnp.tanh(xwa), wup)
    log_w = -math.exp(-0.5) * _sigmoid(z)
    a = _sigmoid(a0 + _dot(xwa, aup))
    g = _dot(_sigmoid(xg), gup)
    kk = k * k_k
    kk = kk / jnp.maximum(jnp.sqrt(_dot(kk * kk, hsum)), 1e-12)
    k = k * (1.0 + (a - 1.0) * k_a)

    ti = lax.broadcasted_iota(jnp.int32, (C, C), 0)
    si = lax.broadcasted_iota(jnp.int32, (C, C), 1)
    tri = jnp.where(ti >= si, 1.0, 0.0).astype(BF16)
    log_w_hi = log_w.astype(BF16)
    cum = (jnp.dot(tri, log_w_hi, preferred_element_type=F32)
           + _dot(tri, log_w - log_w_hi.astype(F32)))
    g_in = jnp.exp(cum)
    g_inv = jnp.exp(-cum)
    g_end = g_in[C - 1:C, :]
    kt = kk * jnp.exp(cum - log_w)
    bt = kk * a * g_inv
    kq = k * g_inv
    rt = r * g_in

    masks = _head_masks(C, W)
    t3 = lax.broadcasted_iota(jnp.int32, (H, C, C), 1)
    s3 = lax.broadcasted_iota(jnp.int32, (H, C, C), 2)
    a_ab = jnp.where(t3 > s3, _dot_nt(_stack_heads(kt, masks), bt).reshape(H, C, C), 0.0)
    bt_rows = _stack_heads(bt, masks)
    kq_rows = _stack_heads(kq, masks)
    t1 = lax.broadcasted_iota(jnp.int32, (C, H * C), 0)
    s1 = lax.broadcasted_iota(jnp.int32, (C, H * C), 1) % C
    a_ak = jnp.where(t1 > s1, _dot_nt(kt, kq_rows), 0.0)
    t2 = lax.broadcasted_iota(jnp.int32, (C, 2 * H * C), 0)
    s2 = lax.broadcasted_iota(jnp.int32, (C, 2 * H * C), 1) % C
    p_r = jnp.where(t2 >= s2, _dot_nt(rt, jnp.concatenate([bt_rows, kq_rows], axis=0)), 0.0)
    return a_ab, (a_ak, p_r, kt, rt, bt, kq, g_end, r, k, v, g)


def _rwkv_finish(vals, t_inv, s0, prm):
    (a_ak, p_r, kt, rt, bt, kq, g_end, r, k, v, g) = vals
    (mu, wup, w0, aup, a0, gup, k_k, k_a, r_k, lnw, lnb, hsum) = prm
    C = RWKV_CHUNK
    W = WIDTH
    H = HEADS
    masks = _head_masks(C, W)
    ks = _dot_nt(kt, s0)
    rs = _dot_nt(rt, s0)
    v_rows = _stack_heads(v, masks)
    av = _dot(a_ak, v_rows)
    u = _select_heads(_dot(t_inv.reshape(H * C, C), -(ks + av)), masks, C)
    y = rs + _dot(p_r, jnp.concatenate([_stack_heads(u, masks), v_rows], axis=0))
    upd = _dot_tn(jnp.concatenate([u, v], axis=0),
                  jnp.concatenate([bt * g_end, kq * g_end], axis=0))
    hi = lax.broadcasted_iota(jnp.int32, (W, W), 0) // HEAD_DIM
    hj = lax.broadcasted_iota(jnp.int32, (W, W), 1) // HEAD_DIM
    s_new = s0 * g_end + jnp.where(hi == hj, upd, 0.0)

    inv_d = 1.0 / HEAD_DIM
    mean = _dot(y, hsum) * inv_d
    yc = y - mean
    var = _dot(yc * yc, hsum) * inv_d
    yn = yc * lax.rsqrt(var + RWKV_LNX_EPS) * lnw + lnb
    bonus = _dot(r * k * r_k, hsum) * v
    return (yn + bonus) * g, s_new


def _rwkv_kernel(p_ref, mu_ref, wup_ref, w0_ref, aup_ref, a0_ref, gup_ref, kk_ref, ka_ref,
                 rk_ref, lnw_ref, lnb_ref, hsum_ref, o_ref, carry_ref, s_ref):
    @pl.when(pl.program_id(1) == 0)
    def _():
        carry_ref[...] = jnp.zeros_like(carry_ref)
        s_ref[...] = jnp.zeros_like(s_ref)

    prm = tuple(ref[...] for ref in (mu_ref, wup_ref, w0_ref, aup_ref, a0_ref, gup_ref, kk_ref,
                                     ka_ref, rk_ref, lnw_ref, lnb_ref, hsum_ref))
    G = p_ref.shape[0]
    H = HEADS
    a_abs, vals = [], []
    for i in range(G):
        p = p_ref[i]
        a_ab, val = _rwkv_prepare(p, carry_ref[i], prm)
        carry_ref[i] = p[RWKV_CHUNK - 1:RWKV_CHUNK, :]
        a_abs.append(a_ab)
        vals.append(val)
    t_inv = _unit_lower_inverse(jnp.concatenate(a_abs, axis=0))
    for i in range(G):
        out, s_new = _rwkv_finish(vals[i], t_inv[i * H:(i + 1) * H], s_ref[i], prm)
        s_ref[i] = s_new
        o_ref[i] = out


def _rwkv(p, mu, wup, w0, aup, a0, gup, k_k, k_a, r_k, lnx_w, lnx_b):
    B, T, _ = p.shape
    C = RWKV_CHUNK
    W = WIDTH
    G = RWKV_ROWS if B % RWKV_ROWS == 0 else 1
    lora_in = W_LORA + A_LORA
    wup_pad = jnp.zeros((lora_in, W), F32).at[:W_LORA].set(wup).astype(BF16)
    aup_pad = jnp.zeros((lora_in, W), F32).at[W_LORA:].set(aup).astype(BF16)
    head = np.arange(W) // HEAD_DIM
    hsum = jnp.asarray((head[:, None] == head[None, :]).astype(np.float32)).astype(BF16)
    vec = lambda a: a.reshape(1, -1)
    const = lambda b, t: (0, 0)
    params = [vec(mu), wup_pad, vec(w0), aup_pad, vec(a0), gup.astype(BF16), vec(k_k), vec(k_a),
              vec(r_k), vec(lnx_w), vec(lnx_b), hsum]
    return pl.pallas_call(
        _rwkv_kernel,
        grid=(B // G, T // C),
        in_specs=[pl.BlockSpec((G, C, RWKV_COLS), lambda b, t: (b, t, 0))]
        + [pl.BlockSpec(a.shape, const) for a in params],
        out_specs=pl.BlockSpec((G, C, W), lambda b, t: (b, t, 0)),
        out_shape=jax.ShapeDtypeStruct((B, T, W), F32),
        scratch_shapes=[pltpu.VMEM((G, 1, RWKV_COLS), F32), pltpu.VMEM((G, W, W), F32)],
        compiler_params=_cparams(("arbitrary", "arbitrary")),
        name="rwkv7",
    )(p, *params)


def _retention_kernel(p_ref, cos_ref, sin_ref, dec_ref, xi_ref, zeta_ref, cd_ref, hsum_ref,
                      o_ref, s_ref):
    C = RET_CHUNK
    W = WIDTH

    @pl.when(pl.program_id(1) == 0)
    def _():
        s_ref[...] = jnp.zeros_like(s_ref)

    cos = cos_ref[...]
    sin = sin_ref[...]
    masks = _head_masks(C, W)
    hi = lax.broadcasted_iota(jnp.int32, (W, W), 0) // HEAD_DIM
    hj = lax.broadcasted_iota(jnp.int32, (W, W), 1) // HEAD_DIM
    first_half = lax.broadcasted_iota(jnp.int32, (C, W), 1) % HEAD_DIM < HEAD_DIM // 2

    def swap_halves(x):
        return jnp.where(first_half, pltpu.roll(x, W - HEAD_DIM // 2, 1),
                         pltpu.roll(x, HEAD_DIM // 2, 1))

    for i in range(p_ref.shape[0]):
        p = p_ref[i]
        q = p[:, 0:W] * cos + swap_halves(p[:, 0:W]) * sin
        k = (p[:, W:2 * W] * cos + swap_halves(p[:, W:2 * W]) * sin) * (HEAD_DIM ** -0.5)
        v = p[:, 2 * W:3 * W]
        gate = p[:, 3 * W:4 * W]
        scores = _dot_nt(q, _stack_heads(k, masks)) * dec_ref[...]
        y = _dot(scores, _stack_heads(v, masks))
        s0 = s_ref[i]
        y = y + _dot(q * xi_ref[...], s0)
        kv = _dot_tn(k * zeta_ref[...], v)
        s_ref[i] = s0 * cd_ref[...] + jnp.where(hi == hj, kv, 0.0)
        ms = _dot(y * y, hsum_ref[...]) * (1.0 / HEAD_DIM)
        o_ref[i] = _silu(gate) * (y * lax.rsqrt(ms + RET_NORM_EPS))


def _retention_tables(T):
    C, H, d = RET_CHUNK, HEADS, HEAD_DIM
    pos = jnp.arange(T, dtype=F32)
    inv_freq = ROPE_BASE ** (-jnp.arange(0, d, 2, dtype=F32) / d)
    ang = pos[:, None] * inv_freq[None, :]
    cos = jnp.cos(ang)
    sin = jnp.sin(ang)
    cos_full = jnp.tile(jnp.concatenate([cos, cos], -1), (1, H))
    sin_full = jnp.tile(jnp.concatenate([-sin, sin], -1), (1, H))
    log_gamma = jnp.log1p(-(2.0 ** (-5.0 - jnp.arange(H, dtype=F32))))
    idx = jnp.arange(C, dtype=F32)
    diff = idx[:, None] - idx[None, :]
    dec = jnp.where(diff >= 0, jnp.exp(log_gamma[:, None, None] * jnp.maximum(diff, 0.0)), 0.0)
    xi = jnp.exp(log_gamma[:, None] * (idx + 1.0))
    zeta = jnp.exp(log_gamma[:, None] * (C - 1.0 - idx))
    cd = jnp.exp(log_gamma * C)
    per_lane = lambda a: jnp.repeat(a.T, d, axis=1)
    return (cos_full, sin_full, dec.transpose(1, 0, 2).reshape(C, H * C), per_lane(xi), per_lane(zeta),
            jnp.repeat(cd, d).reshape(1, H * d))


def _retention(p):
    B, T, cols = p.shape
    C = RET_CHUNK
    W = WIDTH
    cos, sin, dec, xi, zeta, cd = _retention_tables(T)
    head = np.arange(W) // HEAD_DIM
    hsum = jnp.asarray((head[:, None] == head[None, :]).astype(np.float32)).astype(BF16)
    const = lambda b, t: (0, 0)
    G = RET_ROWS if B % RET_ROWS == 0 else 1
    return pl.pallas_call(
        _retention_kernel,
        grid=(B // G, T // C),
        in_specs=[
            pl.BlockSpec((G, C, cols), lambda b, t: (b, t, 0)),
            pl.BlockSpec((C, W), lambda b, t: (t, 0)),
            pl.BlockSpec((C, W), lambda b, t: (t, 0)),
            pl.BlockSpec(dec.shape, const),
            pl.BlockSpec(xi.shape, const),
            pl.BlockSpec(zeta.shape, const),
            pl.BlockSpec(cd.shape, const),
            pl.BlockSpec(hsum.shape, const),
        ],
        out_specs=pl.BlockSpec((G, C, W), lambda b, t: (b, t, 0)),
        out_shape=jax.ShapeDtypeStruct((B, T, W), F32),
        scratch_shapes=[pltpu.VMEM((G, W, W), F32)],
        compiler_params=_cparams(("arbitrary", "arbitrary")),
        name="retention",
    )(p, cos, sin, dec, xi, zeta, cd, hsum)


def _pool_kernel(u_ref, w_ref, scale_ref, o_ref):
    u = u_ref[0]
    T = u.shape[0]
    row = lax.broadcasted_iota(jnp.int32, (T, 1), 0)

    def lag(x, k):
        return jnp.where(row >= k, pltpu.roll(x, k, 0), 0.0)

    s2 = u + lag(u, 1)
    s4 = s2 + lag(s2, 2)
    s8 = s4 + lag(s4, 4)
    s16 = s8 + lag(s8, 8)
    grp = lax.broadcasted_iota(jnp.int32, (1, POOL_WIDTH), 1) // POOL_GROUP_DIM
    s = jnp.where(grp == 0, s2, jnp.where(grp == 1, s4, jnp.where(grp == 2, s8, s16)))
    win = jnp.where(grp == 0, POOL_WINDOWS[0],
                    jnp.where(grp == 1, POOL_WINDOWS[1],
                              jnp.where(grp == 2, POOL_WINDOWS[2], POOL_WINDOWS[3])))
    count = jnp.minimum(row + 1, win).astype(F32)
    pooled = s / count - u
    o_ref[0] = _dot(pooled, w_ref[...]) * scale_ref[...]


def _pool(u, pool_w, pool_scale):
    B, T, Wp = u.shape
    G, d = POOL_GROUPS, POOL_GROUP_DIM
    wbd = jnp.zeros((Wp, Wp), F32)
    for gi in range(G):
        wbd = wbd.at[gi * d:(gi + 1) * d, gi * d:(gi + 1) * d].set(pool_w[gi])
    return pl.pallas_call(
        _pool_kernel,
        grid=(B,),
        in_specs=[
            pl.BlockSpec((1, T, Wp), lambda b: (b, 0, 0)),
            pl.BlockSpec((Wp, Wp), lambda b: (0, 0)),
            pl.BlockSpec((1, Wp), lambda b: (0, 0)),
        ],
        out_specs=pl.BlockSpec((1, T, Wp), lambda b: (b, 0, 0)),
        out_shape=jax.ShapeDtypeStruct((B, T, Wp), F32),
        compiler_params=_cparams(("arbitrary",)),
        name="pool",
    )(u, wbd.astype(BF16), pool_scale.reshape(1, Wp))


def _route(logits_t, bias_col):
    E, tm = logits_t.shape
    per_group = E // N_GROUPS
    neg_inf = -jnp.inf
    scores = _sigmoid(logits_t)
    choice = scores + bias_col
    c3 = choice.reshape(N_GROUPS, per_group, tm)
    sub = lax.broadcasted_iota(jnp.int32, c3.shape, 1)
    m1 = jnp.max(c3, axis=1, keepdims=True)
    first = jnp.min(jnp.where(c3 == m1, sub, per_group), axis=1, keepdims=True)
    m2 = jnp.max(jnp.where(sub == first, neg_inf, c3), axis=1, keepdims=True)
    gs = m1 + m2
    gidx = lax.broadcasted_iota(jnp.int32, gs.shape, 0)
    grank = jnp.zeros(gs.shape, jnp.int32)
    for j in range(N_GROUPS):
        other = gs[j:j + 1]
        ahead = jnp.where(other > gs, 1, jnp.where((other == gs) & (gidx > j), 1, 0))
        grank = grank + ahead
    gmask = jnp.broadcast_to(grank < TOPK_GROUPS, c3.shape)
    masked = jnp.where(gmask, c3, neg_inf).reshape(E, tm)
    eidx = lax.broadcasted_iota(jnp.int32, (E, tm), 0)
    top = jnp.zeros((E, tm), F32)
    for _ in range(TOP_K):
        best = jnp.max(masked, axis=0, keepdims=True)
        first = jnp.min(jnp.where(masked == best, eidx, E), axis=0, keepdims=True)
        hit = eidx == first
        top = jnp.where(hit, scores, top)
        masked = jnp.where(hit, neg_inf, masked)
    return top / jnp.sum(top, axis=0, keepdims=True) * ROUTED_SCALE


def _outproj_kernel(yr_ref, yt_ref, yp_ref, x_ref, w1_ref, w2_ref, w3_ref, g1_ref, ng_ref,
                    sc_ref, sh_ref, wr_ref, rb_ref, x1_ref, h_ref, wc_ref):
    mixed = (_dot(yr_ref[...], w1_ref[...]) + _dot(yt_ref[...], w2_ref[...])
             + _dot(yp_ref[...], w3_ref[...]))
    x1 = x_ref[...] + g1_ref[0] * mixed
    x1_ref[...] = x1
    h = _modulated_norm(x1, ng_ref[...], sc_ref[0], sh_ref[0])
    h_ref[...] = h.astype(BF16)
    logits_t = lax.dot_general(wr_ref[...], h, (((1,), (1,)), ((), ())),
                               preferred_element_type=F32, precision=HIGHEST)
    wc_t = _route(logits_t, rb_ref[...])
    pad = jnp.zeros((LANES - N_EXPERTS, wc_t.shape[1]), F32)
    wc_ref[...] = jnp.concatenate([wc_t, pad], axis=0).T


def _outproj(yr, yt, yp, x2, w_out, gate1, ng, scale2, shift2, w_router, router_bias, T):
    N, D = x2.shape
    B = N // T
    tm = min(1024, T)
    per_b = T // tm
    W = WIDTH
    w1 = w_out[:W].astype(BF16)
    w2 = w_out[W:2 * W].astype(BF16)
    w3 = w_out[2 * W:].astype(BF16)
    row = lambda i: (i, 0)
    const = lambda i: (0, 0)
    bvec = lambda i: (i // per_b, 0, 0)
    return pl.pallas_call(
        _outproj_kernel,
        grid=(N // tm,),
        in_specs=[
            pl.BlockSpec((tm, W), row),
            pl.BlockSpec((tm, W), row),
            pl.BlockSpec((tm, POOL_WIDTH), row),
            pl.BlockSpec((tm, D), row),
            pl.BlockSpec(w1.shape, const),
            pl.BlockSpec(w2.shape, const),
            pl.BlockSpec(w3.shape, const),
            pl.BlockSpec((1, 1, D), bvec),
            pl.BlockSpec((1, D), const),
            pl.BlockSpec((1, 1, D), bvec),
            pl.BlockSpec((1, 1, D), bvec),
            pl.BlockSpec((N_EXPERTS, D), const),
            pl.BlockSpec((N_EXPERTS, 1), const),
        ],
        out_specs=[
            pl.BlockSpec((tm, D), row),
            pl.BlockSpec((tm, D), row),
            pl.BlockSpec((tm, LANES), row),
        ],
        out_shape=[
            jax.ShapeDtypeStruct((N, D), F32),
            jax.ShapeDtypeStruct((N, D), BF16),
            jax.ShapeDtypeStruct((N, LANES), F32),
        ],
        compiler_params=_cparams(("arbitrary",)),
        name="outproj_router",
    )(yr, yt, yp, x2, w1, w2, w3, gate1.reshape(B, 1, D), ng.reshape(1, D),
      scale2.reshape(B, 1, D), shift2.reshape(B, 1, D), w_router.T,
      router_bias.reshape(N_EXPERTS, 1))


def _slot_positions(wc):
    tm = wc.shape[0]
    sel = jnp.where(wc > 0.0, 1.0, 0.0).astype(BF16)
    ti = lax.broadcasted_iota(jnp.int32, (tm, tm), 0)
    si = lax.broadcasted_iota(jnp.int32, (tm, tm), 1)
    earlier = jnp.where(ti > si, 1.0, 0.0).astype(BF16)
    return sel, jnp.dot(earlier, sel, preferred_element_type=F32)


def _slot_rows(code_t, value_t, tm):
    slot = lax.broadcasted_iota(jnp.int32, (MOE_CAP, tm), 0).astype(F32)
    rows = []
    for e in range(N_EXPERTS):
        value = 1.0 if value_t is None else value_t[e:e + 1, :]
        rows.append(jnp.where(code_t[e:e + 1, :] == slot, value, 0.0).astype(BF16))
    return jnp.concatenate(rows, axis=0)


def _dispatch_kernel(h_ref, wc_ref, xs_ref, over_ref, cnt_ref):
    nb, ts, D = h_ref.shape
    tm = nb * ts
    wc = wc_ref[...].reshape(tm, LANES)
    sel, pos = _slot_positions(wc)
    over_ref[...] = jnp.where(pos >= MOE_CAP, wc, 0.0).reshape(nb, ts, LANES)
    cnt_ref[...] = jnp.sum(sel.astype(F32), axis=0, keepdims=True)
    code = jnp.where(wc > 0.0, pos, -1.0)
    onehot = _slot_rows(code.T, None, tm)
    xs_ref[...] = jnp.dot(onehot, h_ref[...].reshape(tm, D),
                          preferred_element_type=F32).astype(BF16)


def _dispatch(h, wc, B, T):
    N, D = h.shape
    ts = MOE_TILE_SEQ
    nt = T // ts
    L = N_EXPERTS * MOE_CAP
    tile = lambda i: (0, i, 0, 0)
    xs, over, cnt = pl.pallas_call(
        _dispatch_kernel,
        grid=(nt,),
        in_specs=[
            pl.BlockSpec((B, None, ts, D), tile),
            pl.BlockSpec((B, None, ts, LANES), tile),
        ],
        out_specs=[
            pl.BlockSpec((None, L, D), lambda i: (i, 0, 0)),
            pl.BlockSpec((B, None, ts, LANES), tile),
            pl.BlockSpec((None, 1, LANES), lambda i: (i, 0, 0)),
        ],
        out_shape=[
            jax.ShapeDtypeStruct((nt, L, D), BF16),
            jax.ShapeDtypeStruct((B, nt, ts, LANES), F32),
            jax.ShapeDtypeStruct((nt, 1, LANES), F32),
        ],
        compiler_params=_cparams(("arbitrary",)),
        name="moe_dispatch",
    )(h.reshape(B, nt, ts, D), wc.reshape(B, nt, ts, LANES))
    return xs, over.reshape(N, LANES), cnt.reshape(nt, LANES)[:, :N_EXPERTS]


def _slot_pieces(cap):
    bounds = [0] + [n for n in MOE_USED_STEPS if n < cap] + [cap]
    return list(zip(bounds[:-1], bounds[1:]))


def _experts_kernel(used_ref, *refs):
    pieces = _slot_pieces(refs[-4].shape[1])
    x_refs = refs[:len(pieces)]
    wg_ref, wu_ref, wd_ref, o_ref, wg_s, wu_s, wd_s = refs[len(pieces):]

    @pl.when(pl.program_id(1) == 0)
    def _():
        wg_s[...] = wg_ref[0, 0].astype(BF16)
        wu_s[...] = wu_ref[0, 0].astype(BF16)
        wd_s[...] = wd_ref[0, 0].astype(BF16)

    tg, cap, D = o_ref.shape
    used = used_ref[pl.program_id(0), pl.program_id(1)]

    def run(k):
        x = jnp.concatenate([x_refs[j][...].reshape(tg * (hi - lo), D)
                             for j, (lo, hi) in enumerate(pieces[:k + 1])], axis=0)
        hid = _silu(jnp.dot(x, wg_s[...], preferred_element_type=F32)) * jnp.dot(
            x, wu_s[...], preferred_element_type=F32)
        y = _dot(hid, wd_s[...]).astype(BF16)
        for lo, hi in pieces[:k + 1]:
            o_ref[:, lo:hi, :] = y[tg * lo:tg * hi].reshape(tg, hi - lo, D)
        if pieces[k][1] < cap:
            o_ref[:, pieces[k][1]:cap, :] = jnp.zeros((tg, cap - pieces[k][1], D), BF16)

    for k, (lo, hi) in enumerate(pieces):
        pl.when((used > lo) & (used <= hi) if hi < cap else used > lo)(functools.partial(run, k))

    @pl.when(used == 0)
    def _():
        o_ref[...] = jnp.zeros_like(o_ref)


def _experts(xs, cnt, wg, wu, wd, layer):
    nt, L, D = xs.shape
    _, E, _, Hd = wg.shape
    cap = L // E
    tg = math.gcd(nt, 16)
    used = jnp.minimum(jnp.max(cnt.reshape(nt // tg, tg, E), axis=1), cap).T.astype(jnp.int32)
    slots = pl.BlockSpec((tg, None, cap, D), lambda e, g, u: (g, e, 0, 0))

    def piece_spec(lo, hi):
        assert lo % (hi - lo) == 0

        def index(e, g, u):
            live = u[e, g] > lo
            return (jnp.where(live, g, 0), jnp.where(live, e, 0), lo // (hi - lo), 0)

        return pl.BlockSpec((tg, None, hi - lo, D), index)

    pieces = _slot_pieces(cap)
    xs4 = xs.reshape(nt, E, cap, D)
    grid_spec = pltpu.PrefetchScalarGridSpec(
        num_scalar_prefetch=1,
        grid=(E, nt // tg),
        in_specs=[piece_spec(lo, hi) for lo, hi in pieces] + [
            pl.BlockSpec((1, 1, D, Hd), lambda e, g, u: (layer, e, 0, 0)),
            pl.BlockSpec((1, 1, D, Hd), lambda e, g, u: (layer, e, 0, 0)),
            pl.BlockSpec((1, 1, Hd, D), lambda e, g, u: (layer, e, 0, 0)),
        ],
        out_specs=slots,
        scratch_shapes=[pltpu.VMEM((D, Hd), BF16), pltpu.VMEM((D, Hd), BF16),
                        pltpu.VMEM((Hd, D), BF16)],
    )
    return pl.pallas_call(
        _experts_kernel,
        grid_spec=grid_spec,
        out_shape=jax.ShapeDtypeStruct((nt, E, cap, D), BF16),
        compiler_params=_cparams(("arbitrary", "arbitrary")),
        name="moe_experts",
    )(used, *([xs4] * len(pieces)), wg, wu, wd).reshape(nt, L, D)


def _combine_kernel(y_ref, wc_ref, h_ref, x1_ref, g2_ref, sg_ref, su_ref, sd_ref, fg_ref,
                    *rest, final_norm):
    ex_ref = rest[0] if len(rest) == 2 else None
    o_ref = rest[-1]
    nb, ts, D = h_ref.shape
    tm = nb * ts
    wc = wc_ref[...].reshape(tm, LANES)
    _, pos = _slot_positions(wc)
    code = jnp.where(wc > 0.0, pos, -1.0)
    weighted = _slot_rows(code.T, wc.T, tm)
    routed = _dot_tn(weighted, y_ref[...])
    h = h_ref[...].reshape(tm, D)
    hid = _silu(jnp.dot(h, sg_ref[...], preferred_element_type=F32)) * jnp.dot(
        h, su_ref[...], preferred_element_type=F32)
    y = routed + _dot(hid, sd_ref[...])
    if ex_ref is not None:
        y = y + ex_ref[...].reshape(tm, D)
    gate = jnp.broadcast_to(g2_ref[...], (nb, ts, D)).reshape(tm, D)
    xo = x1_ref[...].reshape(tm, D) + gate * y
    if final_norm:
        ms = jnp.mean(xo * xo, axis=-1, keepdims=True)
        xo = xo * lax.rsqrt(ms + NORM_EPS) * fg_ref[...]
    o_ref[...] = xo.reshape(nb, ts, D)


def _combine(ys, wc, h, x1, extra, gate2, sg, su, sd, final_g, B, T, final_norm):
    N, D = x1.shape
    ts = MOE_TILE_SEQ
    nt = T // ts
    L = N_EXPERTS * MOE_CAP
    tile = lambda i: (0, i, 0, 0)
    const = lambda i: (0, 0)
    tok = lambda a: a.reshape(B, nt, ts, a.shape[-1])
    extras = [] if extra is None else [tok(extra)]
    out = pl.pallas_call(
        functools.partial(_combine_kernel, final_norm=final_norm),
        grid=(nt,),
        in_specs=[
            pl.BlockSpec((None, L, D), lambda i: (i, 0, 0)),
            pl.BlockSpec((B, None, ts, LANES), tile),
            pl.BlockSpec((B, None, ts, D), tile),
            pl.BlockSpec((B, None, ts, D), tile),
            pl.BlockSpec((B, 1, D), lambda i: (0, 0, 0)),
            pl.BlockSpec(sg.shape, const),
            pl.BlockSpec(su.shape, const),
            pl.BlockSpec(sd.shape, const),
            pl.BlockSpec((1, D), const),
        ] + [pl.BlockSpec((B, None, ts, D), tile) for _ in extras],
        out_specs=pl.BlockSpec((B, None, ts, D), tile),
        out_shape=jax.ShapeDtypeStruct((B, nt, ts, D), F32),
        compiler_params=_cparams(("arbitrary",)),
        name="moe_combine",
    )(ys, tok(wc), tok(h), tok(x1), gate2.reshape(B, 1, D), sg, su, sd,
      final_g.reshape(1, D), *extras)
    return out.reshape(N, D)


def _overflow_kernel(h_ref, wc_ref, wg_ref, wu_ref, wd_ref, o_ref):
    e = pl.program_id(1)

    @pl.when(e == 0)
    def _():
        o_ref[...] = jnp.zeros_like(o_ref)

    h = h_ref[...]
    lane = lax.broadcasted_iota(jnp.int32, wc_ref.shape, 1)
    w = jnp.sum(jnp.where(lane == e, wc_ref[...], 0.0), axis=1, keepdims=True)
    hid = _silu(jnp.dot(h, wg_ref[0], preferred_element_type=F32)) * jnp.dot(
        h, wu_ref[0], preferred_element_type=F32)
    o_ref[...] += _dot(hid * w, wd_ref[0])


def _overflow(h, wc_over, wg, wu, wd):
    N, D = h.shape
    tm = math.gcd(N, 1024)
    E, _, Hd = wg.shape
    row = lambda i, e: (i, 0)
    return pl.pallas_call(
        _overflow_kernel,
        grid=(N // tm, E),
        in_specs=[
            pl.BlockSpec((tm, D), row),
            pl.BlockSpec((tm, LANES), row),
            pl.BlockSpec((1, D, Hd), lambda i, e: (e, 0, 0)),
            pl.BlockSpec((1, D, Hd), lambda i, e: (e, 0, 0)),
            pl.BlockSpec((1, Hd, D), lambda i, e: (e, 0, 0)),
        ],
        out_specs=pl.BlockSpec((tm, D), row),
        out_shape=jax.ShapeDtypeStruct((N, D), F32),
        compiler_params=_cparams(("arbitrary", "arbitrary")),
        name="moe_overflow",
    )(h, wc_over, wg, wu, wd)


def _moe(h, wc, wg, wu, wd, layer, sg, su, sd, x1, gate2, final_g, B, T, final_norm):
    xs, wc_over, cnt = _dispatch(h, wc, B, T)
    ys = _experts(xs, cnt, wg, wu, wd, layer)
    def finish(extra):
        return _combine(ys, wc, h, x1, extra, gate2, sg, su, sd, final_g, B, T, final_norm)

    def with_overflow():
        return finish(_overflow(h, wc_over, wg[layer].astype(BF16), wu[layer].astype(BF16),
                                wd[layer].astype(BF16)))

    return lax.cond(jnp.any(wc_over != 0.0), with_overflow, lambda: finish(None))


def _split_w_in(w_in):
    w_rwkv = w_in[:, :RWKV_COLS]
    w_ret = w_in[:, RWKV_COLS:RWKV_COLS + RET_COLS]
    w_pool = w_in[:, RWKV_COLS + RET_COLS:]
    return w_rwkv.astype(BF16), w_ret.astype(BF16), w_pool.astype(BF16)


def kernel(x, c, norm1_g, norm2_g, w_ada, b_ada, w_in, w_out, rwkv_mu, rwkv_w_up, rwkv_w0,
           rwkv_a_up, rwkv_a0, rwkv_g_up, rwkv_k_k, rwkv_k_a, rwkv_r_k, rwkv_lnx_w, rwkv_lnx_b,
           pool_w, pool_scale, w_router, router_bias, we_gate, we_up, we_down,
           ws_gate, ws_up, ws_down, final_g):
    B, T, D = x.shape
    L = w_in.shape[0]
    N = B * T
    mod = _adaln(c, w_ada, b_ada)
    x2 = x.reshape(N, D)
    for l in range(L):
        shift1, scale1, gate1, shift2, scale2, gate2 = jnp.split(mod[l], 6, axis=-1)
        w1, w2, w3 = _split_w_in(w_in[l])
        p_rwkv, p_ret, p_pool = _inproj(x2, norm1_g[l], scale1, shift1, w1, w2, w3, T)
        y_rwkv = _rwkv(p_rwkv.reshape(B, T, -1), rwkv_mu[l], rwkv_w_up[l], rwkv_w0[l],
                       rwkv_a_up[l], rwkv_a0[l], rwkv_g_up[l], rwkv_k_k[l], rwkv_k_a[l],
                       rwkv_r_k[l], rwkv_lnx_w[l], rwkv_lnx_b[l])
        y_ret = _retention(p_ret.reshape(B, T, -1))
        y_pool = _pool(p_pool.reshape(B, T, -1), pool_w[l], pool_scale[l])
        x1, h, wc = _outproj(y_rwkv.reshape(N, -1), y_ret.reshape(N, -1), y_pool.reshape(N, -1),
                             x2, w_out[l], gate1, norm2_g[l], scale2, shift2,
                             w_router[l], router_bias[l], T)
        x2 = _moe(h, wc, we_gate, we_up, we_down, l,
                  ws_gate[l].astype(BF16), ws_up[l].astype(BF16),
                  ws_down[l].astype(BF16), x1, gate2, final_g, B, T,
                  final_norm=(l == L - 1))
    return x2.reshape(B, T, D)
```

```python
import functools
import math

import numpy as np
import jax
import jax.numpy as jnp
from jax import lax
from jax.experimental import pallas as pl
from jax.experimental.pallas import tpu as pltpu

F32 = jnp.float32
BF16 = jnp.bfloat16
HIGHEST = lax.Precision.HIGHEST

D_MODEL = 1024
HEADS = 6
HEAD_DIM = 64
WIDTH = HEADS * HEAD_DIM
W_LORA = 64
A_LORA = 64
G_LORA = 128
RWKV_COLS = 3 * WIDTH + W_LORA + A_LORA + G_LORA
RET_COLS = 4 * WIDTH
RET_CHUNK = 128
RWKV_CHUNK = 128
RWKV_SUB = 16
RET_ROWS = 4
RWKV_ROWS = 4
POOL_GROUPS = 4
POOL_GROUP_DIM = 64
POOL_WIDTH = POOL_GROUPS * POOL_GROUP_DIM
POOL_WINDOWS = (2, 4, 8, 16)
N_EXPERTS = 64
TOP_K = 8
N_GROUPS = 8
TOPK_GROUPS = 4
EXPERT_HIDDEN = 256
ROUTED_SCALE = 2.5
NORM_EPS = 1e-6
RWKV_LNX_EPS = 64e-5
RET_NORM_EPS = 1e-6
ROPE_BASE = 10000.0
LANES = 128
MOE_TILE_SEQ = 16
MOE_CAP = 64
MOE_USED_STEPS = (48, 64)
VMEM_LIMIT = 48 * 1024 * 1024


def _cparams(sem):
    return pltpu.CompilerParams(dimension_semantics=sem, vmem_limit_bytes=VMEM_LIMIT)


def _dot(a, b):
    return jnp.dot(a.astype(BF16), b.astype(BF16), preferred_element_type=F32)


def _dot_nt(a, b):
    return lax.dot_general(a.astype(BF16), b.astype(BF16), (((1,), (1,)), ((), ())),
                           preferred_element_type=F32)


def _dot_tn(a, b):
    return lax.dot_general(a.astype(BF16), b.astype(BF16), (((0,), (0,)), ((), ())),
                           preferred_element_type=F32)


def _dot_f32(a, b):
    return jnp.dot(a, b, preferred_element_type=F32, precision=HIGHEST)


def _sigmoid(x):
    return 1.0 / (1.0 + jnp.exp(-x))


def _silu(x):
    return x * _sigmoid(x)


def _head_masks(rows, width):
    lane = lax.broadcasted_iota(jnp.int32, (rows, width), 1)
    return [lane // HEAD_DIM == h for h in range(width // HEAD_DIM)]


def _stack_heads(x, masks):
    return jnp.concatenate([jnp.where(m, x, 0.0) for m in masks], axis=0).astype(BF16)


def _select_heads(stacked, masks, c):
    out = stacked[0:c]
    for h in range(1, len(masks)):
        out = jnp.where(masks[h], stacked[h * c:(h + 1) * c], out)
    return out


def _adaln_kernel(c_ref, w_ref, b_ref, o_ref):
    o_ref[0] = _dot_f32(_silu(c_ref[...]), w_ref[0]) + b_ref[0]


def _adaln(c, w_ada, b_ada):
    L, D, M = w_ada.shape
    B = c.shape[0]
    tn = 1536
    return pl.pallas_call(
        _adaln_kernel,
        grid=(L, M // tn),
        in_specs=[
            pl.BlockSpec((B, D), lambda l, j: (0, 0)),
            pl.BlockSpec((1, D, tn), lambda l, j: (l, 0, j)),
            pl.BlockSpec((1, 1, tn), lambda l, j: (l, 0, j)),
        ],
        out_specs=pl.BlockSpec((1, B, tn), lambda l, j: (l, 0, j)),
        out_shape=jax.ShapeDtypeStruct((L, B, M), F32),
        compiler_params=_cparams(("arbitrary", "arbitrary")),
        name="adaln",
    )(c, w_ada, b_ada.reshape(L, 1, M))


def _modulated_norm(x, g, scale, shift):
    ms = jnp.mean(x * x, axis=-1, keepdims=True)
    return x * lax.rsqrt(ms + NORM_EPS) * g * (1.0 + scale) + shift


def _inproj_kernel(x_ref, g_ref, sc_ref, sh_ref, w1_ref, w2_ref, w3_ref, o1_ref, o2_ref, o3_ref):
    h = _modulated_norm(x_ref[...], g_ref[...], sc_ref[0], sh_ref[0]).astype(BF16)
    o1_ref[...] = jnp.dot(h, w1_ref[...], preferred_element_type=F32)
    o2_ref[...] = jnp.dot(h, w2_ref[...], preferred_element_type=F32)
    o3_ref[...] = jnp.dot(h, w3_ref[...], preferred_element_type=F32)


def _inproj(x2, g, scale, shift, w1, w2, w3, T):
    N, D = x2.shape
    B = N // T
    tm = min(512, T)
    per_b = T // tm
    row = lambda i: (i, 0)
    const = lambda i: (0, 0)
    bvec = lambda i: (i // per_b, 0, 0)
    return pl.pallas_call(
        _inproj_kernel,
        grid=(N // tm,),
        in_specs=[
            pl.BlockSpec((tm, D), row),
            pl.BlockSpec((1, D), const),
            pl.BlockSpec((1, 1, D), bvec),
            pl.BlockSpec((1, 1, D), bvec),
            pl.BlockSpec(w1.shape, const),
            pl.BlockSpec(w2.shape, const),
            pl.BlockSpec(w3.shape, const),
        ],
        out_specs=[
            pl.BlockSpec((tm, w1.shape[1]), row),
            pl.BlockSpec((tm, w2.shape[1]), row),
            pl.BlockSpec((tm, w3.shape[1]), row),
        ],
        out_shape=[
            jax.ShapeDtypeStruct((N, w1.shape[1]), F32),
            jax.ShapeDtypeStruct((N, w2.shape[1]), F32),
            jax.ShapeDtypeStruct((N, w3.shape[1]), F32),
        ],
        compiler_params=_cparams(("arbitrary",)),
        name="inproj",
    )(x2, g.reshape(1, D), scale.reshape(B, 1, D), shift.reshape(B, 1, D), w1, w2, w3)


def _unit_lower_inverse(a3):
    H, C, _ = a3.shape
    ri = lax.broadcasted_iota(jnp.int32, (H, C, C), 1)
    ci = lax.broadcasted_iota(jnp.int32, (H, C, C), 2)
    eye = (ri == ci).astype(F32)
    same = (ri // RWKV_SUB) == (ci // RWKV_SUB)
    dm = jnp.where(same, a3, 0.0)
    off = jnp.where(same, 0.0, a3)

    def bmm(x, y):
        return jnp.einsum('hij,hjk->hik', x.astype(BF16), y.astype(BF16),
                          preferred_element_type=F32)

    d2 = bmm(dm, dm)
    d4 = bmm(d2, d2)
    d8 = bmm(d4, d4)
    x = eye - dm
    x = x + bmm(x, d2)
    x = x + bmm(x, d4)
    x = x + bmm(x, d8)
    n = bmm(x, off)
    y = eye - n
    power = bmm(n, n)
    order = 2
    while order < C // RWKV_SUB:
        y = y + bmm(y, power)
        order *= 2
        if order < C // RWKV_SUB:
            power = bmm(power, power)
    return bmm(y, x)


def _rwkv_prepare(p, carry, prm):
    (mu, wup, w0, aup, a0, gup, k_k, k_a, r_k, lnw, lnb, hsum) = prm
    C = RWKV_CHUNK
    W = WIDTH
    H = HEADS
    row = lax.broadcasted_iota(jnp.int32, (C, 1), 0)
    prev = jnp.where(row == 0, carry, pltpu.roll(p, 1, 0))
    xs = p + (prev - p) * mu

    r = xs[:, 0:W]
    k = xs[:, W:2 * W]
    v = xs[:, 2 * W:3 * W]
    xwa = xs[:, 3 * W:3 * W + W_LORA + A_LORA]
    xg = xs[:, 3 * W + W_LORA + A_LORA:]

    z = w0 + _dot(jnp.tanh(xwa), wup)
    log_w = -math.exp(-0.5) * _sigmoid(z)
    a = _sigmoid(a0 + _dot(xwa, aup))
    g = _dot(_sigmoid(xg), gup)
    kk = k * k_k
    kk = kk / jnp.maximum(jnp.sqrt(_dot(kk * kk, hsum)), 1e-12)
    k = k * (1.0 + (a - 1.0) * k_a)

    ti = lax.broadcasted_iota(jnp.int32, (C, C), 0)
    si = lax.broadcasted_iota(jnp.int32, (C, C), 1)
    tri = jnp.where(ti >= si, 1.0, 0.0).astype(BF16)
    log_w_hi = log_w.astype(BF16)
    cum = (jnp.dot(tri, log_w_hi, preferred_element_type=F32)
           + _dot(tri, log_w - log_w_hi.astype(F32)))
    g_in = jnp.exp(cum)
    g_inv = jnp.exp(-cum)
    g_end = g_in[C - 1:C, :]
    kt = kk * jnp.exp(cum - log_w)
    bt = kk * a * g_inv
    kq = k * g_inv
    rt = r * g_in

    masks = _head_masks(C, W)
    t3 = lax.broadcasted_iota(jnp.int32, (H, C, C), 1)
    s3 = lax.broadcasted_iota(jnp.int32, (H, C, C), 2)
    a_ab = jnp.where(t3 > s3, _dot_nt(_stack_heads(kt, masks), bt).reshape(H, C, C), 0.0)
    bt_rows = _stack_heads(bt, masks)
    kq_rows = _stack_heads(kq, masks)
    t1 = lax.broadcasted_iota(jnp.int32, (C, H * C), 0)
    s1 = lax.broadcasted_iota(jnp.int32, (C, H * C), 1) % C
    a_ak = jnp.where(t1 > s1, _dot_nt(kt, kq_rows), 0.0)
    t2 = lax.broadcasted_iota(jnp.int32, (C, 2 * H * C), 0)
    s2 = lax.broadcasted_iota(jnp.int32, (C, 2 * H * C), 1) % C
    p_r = jnp.where(t2 >= s2, _dot_nt(rt, jnp.concatenate([bt_rows, kq_rows], axis=0)), 0.0)
    return a_ab, (a_ak, p_r, kt, rt, bt, kq, g_end, r, k, v, g)


def _rwkv_finish(vals, t_inv, s0, prm):
    (a_ak, p_r, kt, rt, bt, kq, g_end, r, k, v, g) = vals
    (mu, wup, w0, aup, a0, gup, k_k, k_a, r_k, lnw, lnb, hsum) = prm
    C = RWKV_CHUNK
    W = WIDTH
    H = HEADS
    masks = _head_masks(C, W)
    ks = _dot_nt(kt, s0)
    rs = _dot_nt(rt, s0)
    v_rows = _stack_heads(v, masks)
    av = _dot(a_ak, v_rows)
    u = _select_heads(_dot(t_inv.reshape(H * C, C), -(ks + av)), masks, C)
    y = rs + _dot(p_r, jnp.concatenate([_stack_heads(u, masks), v_rows], axis=0))
    upd = _dot_tn(jnp.concatenate([u, v], axis=0),
                  jnp.concatenate([bt * g_end, kq * g_end], axis=0))
    hi = lax.broadcasted_iota(jnp.int32, (W, W), 0) // HEAD_DIM
    hj = lax.broadcasted_iota(jnp.int32, (W, W), 1) // HEAD_DIM
    s_new = s0 * g_end + jnp.where(hi == hj, upd, 0.0)

    inv_d = 1.0 / HEAD_DIM
    mean = _dot(y, hsum) * inv_d
    yc = y - mean
    var = _dot(yc * yc, hsum) * inv_d
    yn = yc * lax.rsqrt(var + RWKV_LNX_EPS) * lnw + lnb
    bonus = _dot(r * k * r_k, hsum) * v
    return (yn + bonus) * g, s_new


def _rwkv_kernel(p_ref, mu_ref, wup_ref, w0_ref, aup_ref, a0_ref, gup_ref, kk_ref, ka_ref,
                 rk_ref, lnw_ref, lnb_ref, hsum_ref, o_ref, carry_ref, s_ref):
    @pl.when(pl.program_id(1) == 0)
    def _():
        carry_ref[...] = jnp.zeros_like(carry_ref)
        s_ref[...] = jnp.zeros_like(s_ref)

    prm = tuple(ref[...] for ref in (mu_ref, wup_ref, w0_ref, aup_ref, a0_ref, gup_ref, kk_ref,
                                     ka_ref, rk_ref, lnw_ref, lnb_ref, hsum_ref))
    G = p_ref.shape[0]
    H = HEADS
    a_abs, vals = [], []
    for i in range(G):
        p = p_ref[i]
        a_ab, val = _rwkv_prepare(p, carry_ref[i], prm)
        carry_ref[i] = p[RWKV_CHUNK - 1:RWKV_CHUNK, :]
        a_abs.append(a_ab)
        vals.append(val)
    t_inv = _unit_lower_inverse(jnp.concatenate(a_abs, axis=0))
    for i in range(G):
        out, s_new = _rwkv_finish(vals[i], t_inv[i * H:(i + 1) * H], s_ref[i], prm)
        s_ref[i] = s_new
        o_ref[i] = out


def _rwkv(p, mu, wup, w0, aup, a0, gup, k_k, k_a, r_k, lnx_w, lnx_b):
    B, T, _ = p.shape
    C = RWKV_CHUNK
    W = WIDTH
    G = RWKV_ROWS if B % RWKV_ROWS == 0 else 1
    lora_in = W_LORA + A_LORA
    wup_pad = jnp.zeros((lora_in, W), F32).at[:W_LORA].set(wup).astype(BF16)
    aup_pad = jnp.zeros((lora_in, W), F32).at[W_LORA:].set(aup).astype(BF16)
    head = np.arange(W) // HEAD_DIM
    hsum = jnp.asarray((head[:, None] == head[None, :]).astype(np.float32)).astype(BF16)
    vec = lambda a: a.reshape(1, -1)
    const = lambda b, t: (0, 0)
    params = [vec(mu), wup_pad, vec(w0), aup_pad, vec(a0), gup.astype(BF16), vec(k_k), vec(k_a),
              vec(r_k), vec(lnx_w), vec(lnx_b), hsum]
    return pl.pallas_call(
        _rwkv_kernel,
        grid=(B // G, T // C),
        in_specs=[pl.BlockSpec((G, C, RWKV_COLS), lambda b, t: (b, t, 0))]
        + [pl.BlockSpec(a.shape, const) for a in params],
        out_specs=pl.BlockSpec((G, C, W), lambda b, t: (b, t, 0)),
        out_shape=jax.ShapeDtypeStruct((B, T, W), F32),
        scratch_shapes=[pltpu.VMEM((G, 1, RWKV_COLS), F32), pltpu.VMEM((G, W, W), F32)],
        compiler_params=_cparams(("arbitrary", "arbitrary")),
        name="rwkv7",
    )(p, *params)


def _retention_kernel(p_ref, cos_ref, sin_ref, dec_ref, xi_ref, zeta_ref, cd_ref, hsum_ref,
                      o_ref, s_ref):
    C = RET_CHUNK
    W = WIDTH

    @pl.when(pl.program_id(1) == 0)
    def _():
        s_ref[...] = jnp.zeros_like(s_ref)

    cos = cos_ref[...]
    sin = sin_ref[...]
    masks = _head_masks(C, W)
    hi = lax.broadcasted_iota(jnp.int32, (W, W), 0) // HEAD_DIM
    hj = lax.broadcasted_iota(jnp.int32, (W, W), 1) // HEAD_DIM
    first_half = lax.broadcasted_iota(jnp.int32, (C, W), 1) % HEAD_DIM < HEAD_DIM // 2

    def swap_halves(x):
        return jnp.where(first_half, pltpu.roll(x, W - HEAD_DIM // 2, 1),
                         pltpu.roll(x, HEAD_DIM // 2, 1))

    for i in range(p_ref.shape[0]):
        p = p_ref[i]
        q = p[:, 0:W] * cos + swap_halves(p[:, 0:W]) * sin
        k = (p[:, W:2 * W] * cos + swap_halves(p[:, W:2 * W]) * sin) * (HEAD_DIM ** -0.5)
        v = p[:, 2 * W:3 * W]
        gate = p[:, 3 * W:4 * W]
        scores = _dot_nt(q, _stack_heads(k, masks)) * dec_ref[...]
        y = _dot(scores, _stack_heads(v, masks))
        s0 = s_ref[i]
        y = y + _dot(q * xi_ref[...], s0)
        kv = _dot_tn(k * zeta_ref[...], v)
        s_ref[i] = s0 * cd_ref[...] + jnp.where(hi == hj, kv, 0.0)
        ms = _dot(y * y, hsum_ref[...]) * (1.0 / HEAD_DIM)
        o_ref[i] = _silu(gate) * (y * lax.rsqrt(ms + RET_NORM_EPS))


def _retention_tables(T):
    C, H, d = RET_CHUNK, HEADS, HEAD_DIM
    pos = jnp.arange(T, dtype=F32)
    inv_freq = ROPE_BASE ** (-jnp.arange(0, d, 2, dtype=F32) / d)
    ang = pos[:, None] * inv_freq[None, :]
    cos = jnp.cos(ang)
    sin = jnp.sin(ang)
    cos_full = jnp.tile(jnp.concatenate([cos, cos], -1), (1, H))
    sin_full = jnp.tile(jnp.concatenate([-sin, sin], -1), (1, H))
    log_gamma = jnp.log1p(-(2.0 ** (-5.0 - jnp.arange(H, dtype=F32))))
    idx = jnp.arange(C, dtype=F32)
    diff = idx[:, None] - idx[None, :]
    dec = jnp.where(diff >= 0, jnp.exp(log_gamma[:, None, None] * jnp.maximum(diff, 0.0)), 0.0)
    xi = jnp.exp(log_gamma[:, None] * (idx + 1.0))
    zeta = jnp.exp(log_gamma[:, None] * (C - 1.0 - idx))
    cd = jnp.exp(log_gamma * C)
    per_lane = lambda a: jnp.repeat(a.T, d, axis=1)
    return (cos_full, sin_full, dec.transpose(1, 0, 2).reshape(C, H * C), per_lane(xi), per_lane(zeta),
            jnp.repeat(cd, d).reshape(1, H * d))


def _retention(p):
    B, T, cols = p.shape
    C = RET_CHUNK
    W = WIDTH
    cos, sin, dec, xi, zeta, cd = _retention_tables(T)
    head = np.arange(W) // HEAD_DIM
    hsum = jnp.asarray((head[:, None] == head[None, :]).astype(np.float32)).astype(BF16)
    const = lambda b, t: (0, 0)
    G = RET_ROWS if B % RET_ROWS == 0 else 1
    return pl.pallas_call(
        _retention_kernel,
        grid=(B // G, T // C),
        in_specs=[
            pl.BlockSpec((G, C, cols), lambda b, t: (b, t, 0)),
            pl.BlockSpec((C, W), lambda b, t: (t, 0)),
            pl.BlockSpec((C, W), lambda b, t: (t, 0)),
            pl.BlockSpec(dec.shape, const),
            pl.BlockSpec(xi.shape, const),
            pl.BlockSpec(zeta.shape, const),
            pl.BlockSpec(cd.shape, const),
            pl.BlockSpec(hsum.shape, const),
        ],
        out_specs=pl.BlockSpec((G, C, W), lambda b, t: (b, t, 0)),
        out_shape=jax.ShapeDtypeStruct((B, T, W), F32),
        scratch_shapes=[pltpu.VMEM((G, W, W), F32)],
        compiler_params=_cparams(("arbitrary", "arbitrary")),
        name="retention",
    )(p, cos, sin, dec, xi, zeta, cd, hsum)


def _pool_kernel(u_ref, w_ref, scale_ref, o_ref):
    u = u_ref[0]
    T = u.shape[0]
    row = lax.broadcasted_iota(jnp.int32, (T, 1), 0)

    def lag(x, k):
        return jnp.where(row >= k, pltpu.roll(x, k, 0), 0.0)

    s2 = u + lag(u, 1)
    s4 = s2 + lag(s2, 2)
    s8 = s4 + lag(s4, 4)
    s16 = s8 + lag(s8, 8)
    grp = lax.broadcasted_iota(jnp.int32, (1, POOL_WIDTH), 1) // POOL_GROUP_DIM
    s = jnp.where(grp == 0, s2, jnp.where(grp == 1, s4, jnp.where(grp == 2, s8, s16)))
    win = jnp.where(grp == 0, POOL_WINDOWS[0],
                    jnp.where(grp == 1, POOL_WINDOWS[1],
                              jnp.where(grp == 2, POOL_WINDOWS[2], POOL_WINDOWS[3])))
    count = jnp.minimum(row + 1, win).astype(F32)
    pooled = s / count - u
    o_ref[0] = _dot(pooled, w_ref[...]) * scale_ref[...]


def _pool(u, pool_w, pool_scale):
    B, T, Wp = u.shape
    G, d = POOL_GROUPS, POOL_GROUP_DIM
    wbd = jnp.zeros((Wp, Wp), F32)
    for gi in range(G):
        wbd = wbd.at[gi * d:(gi + 1) * d, gi * d:(gi + 1) * d].set(pool_w[gi])
    return pl.pallas_call(
        _pool_kernel,
        grid=(B,),
        in_specs=[
            pl.BlockSpec((1, T, Wp), lambda b: (b, 0, 0)),
            pl.BlockSpec((Wp, Wp), lambda b: (0, 0)),
            pl.BlockSpec((1, Wp), lambda b: (0, 0)),
        ],
        out_specs=pl.BlockSpec((1, T, Wp), lambda b: (b, 0, 0)),
        out_shape=jax.ShapeDtypeStruct((B, T, Wp), F32),
        compiler_params=_cparams(("arbitrary",)),
        name="pool",
    )(u, wbd.astype(BF16), pool_scale.reshape(1, Wp))


def _route(logits_t, bias_col):
    E, tm = logits_t.shape
    per_group = E // N_GROUPS
    neg_inf = -jnp.inf
    scores = _sigmoid(logits_t)
    choice = scores + bias_col
    c3 = choice.reshape(N_GROUPS, per_group, tm)
    sub = lax.broadcasted_iota(jnp.int32, c3.shape, 1)
    m1 = jnp.max(c3, axis=1, keepdims=True)
    first = jnp.min(jnp.where(c3 == m1, sub, per_group), axis=1, keepdims=True)
    m2 = jnp.max(jnp.where(sub == first, neg_inf, c3), axis=1, keepdims=True)
    gs = m1 + m2
    gidx = lax.broadcasted_iota(jnp.int32, gs.shape, 0)
    grank = jnp.zeros(gs.shape, jnp.int32)
    for j in range(N_GROUPS):
        other = gs[j:j + 1]
        ahead = jnp.where(other > gs, 1, jnp.where((other == gs) & (gidx > j), 1, 0))
        grank = grank + ahead
    gmask = jnp.broadcast_to(grank < TOPK_GROUPS, c3.shape)
    masked = jnp.where(gmask, c3, neg_inf).reshape(E, tm)
    eidx = lax.broadcasted_iota(jnp.int32, (E, tm), 0)
    top = jnp.zeros((E, tm), F32)
    for _ in range(TOP_K):
        best = jnp.max(masked, axis=0, keepdims=True)
        first = jnp.min(jnp.where(masked == best, eidx, E), axis=0, keepdims=True)
        hit = eidx == first
        top = jnp.where(hit, scores, top)
        masked = jnp.where(hit, neg_inf, masked)
    return top / jnp.sum(top, axis=0, keepdims=True) * ROUTED_SCALE


def _outproj_kernel(yr_ref, yt_ref, yp_ref, x_ref, w1_ref, w2_ref, w3_ref, g1_ref, ng_ref,
                    sc_ref, sh_ref, wr_ref, rb_ref, x1_ref, h_ref, wc_ref):
    mixed = (_dot(yr_ref[...], w1_ref[...]) + _dot(yt_ref[...], w2_ref[...])
             + _dot(yp_ref[...], w3_ref[...]))
    x1 = x_ref[...] + g1_ref[0] * mixed
    x1_ref[...] = x1
    h = _modulated_norm(x1, ng_ref[...], sc_ref[0], sh_ref[0])
    h_ref[...] = h.astype(BF16)
    logits_t = lax.dot_general(wr_ref[...], h, (((1,), (1,)), ((), ())),
                               preferred_element_type=F32, precision=HIGHEST)
    wc_t = _route(logits_t, rb_ref[...])
    pad = jnp.zeros((LANES - N_EXPERTS, wc_t.shape[1]), F32)
    wc_ref[...] = jnp.concatenate([wc_t, pad], axis=0).T


def _outproj(yr, yt, yp, x2, w_out, gate1, ng, scale2, shift2, w_router, router_bias, T):
    N, D = x2.shape
    B = N // T
    tm = min(1024, T)
    per_b = T // tm
    W = WIDTH
    w1 = w_out[:W].astype(BF16)
    w2 = w_out[W:2 * W].astype(BF16)
    w3 = w_out[2 * W:].astype(BF16)
    row = lambda i: (i, 0)
    const = lambda i: (0, 0)
    bvec = lambda i: (i // per_b, 0, 0)
    return pl.pallas_call(
        _outproj_kernel,
        grid=(N // tm,),
        in_specs=[
            pl.BlockSpec((tm, W), row),
            pl.BlockSpec((tm, W), row),
            pl.BlockSpec((tm, POOL_WIDTH), row),
            pl.BlockSpec((tm, D), row),
            pl.BlockSpec(w1.shape, const),
            pl.BlockSpec(w2.shape, const),
            pl.BlockSpec(w3.shape, const),
            pl.BlockSpec((1, 1, D), bvec),
            pl.BlockSpec((1, D), const),
            pl.BlockSpec((1, 1, D), bvec),
            pl.BlockSpec((1, 1, D), bvec),
            pl.BlockSpec((N_EXPERTS, D), const),
            pl.BlockSpec((N_EXPERTS, 1), const),
        ],
        out_specs=[
            pl.BlockSpec((tm, D), row),
            pl.BlockSpec((tm, D), row),
            pl.BlockSpec((tm, LANES), row),
        ],
        out_shape=[
            jax.ShapeDtypeStruct((N, D), F32),
            jax.ShapeDtypeStruct((N, D), BF16),
            jax.ShapeDtypeStruct((N, LANES), F32),
        ],
        compiler_params=_cparams(("arbitrary",)),
        name="outproj_router",
    )(yr, yt, yp, x2, w1, w2, w3, gate1.reshape(B, 1, D), ng.reshape(1, D),
      scale2.reshape(B, 1, D), shift2.reshape(B, 1, D), w_router.T,
      router_bias.reshape(N_EXPERTS, 1))


def _slot_positions(wc):
    tm = wc.shape[0]
    sel = jnp.where(wc > 0.0, 1.0, 0.0).astype(BF16)
    ti = lax.broadcasted_iota(jnp.int32, (tm, tm), 0)
    si = lax.broadcasted_iota(jnp.int32, (tm, tm), 1)
    earlier = jnp.where(ti > si, 1.0, 0.0).astype(BF16)
    return sel, jnp.dot(earlier, sel, preferred_element_type=F32)


def _slot_rows(code_t, value_t, tm):
    slot = lax.broadcasted_iota(jnp.int32, (MOE_CAP, tm), 0).astype(F32)
    rows = []
    for e in range(N_EXPERTS):
        value = 1.0 if value_t is None else value_t[e:e + 1, :]
        rows.append(jnp.where(code_t[e:e + 1, :] == slot, value, 0.0).astype(BF16))
    return jnp.concatenate(rows, axis=0)


def _dispatch_kernel(h_ref, wc_ref, xs_ref, over_ref, cnt_ref):
    nb, ts, D = h_ref.shape
    tm = nb * ts
    wc = wc_ref[...].reshape(tm, LANES)
    sel, pos = _slot_positions(wc)
    over_ref[...] = jnp.where(pos >= MOE_CAP, wc, 0.0).reshape(nb, ts, LANES)
    cnt_ref[...] = jnp.sum(sel.astype(F32), axis=0, keepdims=True)
    code = jnp.where(wc > 0.0, pos, -1.0)
    onehot = _slot_rows(code.T, None, tm)
    xs_ref[...] = jnp.dot(onehot, h_ref[...].reshape(tm, D),
                          preferred_element_type=F32).astype(BF16)


def _dispatch(h, wc, B, T):
    N, D = h.shape
    ts = MOE_TILE_SEQ
    nt = T // ts
    L = N_EXPERTS * MOE_CAP
    tile = lambda i: (0, i, 0, 0)
    xs, over, cnt = pl.pallas_call(
        _dispatch_kernel,
        grid=(nt,),
        in_specs=[
            pl.BlockSpec((B, None, ts, D), tile),
            pl.BlockSpec((B, None, ts, LANES), tile),
        ],
        out_specs=[
            pl.BlockSpec((None, L, D), lambda i: (i, 0, 0)),
            pl.BlockSpec((B, None, ts, LANES), tile),
            pl.BlockSpec((None, 1, LANES), lambda i: (i, 0, 0)),
        ],
        out_shape=[
            jax.ShapeDtypeStruct((nt, L, D), BF16),
            jax.ShapeDtypeStruct((B, nt, ts, LANES), F32),
            jax.ShapeDtypeStruct((nt, 1, LANES), F32),
        ],
        compiler_params=_cparams(("arbitrary",)),
        name="moe_dispatch",
    )(h.reshape(B, nt, ts, D), wc.reshape(B, nt, ts, LANES))
    return xs, over.reshape(N, LANES), cnt.reshape(nt, LANES)[:, :N_EXPERTS]


def _slot_pieces(cap):
    bounds = [0] + [n for n in MOE_USED_STEPS if n < cap] + [cap]
    return list(zip(bounds[:-1], bounds[1:]))


def _experts_kernel(used_ref, *refs):
    pieces = _slot_pieces(refs[-4].shape[1])
    x_refs = refs[:len(pieces)]
    wg_ref, wu_ref, wd_ref, o_ref, wg_s, wu_s, wd_s = refs[len(pieces):]

    @pl.when(pl.program_id(1) == 0)
    def _():
        wg_s[...] = wg_ref[0, 0].astype(BF16)
        wu_s[...] = wu_ref[0, 0].astype(BF16)
        wd_s[...] = wd_ref[0, 0].astype(BF16)

    tg, cap, D = o_ref.shape
    used = used_ref[pl.program_id(0), pl.program_id(1)]

    def run(k):
        x = jnp.concatenate([x_refs[j][...].reshape(tg * (hi - lo), D)
                             for j, (lo, hi) in enumerate(pieces[:k + 1])], axis=0)
        hid = _silu(jnp.dot(x, wg_s[...], preferred_element_type=F32)) * jnp.dot(
            x, wu_s[...], preferred_element_type=F32)
        y = _dot(hid, wd_s[...]).astype(BF16)
        for lo, hi in pieces[:k + 1]:
            o_ref[:, lo:hi, :] = y[tg * lo:tg * hi].reshape(tg, hi - lo, D)
        if pieces[k][1] < cap:
            o_ref[:, pieces[k][1]:cap, :] = jnp.zeros((tg, cap - pieces[k][1], D), BF16)

    for k, (lo, hi) in enumerate(pieces):
        pl.when((used > lo) & (used <= hi) if hi < cap else used > lo)(functools.partial(run, k))

    @pl.when(used == 0)
    def _():
        o_ref[...] = jnp.zeros_like(o_ref)


def _experts(xs, cnt, wg, wu, wd, layer):
    nt, L, D = xs.shape
    _, E, _, Hd = wg.shape
    cap = L // E
    tg = math.gcd(nt, 16)
    used = jnp.minimum(jnp.max(cnt.reshape(nt // tg, tg, E), axis=1), cap).T.astype(jnp.int32)
    slots = pl.BlockSpec((tg, None, cap, D), lambda e, g, u: (g, e, 0, 0))

    def piece_spec(lo, hi):
        assert lo % (hi - lo) == 0

        def index(e, g, u):
            live = u[e, g] > lo
            return (jnp.where(live, g, 0), jnp.where(live, e, 0), lo // (hi - lo), 0)

        return pl.BlockSpec((tg, None, hi - lo, D), index)

    pieces = _slot_pieces(cap)
    xs4 = xs.reshape(nt, E, cap, D)
    grid_spec = pltpu.PrefetchScalarGridSpec(
        num_scalar_prefetch=1,
        grid=(E, nt // tg),
        in_specs=[piece_spec(lo, hi) for lo, hi in pieces] + [
            pl.BlockSpec((1, 1, D, Hd), lambda e, g, u: (layer, e, 0, 0)),
            pl.BlockSpec((1, 1, D, Hd), lambda e, g, u: (layer, e, 0, 0)),
            pl.BlockSpec((1, 1, Hd, D), lambda e, g, u: (layer, e, 0, 0)),
        ],
        out_specs=slots,
        scratch_shapes=[pltpu.VMEM((D, Hd), BF16), pltpu.VMEM((D, Hd), BF16),
                        pltpu.VMEM((Hd, D), BF16)],
    )
    return pl.pallas_call(
        _experts_kernel,
        grid_spec=grid_spec,
        out_shape=jax.ShapeDtypeStruct((nt, E, cap, D), BF16),
        compiler_params=_cparams(("arbitrary", "arbitrary")),
        name="moe_experts",
    )(used, *([xs4] * len(pieces)), wg, wu, wd).reshape(nt, L, D)


def _combine_kernel(y_ref, wc_ref, h_ref, x1_ref, g2_ref, sg_ref, su_ref, sd_ref, fg_ref,
                    *rest, final_norm):
    ex_ref = rest[0] if len(rest) == 2 else None
    o_ref = rest[-1]
    nb, ts, D = h_ref.shape
    tm = nb * ts
    wc = wc_ref[...].reshape(tm, LANES)
    _, pos = _slot_positions(wc)
    code = jnp.where(wc > 0.0, pos, -1.0)
    weighted = _slot_rows(code.T, wc.T, tm)
    routed = _dot_tn(weighted, y_ref[...])
    h = h_ref[...].reshape(tm, D)
    hid = _silu(jnp.dot(h, sg_ref[...], preferred_element_type=F32)) * jnp.dot(
        h, su_ref[...], preferred_element_type=F32)
    y = routed + _dot(hid, sd_ref[...])
    if ex_ref is not None:
        y = y + ex_ref[...].reshape(tm, D)
    gate = jnp.broadcast_to(g2_ref[...], (nb, ts, D)).reshape(tm, D)
    xo = x1_ref[...].reshape(tm, D) + gate * y
    if final_norm:
        ms = jnp.mean(xo * xo, axis=-1, keepdims=True)
        xo = xo * lax.rsqrt(ms + NORM_EPS) * fg_ref[...]
    o_ref[...] = xo.reshape(nb, ts, D)


def _combine(ys, wc, h, x1, extra, gate2, sg, su, sd, final_g, B, T, final_norm):
    N, D = x1.shape
    ts = MOE_TILE_SEQ
    nt = T // ts
    L = N_EXPERTS * MOE_CAP
    tile = lambda i: (0, i, 0, 0)
    const = lambda i: (0, 0)
    tok = lambda a: a.reshape(B, nt, ts, a.shape[-1])
    extras = [] if extra is None else [tok(extra)]
    out = pl.pallas_call(
        functools.partial(_combine_kernel, final_norm=final_norm),
        grid=(nt,),
        in_specs=[
            pl.BlockSpec((None, L, D), lambda i: (i, 0, 0)),
            pl.BlockSpec((B, None, ts, LANES), tile),
            pl.BlockSpec((B, None, ts, D), tile),
            pl.BlockSpec((B, None, ts, D), tile),
            pl.BlockSpec((B, 1, D), lambda i: (0, 0, 0)),
            pl.BlockSpec(sg.shape, const),
            pl.BlockSpec(su.shape, const),
            pl.BlockSpec(sd.shape, const),
            pl.BlockSpec((1, D), const),
        ] + [pl.BlockSpec((B, None, ts, D), tile) for _ in extras],
        out_specs=pl.BlockSpec((B, None, ts, D), tile),
        out_shape=jax.ShapeDtypeStruct((B, nt, ts, D), F32),
        compiler_params=_cparams(("arbitrary",)),
        name="moe_combine",
    )(ys, tok(wc), tok(h), tok(x1), gate2.reshape(B, 1, D), sg, su, sd,
      final_g.reshape(1, D), *extras)
    return out.reshape(N, D)


def _overflow_kernel(h_ref, wc_ref, wg_ref, wu_ref, wd_ref, o_ref):
    e = pl.program_id(1)

    @pl.when(e == 0)
    def _():
        o_ref[...] = jnp.zeros_like(o_ref)

    h = h_ref[...]
    lane = lax.broadcasted_iota(jnp.int32, wc_ref.shape, 1)
    w = jnp.sum(jnp.where(lane == e, wc_ref[...], 0.0), axis=1, keepdims=True)
    hid = _silu(jnp.dot(h, wg_ref[0], preferred_element_type=F32)) * jnp.dot(
        h, wu_ref[0], preferred_element_type=F32)
    o_ref[...] += _dot(hid * w, wd_ref[0])


def _overflow(h, wc_over, wg, wu, wd):
    N, D = h.shape
    tm = math.gcd(N, 1024)
    E, _, Hd = wg.shape
    row = lambda i, e: (i, 0)
    return pl.pallas_call(
        _overflow_kernel,
        grid=(N // tm, E),
        in_specs=[
            pl.BlockSpec((tm, D), row),
            pl.BlockSpec((tm, LANES), row),
            pl.BlockSpec((1, D, Hd), lambda i, e: (e, 0, 0)),
            pl.BlockSpec((1, D, Hd), lambda i, e: (e, 0, 0)),
            pl.BlockSpec((1, Hd, D), lambda i, e: (e, 0, 0)),
        ],
        out_specs=pl.BlockSpec((tm, D), row),
        out_shape=jax.ShapeDtypeStruct((N, D), F32),
        compiler_params=_cparams(("arbitrary", "arbitrary")),
        name="moe_overflow",
    )(h, wc_over, wg, wu, wd)


def _moe(h, wc, wg, wu, wd, layer, sg, su, sd, x1, gate2, final_g, B, T, final_norm):
    xs, wc_over, cnt = _dispatch(h, wc, B, T)
    ys = _experts(xs, cnt, wg, wu, wd, layer)
    def finish(extra):
        return _combine(ys, wc, h, x1, extra, gate2, sg, su, sd, final_g, B, T, final_norm)

    def with_overflow():
        return finish(_overflow(h, wc_over, wg[layer].astype(BF16), wu[layer].astype(BF16),
                                wd[layer].astype(BF16)))

    return lax.cond(jnp.any(wc_over != 0.0), with_overflow, lambda: finish(None))


def _split_w_in(w_in):
    w_rwkv = w_in[:, :RWKV_COLS]
    w_ret = w_in[:, RWKV_COLS:RWKV_COLS + RET_COLS]
    w_pool = w_in[:, RWKV_COLS + RET_COLS:]
    return w_rwkv.astype(BF16), w_ret.astype(BF16), w_pool.astype(BF16)


def kernel(x, c, norm1_g, norm2_g, w_ada, b_ada, w_in, w_out, rwkv_mu, rwkv_w_up, rwkv_w0,
           rwkv_a_up, rwkv_a0, rwkv_g_up, rwkv_k_k, rwkv_k_a, rwkv_r_k, rwkv_lnx_w, rwkv_lnx_b,
           pool_w, pool_scale, w_router, router_bias, we_gate, we_up, we_down,
           ws_gate, ws_up, ws_down, final_g):
    B, T, D = x.shape
    L = w_in.shape[0]
    N = B * T
    mod = _adaln(c, w_ada, b_ada)
    x2 = x.reshape(N, D)
    for l in range(L):
        shift1, scale1, gate1, shift2, scale2, gate2 = jnp.split(mod[l], 6, axis=-1)
        w1, w2, w3 = _split_w_in(w_in[l])
        p_rwkv, p_ret, p_pool = _inproj(x2, norm1_g[l], scale1, shift1, w1, w2, w3, T)
        y_rwkv = _rwkv(p_rwkv.reshape(B, T, -1), rwkv_mu[l], rwkv_w_up[l], rwkv_w0[l],
                       rwkv_a_up[l], rwkv_a0[l], rwkv_g_up[l], rwkv_k_k[l], rwkv_k_a[l],
                       rwkv_r_k[l], rwkv_lnx_w[l], rwkv_lnx_b[l])
        y_ret = _retention(p_ret.reshape(B, T, -1))
        y_pool = _pool(p_pool.reshape(B, T, -1), pool_w[l], pool_scale[l])
        x1, h, wc = _outproj(y_rwkv.reshape(N, -1), y_ret.reshape(N, -1), y_pool.reshape(N, -1),
                             x2, w_out[l], gate1, norm2_g[l], scale2, shift2,
                             w_router[l], router_bias[l], T)
        x2 = _moe(h, wc, we_gate, we_up, we_down, l,
                  ws_gate[l].astype(BF16), ws_up[l].astype(BF16),
                  ws_down[l].astype(BF16), x1, gate2, final_g, B, T,
                  final_norm=(l == L - 1))
    return x2.reshape(B, T, D)
```
